```python
import math
import jax, jax.numpy as jnp
from jax import lax
import numpy as np

D_MODEL = 1024
BATCH = 4
SEQ = 8192
DEPTH = 2

N_A_LAYERS = DEPTH // 2
N_B_LAYERS = DEPTH - N_A_LAYERS
D_FF = 2816
RWKV_HEAD = 64
RWKV_HEADS = D_MODEL // RWKV_HEAD
LORA_DECAY = 64
LORA_AAA = 64
LORA_GATE = 160
RWKV_LN_EPS = 64e-5
N_SHIFT_MIX = 6
DIFF_HEAD = 64
DIFF_HEADS = D_MODEL // (2 * DIFF_HEAD)
DIFF_V_HEAD = 2 * DIFF_HEAD
Q_BLOCK = 128
NORM_EPS = 1e-6
SUBLN_EPS = 1e-5

kernel_name = "rwkv7_diffattn_yoco_macaron"


def rms_norm(x, g, eps=NORM_EPS):
    xf = x.astype(jnp.float32)
    y = xf * lax.rsqrt(jnp.mean(xf * xf, axis=-1, keepdims=True) + eps)
    return (y * g.astype(jnp.float32)).astype(x.dtype)


def swiglu(h, w_in, w_out):
    gate, up = jnp.split(h @ w_in, 2, axis=-1)
    return (jax.nn.silu(gate) * up) @ w_out


def lambda_init(layer_idx):
    return 0.8 - 0.6 * math.exp(-0.3 * layer_idx)


def alibi_slopes(n_heads):
    return 2.0 ** (-8.0 * jnp.arange(1, n_heads + 1, dtype=jnp.float32) / n_heads)


def wkv7_scan(r, decay, k, v, a, b):
    B, T, H, N = r.shape

    def step(S, inp):
        r_t, w_t, k_t, v_t, a_t, b_t = inp
        sa = jnp.einsum('bhij,bhj->bhi', S, a_t)
        S = S * w_t[:, :, None, :] + sa[..., None] * b_t[:, :, None, :] + v_t[..., None] * k_t[:, :, None, :]
        return S, jnp.einsum('bhij,bhj->bhi', S, r_t)

    xs = [jnp.moveaxis(t.astype(jnp.float32), 1, 0) for t in (r, decay, k, v, a, b)]
    S0 = jnp.zeros((B, H, N, N), jnp.float32)
    _, y = lax.scan(step, S0, xs)
    return jnp.moveaxis(y, 0, 1)


def rwkv7_time_mix(h, mu, w_rkv, w0, w1, w2, a0, a1, a2, g1, g2, k_k, k_a, r_k, ln_g, ln_b, w_o):
    B, T, C = h.shape
    H, N = RWKV_HEADS, RWKV_HEAD
    h_prev = jnp.pad(h, ((0, 0), (1, 0), (0, 0)))[:, :-1]
    dx = h_prev - h
    r, k, v = jnp.einsum('sbtc,scd->sbtd', h[None] + dx[None] * mu[:3, None, None, :], w_rkv)
    xw = h + dx * mu[3]
    xa = h + dx * mu[4]
    xg = h + dx * mu[5]
    w_log = -jax.nn.softplus(-(w0 + jnp.tanh(xw @ w1) @ w2)) - 0.5
    decay = jnp.exp(-jnp.exp(w_log.astype(jnp.float32)))
    a = jax.nn.sigmoid(a0 + (xa @ a1) @ a2)
    g = jax.nn.sigmoid(xg @ g1) @ g2
    kk = (k * k_k).reshape(B, T, H, N).astype(jnp.float32)
    kk = kk * lax.rsqrt(jnp.maximum(jnp.sum(kk * kk, axis=-1, keepdims=True), 1e-24))
    k = k * (1 + (a - 1) * k_a)
    heads = lambda t: t.reshape(B, T, H, N)
    a_h = heads(a).astype(jnp.float32)
    y = wkv7_scan(heads(r), heads(decay), heads(k), heads(v), -kk, kk * a_h)
    mean = jnp.mean(y, axis=-1, keepdims=True)
    var = jnp.mean(jnp.square(y - mean), axis=-1, keepdims=True)
    y = ((y - mean) * lax.rsqrt(var + RWKV_LN_EPS)).reshape(B, T, C) * ln_g + ln_b
    bonus = jnp.sum(heads(r).astype(jnp.float32) * heads(k).astype(jnp.float32) * r_k, axis=-1, keepdims=True)
    y = y + (bonus * heads(v).astype(jnp.float32)).reshape(B, T, C)
    return (y.astype(h.dtype) * g) @ w_o


def shared_kv(x, kv_norm, w_kv, k_norm):
    B, T, _ = x.shape
    kv = rms_norm(x, kv_norm) @ w_kv
    k = rms_norm(kv[..., :D_MODEL].reshape(B, T, DIFF_HEADS, 2, DIFF_HEAD), k_norm).transpose(0, 2, 3, 1, 4)
    v = kv[..., D_MODEL:].reshape(B, T, DIFF_HEADS, DIFF_V_HEAD).transpose(0, 2, 1, 3)
    return k, v


def diff_attention(h, k_sh, v_sh, w_q, q_norm, lam, subln, w_o, lam_init):
    B, T, _ = h.shape
    H, d = DIFF_HEADS, DIFF_HEAD
    n_blk = T // Q_BLOCK
    q = rms_norm((h @ w_q).reshape(B, T, H, 2, d), q_norm)
    q = q.reshape(B, n_blk, Q_BLOCK, H, 2, d).transpose(1, 0, 3, 4, 2, 5)
    lamf = lam.astype(jnp.float32)
    lam_full = jnp.exp(jnp.sum(lamf[0] * lamf[1])) - jnp.exp(jnp.sum(lamf[2] * lamf[3])) + lam_init
    slopes = alibi_slopes(H)
    k_pos = jnp.arange(T, dtype=jnp.int32)
    scale = d ** -0.5

    def block(args):
        qb, start = args
        s = jnp.einsum('bhcqd,bhckd->bhcqk', qb, k_sh).astype(jnp.float32) * scale
        dist = (start + jnp.arange(Q_BLOCK, dtype=jnp.int32))[:, None] - k_pos[None, :]
        bias = -slopes[:, None, None] * dist.astype(jnp.float32)
        s = jnp.where(dist >= 0, s + bias[None, :, None], -jnp.inf)
        p = jax.nn.softmax(s, axis=-1)
        attn = p[:, :, 0] - lam_full * p[:, :, 1]
        return jnp.einsum('bhqk,bhkd->bhqd', attn.astype(v_sh.dtype), v_sh)

    starts = jnp.arange(n_blk, dtype=jnp.int32) * Q_BLOCK
    o = lax.map(block, (q, starts))
    o = rms_norm(o, subln, SUBLN_EPS) * (1.0 - lam_init)
    o = o.transpose(1, 0, 3, 2, 4).reshape(B, T, H * DIFF_V_HEAD)
    return o @ w_o


def setup_inputs(seed: int = 0) -> dict:
    key = jax.random.key(seed)
    ks = jax.random.split(key, 29)
    C, F = D_MODEL, D_FF
    nA, nB = N_A_LAYERS, N_B_LAYERS
    nrm = lambda k, shape, s: jax.random.normal(k, shape, jnp.float32) * s
    gain = lambda k, shape: 1.0 + nrm(k, shape, 0.02)
    return {
        "x": nrm(ks[0], (BATCH, SEQ, C), 1.0),
        "ffn_norm": gain(ks[1], (DEPTH, 2, C)),
        "ffn_w_in": nrm(ks[2], (DEPTH, 2, C, 2 * F), C ** -0.5),
        "ffn_w_out": nrm(ks[3], (DEPTH, 2, F, C), F ** -0.5),
        "mix_norm": gain(ks[4], (DEPTH, C)),
        "rwkv_mu": jax.random.uniform(ks[5], (nA, N_SHIFT_MIX, C), jnp.float32),
        "rwkv_w_rkv": nrm(ks[6], (nA, 3, C, C), C ** -0.5),
        "rwkv_w0": jax.random.uniform(ks[7], (nA, C), jnp.float32, -6.0, -1.0),
        "rwkv_w1": nrm(ks[8], (nA, C, LORA_DECAY), C ** -0.5),
        "rwkv_w2": nrm(ks[9], (nA, LORA_DECAY, C), 0.5 * LORA_DECAY ** -0.5),
        "rwkv_a0": nrm(ks[10], (nA, C), 0.1),
        "rwkv_a1": nrm(ks[11], (nA, C, LORA_AAA), C ** -0.5),
        "rwkv_a2": nrm(ks[12], (nA, LORA_AAA, C), 0.5 * LORA_AAA ** -0.5),
        "rwkv_g1": nrm(ks[13], (nA, C, LORA_GATE), C ** -0.5),
        "rwkv_g2": nrm(ks[14], (nA, LORA_GATE, C), LORA_GATE ** -0.5),
        "rwkv_k_k": 0.85 + nrm(ks[15], (nA, C), 0.02),
        "rwkv_k_a": gain(ks[16], (nA, C)),
        "rwkv_r_k": nrm(ks[17], (nA, RWKV_HEADS, RWKV_HEAD), 0.1),
        "rwkv_ln_g": gain(ks[18], (nA, C)),
        "rwkv_ln_b": nrm(ks[19], (nA, C), 0.02),
        "rwkv_w_o": nrm(ks[20], (nA, C, C), 0.5 * C ** -0.5),
        "kv_norm": gain(ks[21], (C,)),
        "w_kv": nrm(ks[22], (C, 2 * C), C ** -0.5),
        "k_norm": gain(ks[23], (DIFF_HEAD,)),
        "diff_w_q": nrm(ks[24], (nB, C, C), C ** -0.5),
        "diff_q_norm": gain(ks[25], (nB, DIFF_HEAD)),
        "diff_lambda": nrm(ks[26], (nB, 4, DIFF_HEAD), 0.1),
        "diff_subln": gain(ks[27], (nB, DIFF_V_HEAD)),
        "diff_w_o": nrm(ks[28], (nB, C, C), 0.5 * C ** -0.5),
    }


def reference(x, ffn_norm, ffn_w_in, ffn_w_out, mix_norm, rwkv_mu, rwkv_w_rkv, rwkv_w0, rwkv_w1, rwkv_w2,
              rwkv_a0, rwkv_a1, rwkv_a2, rwkv_g1, rwkv_g2, rwkv_k_k, rwkv_k_a, rwkv_r_k, rwkv_ln_g, rwkv_ln_b,
              rwkv_w_o, kv_norm, w_kv, k_norm, diff_w_q, diff_q_norm, diff_lambda, diff_subln, diff_w_o):
    k_sh = None
    v_sh = None
    for l in range(DEPTH):
        x = x + 0.5 * swiglu(rms_norm(x, ffn_norm[l, 0]), ffn_w_in[l, 0], ffn_w_out[l, 0])
        h = rms_norm(x, mix_norm[l])
        if l < N_A_LAYERS:
            i = l
            x = x + rwkv7_time_mix(h, rwkv_mu[i], rwkv_w_rkv[i], rwkv_w0[i], rwkv_w1[i], rwkv_w2[i],
                                   rwkv_a0[i], rwkv_a1[i], rwkv_a2[i], rwkv_g1[i], rwkv_g2[i],
                                   rwkv_k_k[i], rwkv_k_a[i], rwkv_r_k[i], rwkv_ln_g[i], rwkv_ln_b[i], rwkv_w_o[i])
        else:
            j = l - N_A_LAYERS
            x = x + diff_attention(h, k_sh, v_sh, diff_w_q[j], diff_q_norm[j], diff_lambda[j],
                                   diff_subln[j], diff_w_o[j], lambda_init(l))
        x = x + 0.5 * swiglu(rms_norm(x, ffn_norm[l, 1]), ffn_w_in[l, 1], ffn_w_out[l, 1])
        if l == N_A_LAYERS - 1:
            k_sh, v_sh = shared_kv(x, kv_norm, w_kv, k_norm)
    return x
```

```python
import functools
import math

import jax
import jax.numpy as jnp
from jax import lax
from jax.experimental import pallas as pl
from jax.experimental.pallas import tpu as pltpu

F32 = jnp.float32
BF16 = jnp.bfloat16

LANES = 128
HEAD = 64
PAIR = 2 * HEAD
CHUNK = 64
NORM_EPS = 1e-6
SUBLN_EPS = 1e-5
RWKV_LN_EPS = 64e-5
VMEM_LIMIT = 56 * 1024 * 1024
NEG_BIG = -1e30


def _cparams(sem):
    return pltpu.CompilerParams(dimension_semantics=sem, vmem_limit_bytes=VMEM_LIMIT)


def _const_spec(shape):
    nd = len(shape)
    return pl.BlockSpec(shape, lambda *_: (0,) * nd)


def _mm(a, b):
    return jnp.dot(a.astype(BF16), b.astype(BF16), preferred_element_type=F32)


def _mm_nt(a, b):
    return lax.dot_general(a.astype(BF16), b.astype(BF16), (((1,), (1,)), ((), ())),
                           preferred_element_type=F32)


def _mm_tn(a, b):
    return lax.dot_general(a.astype(BF16), b.astype(BF16), (((0,), (0,)), ((), ())),
                           preferred_element_type=F32)


def _rms(x, g, eps):
    return x * lax.rsqrt(jnp.mean(x * x, axis=-1, keepdims=True) + eps) * g


def _half_mask(shape):
    return lax.broadcasted_iota(jnp.int32, shape, len(shape) - 1) % PAIR < HEAD


def _pair_sum(x, first):
    s1 = jnp.sum(jnp.where(first, x, 0.0), axis=-1, keepdims=True)
    s2 = jnp.sum(jnp.where(first, 0.0, x), axis=-1, keepdims=True)
    return jnp.where(first, s1, s2)


def _ffn_kernel(x_ref, g_ref, win_ref, wout_ref, o_ref, h_sc, *, d_ff, tf):
    x = x_ref[...]
    xn = _rms(x, g_ref[...], NORM_EPS).astype(BF16)
    for f0 in range(0, d_ff, tf):
        gate = jnp.dot(xn, win_ref[:, f0:f0 + tf], preferred_element_type=F32)
        up = jnp.dot(xn, win_ref[:, d_ff + f0:d_ff + f0 + tf], preferred_element_type=F32)
        h_sc[:, f0:f0 + tf] = (gate * jax.nn.sigmoid(gate) * up).astype(BF16)
    y = jnp.dot(h_sc[...], wout_ref[...], preferred_element_type=F32)
    o_ref[...] = x + 0.5 * y


def _ffn(x, g, w_in, w_out, *, tm=512, tf=256):
    m, c = x.shape
    d_ff = w_out.shape[0]
    tm = min(tm, m)
    return pl.pallas_call(
        functools.partial(_ffn_kernel, d_ff=d_ff, tf=tf),
        grid=(m // tm,),
        in_specs=[pl.BlockSpec((tm, c), lambda i: (i, 0)),
                  _const_spec((1, c)), _const_spec((c, 2 * d_ff)), _const_spec((d_ff, c))],
        out_specs=pl.BlockSpec((tm, c), lambda i: (i, 0)),
        out_shape=jax.ShapeDtypeStruct((m, c), F32),
        scratch_shapes=[pltpu.VMEM((tm, d_ff), BF16)],
        compiler_params=_cparams(("parallel",)),
        name="ffn",
    )(x, g.reshape(1, c), w_in.astype(BF16), w_out.astype(BF16))


def _rwkv_pre_kernel(x_ref, xp_ref, ng_ref, mu_ref, wrkv_ref, w0_ref, w1_ref, w2_ref, a0_ref, a1_ref,
                     a2_ref, g1_ref, g2_ref, kk_ref, ka_ref,
                     r_out, lw_out, k_out, v_out, kk_out, a_out, g_out, *, blocks_per_seq):
    tm, c = x_ref.shape
    ng = ng_ref[...]
    h = _rms(x_ref[...], ng, NORM_EPS)
    prev = _rms(xp_ref[7:8, :], ng, NORM_EPS)
    prev = jnp.where(pl.program_id(0) % blocks_per_seq == 0, 0.0, prev)
    row = lax.broadcasted_iota(jnp.int32, (tm, c), 0)
    h_prev = jnp.where(row == 0, prev, pltpu.roll(h, 1, 0))
    dx = h_prev - h

    def mix(i):
        return h + dx * mu_ref[i:i + 1, :]

    r = _mm(mix(0), wrkv_ref[0])
    k = _mm(mix(1), wrkv_ref[1])
    v = _mm(mix(2), wrkv_ref[2])
    z = w0_ref[...] + _mm(jnp.tanh(_mm(mix(3), w1_ref[...])), w2_ref[...])
    w_log = -jax.nn.softplus(-z) - 0.5
    lw = -jnp.exp(w_log)
    a = jax.nn.sigmoid(a0_ref[...] + _mm(_mm(mix(4), a1_ref[...]), a2_ref[...]))
    g_out[...] = _mm(jax.nn.sigmoid(_mm(mix(5), g1_ref[...])), g2_ref[...])
    kk = k * kk_ref[...]
    k = k * (1.0 + (a - 1.0) * ka_ref[...])
    first = _half_mask((tm, PAIR))
    for p in range(c // PAIR):
        sl = slice(p * PAIR, (p + 1) * PAIR)
        kkp = kk[:, sl]
        ss = _pair_sum(kkp * kkp, first)
        r_out[0, p] = r[:, sl]
        lw_out[0, p] = lw[:, sl]
        k_out[0, p] = k[:, sl]
        v_out[0, p] = v[:, sl]
        kk_out[0, p] = kkp * lax.rsqrt(jnp.maximum(ss, 1e-24))
        a_out[0, p] = a[:, sl]


def _rwkv_pre(x, batch, ng, mu, w_rkv, w0, w1, w2, a0, a1, a2, g1, g2, k_k, k_a, *, tm=512):
    m, c = x.shape
    t = m // batch
    tm = min(tm, t)
    bps = t // tm
    npair = c // PAIR
    row = lambda a: a.reshape(1, c)
    pair_spec = pl.BlockSpec((1, npair, tm, PAIR), lambda i: (i // bps, 0, i % bps, 0))
    pair_shape = jax.ShapeDtypeStruct((batch, npair, t, PAIR), F32)
    ws = [w_rkv.astype(BF16), row(w0), w1.astype(BF16), w2.astype(BF16), row(a0), a1.astype(BF16),
          a2.astype(BF16), g1.astype(BF16), g2.astype(BF16), row(k_k), row(k_a)]
    return pl.pallas_call(
        functools.partial(_rwkv_pre_kernel, blocks_per_seq=bps),
        grid=(m // tm,),
        in_specs=[pl.BlockSpec((tm, c), lambda i: (i, 0)),
                  pl.BlockSpec((8, c), lambda i: (jnp.maximum(i * (tm // 8) - 1, 0), 0)),
                  _const_spec((1, c)), _const_spec(mu.shape)] + [_const_spec(w.shape) for w in ws],
        out_specs=[pair_spec] * 6 + [pl.BlockSpec((tm, c), lambda i: (i, 0))],
        out_shape=[pair_shape] * 6 + [jax.ShapeDtypeStruct((m, c), F32)],
        compiler_params=_cparams(("parallel",)),
        name="rwkv_pre",
    )(x, x, row(ng), mu, *ws)


def _stack(x, first):
    return jnp.concatenate([jnp.where(first, x, 0.0), jnp.where(first, 0.0, x)], axis=0)


def _wkv_kernel(r_ref, lw_ref, k_ref, v_ref, kk_ref, a_ref, rk_ref, lng_ref, lnb_ref, y_ref, s_sc):
    lb = r_ref.shape[2]
    nc = lb // CHUNK
    l2 = 2 * CHUNK

    @pl.when(pl.program_id(2) == 0)
    def _():
        s_sc[...] = jnp.zeros_like(s_sc)

    first = _half_mask((CHUNK, PAIR))
    ri = lax.broadcasted_iota(jnp.int32, (l2, l2), 0)
    ci = lax.broadcasted_iota(jnp.int32, (l2, l2), 1)
    same = (ri // CHUNK) == (ci // CHUNK)
    strict = same & (ri % CHUNK > ci % CHUNK)
    incl = same & (ri % CHUNK >= ci % CHUNK)
    eye = ri == ci

    bi = lax.broadcasted_iota(jnp.int32, (lb, lb), 0)
    bj = lax.broadcasted_iota(jnp.int32, (lb, lb), 1)
    tri = jnp.where(((bi // CHUNK) == (bj // CHUNK)) & (bj <= bi), 1.0, 0.0).astype(BF16)
    lw_all = lw_ref[0, 0]
    lw_hi = lw_all.astype(BF16)
    lw_lo = (lw_all - lw_hi.astype(F32)).astype(BF16)
    c_all = (jnp.dot(tri, lw_hi, preferred_element_type=F32)
             + jnp.dot(tri, lw_lo, preferred_element_type=F32))

    rk = rk_ref[0]
    lng = lng_ref[0]
    lnb = lnb_ref[0]
    s = s_sc[...]
    for ch in range(nc):
        sl = slice(ch * CHUNK, (ch + 1) * CHUNK)
        r = r_ref[0, 0, sl, :]
        lw = lw_all[sl]
        k = k_ref[0, 0, sl, :]
        v = v_ref[0, 0, sl, :]
        kk = kk_ref[0, 0, sl, :]
        a = a_ref[0, 0, sl, :]
        c = c_all[sl]
        c_last = c[CHUNK - 1:CHUNK, :]
        e_neg = jnp.exp(-c)
        e_last = jnp.exp(c_last - c)
        b = kk * a
        rt = _stack(r * jnp.exp(c), first)
        at = _stack(-kk * jnp.exp(c - lw), first)
        bt = _stack(b * e_neg, first)
        kt = _stack(k * e_neg, first)
        vs = _stack(v, first)
        bk_last = jnp.concatenate([_stack(b * e_last, first), _stack(k * e_last, first)], axis=0)

        ar = jnp.concatenate([at, rt], axis=0)
        gb = _mm_nt(ar, bt)
        gk = _mm_nt(ar, kt)
        n = jnp.where(strict, gb[:l2], 0.0)
        a_ak = jnp.where(strict, gk[:l2], 0.0)
        a_rbk = jnp.concatenate([jnp.where(incl, gb[l2:], 0.0), jnp.where(incl, gk[l2:], 0.0)], axis=1)

        tinv = jnp.where(eye, 1.0, 0.0) + n
        pw = n
        for _ in range(int(math.log2(CHUNK)) - 1):
            pw = _mm(pw, pw)
            tinv = tinv + _mm(tinv, pw)

        x = _mm(tinv, jnp.concatenate([at, _mm(a_ak, vs)], axis=1))
        z = jnp.concatenate([x, jnp.concatenate([jnp.zeros_like(vs), vs], axis=1)], axis=0)
        m1 = _mm(a_rbk, z)
        m2 = _mm_tn(z, bk_last)
        p = jnp.where(eye, jnp.exp(c_last), 0.0) + m2[:PAIR]
        q = m2[PAIR:]

        ys = _mm_nt(rt + m1[:, :PAIR], s) + m1[:, PAIR:]
        s = _mm(s, p) + q
        y = ys[:CHUNK] + ys[CHUNK:]

        mean = _pair_sum(y, first) * (1.0 / HEAD)
        d = y - mean
        var = _pair_sum(d * d, first) * (1.0 / HEAD)
        bonus = _pair_sum(r * k * rk, first)
        y_ref[0, 0, sl, :] = d * lax.rsqrt(var + RWKV_LN_EPS) * lng + lnb + bonus * v
    s_sc[...] = s


def _wkv_scan(r, lw, k, v, kk, a, r_k, ln_g, ln_b, *, lb=256):
    batch, npair, t, _ = r.shape
    lb = min(lb, t)
    seq_spec = pl.BlockSpec((1, 1, lb, PAIR), lambda b, p, i: (b, p, i, 0))
    par_spec = pl.BlockSpec((1, 1, PAIR), lambda b, p, i: (p, 0, 0))
    par = lambda x: x.reshape(npair, 1, PAIR)
    return pl.pallas_call(
        _wkv_kernel,
        grid=(batch, npair, t // lb),
        in_specs=[seq_spec] * 6 + [par_spec] * 3,
        out_specs=seq_spec,
        out_shape=jax.ShapeDtypeStruct(r.shape, F32),
        scratch_shapes=[pltpu.VMEM((PAIR, PAIR), F32)],
        compiler_params=_cparams(("parallel", "parallel", "arbitrary")),
        name="wkv_scan",
    )(r, lw, k, v, kk, a, par(r_k), par(ln_g), par(ln_b))


def _out_proj_kernel(*refs, gated):
    if gated:
        y_ref, g_ref, w_ref, x_ref, o_ref = refs
    else:
        y_ref, w_ref, x_ref, o_ref = refs
    y = jnp.concatenate([y_ref[0, p] for p in range(y_ref.shape[1])], axis=-1)
    if gated:
        y = y * g_ref[...]
    o_ref[...] = x_ref[...] + jnp.dot(y.astype(BF16), w_ref[...], preferred_element_type=F32)


def _out_proj(y, g, w_o, x, *, tm=512):
    batch, nblk, t, _ = y.shape
    m, c = x.shape
    tm = min(tm, t)
    bps = t // tm
    row_spec = pl.BlockSpec((tm, c), lambda i: (i, 0))
    gated = g is not None
    return pl.pallas_call(
        functools.partial(_out_proj_kernel, gated=gated),
        grid=(m // tm,),
        in_specs=([pl.BlockSpec((1, nblk, tm, LANES), lambda i: (i // bps, 0, i % bps, 0))]
                  + ([row_spec] if gated else []) + [_const_spec((c, c)), row_spec]),
        out_specs=row_spec,
        out_shape=jax.ShapeDtypeStruct((m, c), F32),
        compiler_params=_cparams(("parallel",)),
        name="out_proj",
    )(*([y] + ([g] if gated else []) + [w_o.astype(BF16), x]))


def _kv_proj_kernel(x_ref, ng_ref, w_ref, kg_ref, k_out, v_out, *, blocks_per_seq, n_heads):
    tm, c = x_ref.shape
    kv = _mm(_rms(x_ref[...], ng_ref[...], NORM_EPS), w_ref[...])
    first = _half_mask((tm, LANES))
    kg = kg_ref[...]
    lane = lax.broadcasted_iota(jnp.int32, (tm, LANES), 1)
    pos = (pl.program_id(0) % blocks_per_seq) * tm + lax.broadcasted_iota(jnp.int32, (tm, LANES), 0)
    base = jnp.where(lane == 0, (pos // HEAD * HEAD).astype(F32),
                     jnp.where(lane == 1, (pos % HEAD).astype(F32),
                               jnp.where(lane == 2, float(HEAD), 0.0)))
    for h in range(n_heads):
        sl = slice(h * LANES, (h + 1) * LANES)
        kh = kv[:, sl]
        ms = _pair_sum(kh * kh, first) * (1.0 / HEAD)
        kn = kh * lax.rsqrt(ms + NORM_EPS) * kg
        slope = 2.0 ** (-8.0 * (h + 1) / n_heads)
        k_out[0, h] = jnp.concatenate([kn.astype(BF16), (base * slope).astype(BF16)], axis=-1)
        v_out[0, h] = kv[:, c + h * LANES:c + (h + 1) * LANES].astype(BF16)


def _kv_proj(x, batch, ng, w_kv, k_norm, *, tm=512):
    m, c = x.shape
    t = m // batch
    tm = min(tm, t)
    bps = t // tm
    nh = c // LANES
    kg = jnp.concatenate([k_norm, k_norm]).reshape(1, LANES)
    idx = lambda i: (i // bps, 0, i % bps, 0)
    return pl.pallas_call(
        functools.partial(_kv_proj_kernel, blocks_per_seq=bps, n_heads=nh),
        grid=(m // tm,),
        in_specs=[pl.BlockSpec((tm, c), lambda i: (i, 0)), _const_spec((1, c)),
                  _const_spec((c, 2 * c)), _const_spec((1, LANES))],
        out_specs=[pl.BlockSpec((1, nh, tm, 2 * LANES), idx), pl.BlockSpec((1, nh, tm, LANES), idx)],
        out_shape=[jax.ShapeDtypeStruct((batch, nh, t, 2 * LANES), BF16),
                   jax.ShapeDtypeStruct((batch, nh, t, LANES), BF16)],
        compiler_params=_cparams(("parallel",)),
        name="kv_proj",
    )(x, ng.reshape(1, c), w_kv.astype(BF16), kg)


def _q_proj_kernel(x_ref, ng_ref, w_ref, qg_ref, q_out, *, n_heads):
    tm, c = x_ref.shape
    q = _mm(_rms(x_ref[...], ng_ref[...], NORM_EPS), w_ref[...])
    first = _half_mask((tm, LANES))
    qg = qg_ref[...] * (HEAD ** -0.5)
    for h in range(n_heads):
        qh = q[:, h * LANES:(h + 1) * LANES]
        ms = _pair_sum(qh * qh, first) * (1.0 / HEAD)
        q_out[0, h] = (qh * lax.rsqrt(ms + NORM_EPS) * qg).astype(BF16)


def _q_proj(x, batch, ng, w_q, q_norm, *, tm=512):
    m, c = x.shape
    t = m // batch
    tm = min(tm, t)
    bps = t // tm
    nh = c // LANES
    qg = jnp.concatenate([q_norm, q_norm]).reshape(1, LANES)
    return pl.pallas_call(
        functools.partial(_q_proj_kernel, n_heads=nh),
        grid=(m // tm,),
        in_specs=[pl.BlockSpec((tm, c), lambda i: (i, 0)), _const_spec((1, c)),
                  _const_spec((c, c)), _const_spec((1, LANES))],
        out_specs=pl.BlockSpec((1, nh, tm, LANES), lambda i: (i // bps, 0, i % bps, 0)),
        out_shape=jax.ShapeDtypeStruct((batch, nh, t, LANES), BF16),
        compiler_params=_cparams(("parallel",)),
        name="q_proj",
    )(x, ng.reshape(1, c), w_q.astype(BF16), qg)


def _diff_attn_kernel(q_ref, k_ref, v_ref, lam_ref, sg_ref, o_ref, m_sc, l_sc, acc_sc, *, lam_init):
    bq = q_ref.shape[2]
    i = pl.program_id(2)
    q = q_ref[0, 0]
    first = _half_mask((bq, LANES))
    lane = lax.broadcasted_iota(jnp.int32, (bq, LANES), 1)
    feat = jnp.where(lane < 2, 1.0, jnp.where(lane == 2, -(i * (bq // HEAD)).astype(F32), 0.0)).astype(BF16)
    zero = jnp.zeros_like(q)
    qs = jnp.concatenate([jnp.concatenate([jnp.where(first, q, zero), feat], axis=1),
                          jnp.concatenate([jnp.where(first, zero, q), feat], axis=1)], axis=0)

    m_sc[...] = jnp.full_like(m_sc, NEG_BIG)
    l_sc[...] = jnp.zeros_like(l_sc)
    acc_sc[...] = jnp.zeros_like(acc_sc)

    def step(j, masked):
        start = pl.multiple_of(j * bq, bq)
        s = lax.dot_general(qs, k_ref[0, 0, pl.ds(start, bq), :], (((1,), (1,)), ((), ())),
                            preferred_element_type=F32)
        if masked:
            row = lax.broadcasted_iota(jnp.int32, s.shape, 0) % bq
            col = lax.broadcasted_iota(jnp.int32, s.shape, 1)
            s = jnp.where(col <= row, s, NEG_BIG)
        m_prev = m_sc[...]
        m_new = jnp.maximum(m_prev, jnp.max(s, axis=-1, keepdims=True))
        p = jnp.exp(s - m_new)
        alpha = jnp.exp(m_prev - m_new)
        l_sc[...] = alpha * l_sc[...] + jnp.sum(p, axis=-1, keepdims=True)
        acc_sc[...] = alpha * acc_sc[...] + jnp.dot(p.astype(BF16), v_ref[0, 0, pl.ds(start, bq), :],
                                                    preferred_element_type=F32)
        m_sc[...] = m_new

    def body(j, carry):
        step(j, False)
        return carry

    lax.fori_loop(0, i, body, 0)
    step(i, True)

    o = acc_sc[...] / l_sc[...]
    lam = lam_ref[...]
    lam_full = (jnp.exp(jnp.sum(lam[0:1] * lam[1:2], axis=-1, keepdims=True))
                - jnp.exp(jnp.sum(lam[2:3] * lam[3:4], axis=-1, keepdims=True)) + lam_init)
    o = o[:bq] - lam_full * o[bq:]
    o_ref[0, 0] = (_rms(o, sg_ref[...], SUBLN_EPS) * (1.0 - lam_init)).astype(BF16)


def _diff_attn(q, k, v, lam, subln, lam_init, *, bq=512):
    batch, nh, t, _ = q.shape
    bq = min(bq, t)
    return pl.pallas_call(
        functools.partial(_diff_attn_kernel, lam_init=lam_init),
        grid=(batch, nh, t // bq),
        in_specs=[pl.BlockSpec((1, 1, bq, LANES), lambda b, h, i: (b, h, i, 0)),
                  pl.BlockSpec((1, 1, t, 2 * LANES), lambda b, h, i: (b, h, 0, 0)),
                  pl.BlockSpec((1, 1, t, LANES), lambda b, h, i: (b, h, 0, 0)),
                  _const_spec(lam.shape), _const_spec((1, LANES))],
        out_specs=pl.BlockSpec((1, 1, bq, LANES), lambda b, h, i: (b, h, i, 0)),
        out_shape=jax.ShapeDtypeStruct((batch, nh, t, LANES), BF16),
        scratch_shapes=[pltpu.VMEM((2 * bq, 1), F32), pltpu.VMEM((2 * bq, 1), F32),
                        pltpu.VMEM((2 * bq, LANES), F32)],
        compiler_params=_cparams(("parallel", "parallel", "arbitrary")),
        name="diff_attn",
    )(q, k, v, lam, subln.reshape(1, LANES))


def kernel(x, ffn_norm, ffn_w_in, ffn_w_out, mix_norm, rwkv_mu, rwkv_w_rkv, rwkv_w0, rwkv_w1, rwkv_w2, rwkv_a0, rwkv_a1, rwkv_a2, rwkv_g1, rwkv_g2, rwkv_k_k, rwkv_k_a, rwkv_r_k, rwkv_ln_g, rwkv_ln_b, rwkv_w_o, kv_norm, w_kv, k_norm, diff_w_q, diff_q_norm, diff_lambda, diff_subln, diff_w_o):
    batch, t, c = x.shape
    x = x.reshape(batch * t, c)

    x = _ffn(x, ffn_norm[0, 0], ffn_w_in[0, 0], ffn_w_out[0, 0])
    r, lw, k, v, kk, a, g = _rwkv_pre(x, batch, mix_norm[0], rwkv_mu[0], rwkv_w_rkv[0], rwkv_w0[0], rwkv_w1[0],
                                      rwkv_w2[0], rwkv_a0[0], rwkv_a1[0], rwkv_a2[0], rwkv_g1[0], rwkv_g2[0],
                                      rwkv_k_k[0], rwkv_k_a[0])
    y = _wkv_scan(r, lw, k, v, kk, a, rwkv_r_k[0], rwkv_ln_g[0], rwkv_ln_b[0])
    x = _out_proj(y, g, rwkv_w_o[0], x)
    x = _ffn(x, ffn_norm[0, 1], ffn_w_in[0, 1], ffn_w_out[0, 1])
    k_sh, v_sh = _kv_proj(x, batch, kv_norm, w_kv, k_norm)

    x = _ffn(x, ffn_norm[1, 0], ffn_w_in[1, 0], ffn_w_out[1, 0])
    q = _q_proj(x, batch, mix_norm[1], diff_w_q[0], diff_q_norm[0])
    lam_init = 0.8 - 0.6 * math.exp(-0.3 * 1)
    o = _diff_attn(q, k_sh, v_sh, diff_lambda[0], diff_subln[0], lam_init)
    x = _out_proj(o, None, diff_w_o[0], x)
    x = _ffn(x, ffn_norm[1, 1], ffn_w_in[1, 1], ffn_w_out[1, 1])
    return x.reshape(batch, t, c)
```

```python
import functools
import math

import jax
import jax.numpy as jnp
from jax import lax
from jax.experimental import pallas as pl
from jax.experimental.pallas import tpu as pltpu

F32 = jnp.float32
BF16 = jnp.bfloat16

LANES = 128
HEAD = 64
PAIR = 2 * HEAD
CHUNK = 64
ATT = 512
NORM_EPS = 1e-6
SUBLN_EPS = 1e-5
RWKV_LN_EPS = 64e-5
VMEM_LIMIT = 56 * 1024 * 1024
NEG_BIG = -1e30


def _cparams(sem):
    return pltpu.CompilerParams(dimension_semantics=sem, vmem_limit_bytes=VMEM_LIMIT)


def _const_spec(shape):
    nd = len(shape)
    return pl.BlockSpec(shape, lambda *_: (0,) * nd)


def _mm(a, b):
    return jnp.dot(a.astype(BF16), b.astype(BF16), preferred_element_type=F32)


def _mm_nt(a, b):
    return lax.dot_general(a.astype(BF16), b.astype(BF16), (((1,), (1,)), ((), ())),
                           preferred_element_type=F32)


def _mm_tn(a, b):
    return lax.dot_general(a.astype(BF16), b.astype(BF16), (((0,), (0,)), ((), ())),
                           preferred_element_type=F32)


def _rms(x, g, eps):
    return x * lax.rsqrt(jnp.mean(x * x, axis=-1, keepdims=True) + eps) * g


def _half_mask(shape):
    return lax.broadcasted_iota(jnp.int32, shape, len(shape) - 1) % PAIR < HEAD


def _pair_sum(x, first):
    s1 = jnp.sum(jnp.where(first, x, 0.0), axis=-1, keepdims=True)
    s2 = jnp.sum(jnp.where(first, 0.0, x), axis=-1, keepdims=True)
    return jnp.where(first, s1, s2)


def _ffn_kernel(x_ref, g_ref, win_ref, wout_ref, o_ref, h_sc, *, d_ff, tf):
    x = x_ref[...]
    xn = _rms(x, g_ref[...], NORM_EPS).astype(BF16)
    for f0 in range(0, d_ff, tf):
        gate = jnp.dot(xn, win_ref[:, f0:f0 + tf], preferred_element_type=F32)
        up = jnp.dot(xn, win_ref[:, d_ff + f0:d_ff + f0 + tf], preferred_element_type=F32)
        h_sc[:, f0:f0 + tf] = (gate * jax.nn.sigmoid(gate) * up).astype(BF16)
    y = jnp.dot(h_sc[...], wout_ref[...], preferred_element_type=F32)
    o_ref[...] = x + 0.5 * y


def _ffn(x, g, w_in, w_out, *, tm=512, tf=256):
    m, c = x.shape
    d_ff = w_out.shape[0]
    tm = min(tm, m)
    return pl.pallas_call(
        functools.partial(_ffn_kernel, d_ff=d_ff, tf=tf),
        grid=(m // tm,),
        in_specs=[pl.BlockSpec((tm, c), lambda i: (i, 0)),
                  _const_spec((1, c)), _const_spec((c, 2 * d_ff)), _const_spec((d_ff, c))],
        out_specs=pl.BlockSpec((tm, c), lambda i: (i, 0)),
        out_shape=jax.ShapeDtypeStruct((m, c), F32),
        scratch_shapes=[pltpu.VMEM((tm, d_ff), BF16)],
        compiler_params=_cparams(("parallel",)),
        name="ffn",
    )(x, g.reshape(1, c), w_in.astype(BF16), w_out.astype(BF16))


def _rwkv_pre_kernel(x_ref, xp_ref, ng_ref, mu_ref, wrkv_ref, w0_ref, w1_ref, w2_ref, a0_ref, a1_ref,
                     a2_ref, g1_ref, g2_ref, kk_ref, ka_ref,
                     r_out, lw_out, k_out, v_out, kk_out, a_out, g_out, *, blocks_per_seq):
    tm, c = x_ref.shape
    ng = ng_ref[...]
    h = _rms(x_ref[...], ng, NORM_EPS)
    prev = _rms(xp_ref[7:8, :], ng, NORM_EPS)
    prev = jnp.where(pl.program_id(0) % blocks_per_seq == 0, 0.0, prev)
    row = lax.broadcasted_iota(jnp.int32, (tm, c), 0)
    h_prev = jnp.where(row == 0, prev, pltpu.roll(h, 1, 0))
    dx = h_prev - h

    def mix(i):
        return h + dx * mu_ref[i:i + 1, :]

    r = _mm(mix(0), wrkv_ref[0])
    k = _mm(mix(1), wrkv_ref[1])
    v = _mm(mix(2), wrkv_ref[2])
    z = w0_ref[...] + _mm(jnp.tanh(_mm(mix(3), w1_ref[...])), w2_ref[...])
    w_log = -jax.nn.softplus(-z) - 0.5
    lw = -jnp.exp(w_log)
    a = jax.nn.sigmoid(a0_ref[...] + _mm(_mm(mix(4), a1_ref[...]), a2_ref[...]))
    g_out[...] = _mm(jax.nn.sigmoid(_mm(mix(5), g1_ref[...])), g2_ref[...])
    kk = k * kk_ref[...]
    k = k * (1.0 + (a - 1.0) * ka_ref[...])
    first = _half_mask((tm, PAIR))
    for p in range(c // PAIR):
        sl = slice(p * PAIR, (p + 1) * PAIR)
        kkp = kk[:, sl]
        ss = _pair_sum(kkp * kkp, first)
        r_out[0, p] = r[:, sl]
        lw_out[0, p] = lw[:, sl]
        k_out[0, p] = k[:, sl]
        v_out[0, p] = v[:, sl]
        kk_out[0, p] = kkp * lax.rsqrt(jnp.maximum(ss, 1e-24))
        a_out[0, p] = a[:, sl]


def _rwkv_pre(x, batch, ng, mu, w_rkv, w0, w1, w2, a0, a1, a2, g1, g2, k_k, k_a, *, tm=512):
    m, c = x.shape
    t = m // batch
    tm = min(tm, t)
    bps = t // tm
    npair = c // PAIR
    row = lambda a: a.reshape(1, c)
    pair_spec = pl.BlockSpec((1, npair, tm, PAIR), lambda i: (i // bps, 0, i % bps, 0))
    pair_shape = jax.ShapeDtypeStruct((batch, npair, t, PAIR), F32)
    ws = [w_rkv.astype(BF16), row(w0), w1.astype(BF16), w2.astype(BF16), row(a0), a1.astype(BF16),
          a2.astype(BF16), g1.astype(BF16), g2.astype(BF16), row(k_k), row(k_a)]
    return pl.pallas_call(
        functools.partial(_rwkv_pre_kernel, blocks_per_seq=bps),
        grid=(m // tm,),
        in_specs=[pl.BlockSpec((tm, c), lambda i: (i, 0)),
                  pl.BlockSpec((8, c), lambda i: (jnp.maximum(i * (tm // 8) - 1, 0), 0)),
                  _const_spec((1, c)), _const_spec(mu.shape)] + [_const_spec(w.shape) for w in ws],
        out_specs=[pair_spec] * 6 + [pl.BlockSpec((tm, c), lambda i: (i, 0))],
        out_shape=[pair_shape] * 6 + [jax.ShapeDtypeStruct((m, c), F32)],
        compiler_params=_cparams(("parallel",)),
        name="rwkv_pre",
    )(x, x, row(ng), mu, *ws)


def _stack(x, first):
    return jnp.concatenate([jnp.where(first, x, 0.0), jnp.where(first, 0.0, x)], axis=0)


def _wkv_kernel(r_ref, lw_ref, k_ref, v_ref, kk_ref, a_ref, rk_ref, lng_ref, lnb_ref, y_ref, s_sc):
    lb = r_ref.shape[2]
    nc = lb // CHUNK
    l2 = 2 * CHUNK

    @pl.when(pl.program_id(2) == 0)
    def _():
        s_sc[...] = jnp.zeros_like(s_sc)

    first = _half_mask((CHUNK, PAIR))
    ri = lax.broadcasted_iota(jnp.int32, (l2, l2), 0)
    ci = lax.broadcasted_iota(jnp.int32, (l2, l2), 1)
    same = (ri // CHUNK) == (ci // CHUNK)
    strict = same & (ri % CHUNK > ci % CHUNK)
    incl = same & (ri % CHUNK >= ci % CHUNK)
    eye = ri == ci

    bi = lax.broadcasted_iota(jnp.int32, (lb, lb), 0)
    bj = lax.broadcasted_iota(jnp.int32, (lb, lb), 1)
    tri = jnp.where(((bi // CHUNK) == (bj // CHUNK)) & (bj <= bi), 1.0, 0.0).astype(BF16)
    lw_all = lw_ref[0, 0]
    lw_hi = lw_all.astype(BF16)
    lw_lo = (lw_all - lw_hi.astype(F32)).astype(BF16)
    c_all = (jnp.dot(tri, lw_hi, preferred_element_type=F32)
             + jnp.dot(tri, lw_lo, preferred_element_type=F32))

    chunks = range(nc)
    sls = [slice(ch * CHUNK, (ch + 1) * CHUNK) for ch in chunks]
    rt, at, bt, kt, vs, bk_last, g_last = [], [], [], [], [], [], []
    for sl in sls:
        r = r_ref[0, 0, sl, :]
        k = k_ref[0, 0, sl, :]
        kk = kk_ref[0, 0, sl, :]
        c = c_all[sl]
        c_last = c[CHUNK - 1:CHUNK, :]
        e_neg = jnp.exp(-c)
        e_last = jnp.exp(c_last - c)
        b = kk * a_ref[0, 0, sl, :]
        rt.append(_stack(r * jnp.exp(c), first))
        at.append(_stack(-kk * jnp.exp(c - lw_all[sl]), first))
        bt.append(_stack(b * e_neg, first))
        kt.append(_stack(k * e_neg, first))
        vs.append(_stack(v_ref[0, 0, sl, :], first))
        bk_last.append(jnp.concatenate([_stack(b * e_last, first), _stack(k * e_last, first)], axis=0))
        g_last.append(jnp.exp(c_last))

    ar = [jnp.concatenate([at[ch], rt[ch]], axis=0) for ch in chunks]
    gb = [_mm_nt(ar[ch], bt[ch]) for ch in chunks]
    gk = [_mm_nt(ar[ch], kt[ch]) for ch in chunks]
    n = [jnp.where(strict, gb[ch][:l2], 0.0) for ch in chunks]
    a_ak = [jnp.where(strict, gk[ch][:l2], 0.0) for ch in chunks]
    a_rbk = [jnp.concatenate([jnp.where(incl, gb[ch][l2:], 0.0), jnp.where(incl, gk[ch][l2:], 0.0)], axis=1)
             for ch in chunks]
    aakv = [_mm(a_ak[ch], vs[ch]) for ch in chunks]

    ident = jnp.where(eye, 1.0, 0.0)
    tinv = [ident + n[ch] for ch in chunks]
    pw = n
    for _ in range(int(math.log2(CHUNK)) - 1):
        pw = [_mm(pw[ch], pw[ch]) for ch in chunks]
        tinv = [tinv[ch] + _mm(tinv[ch], pw[ch]) for ch in chunks]

    x = [_mm(tinv[ch], jnp.concatenate([at[ch], aakv[ch]], axis=1)) for ch in chunks]
    z = [jnp.concatenate([x[ch], jnp.concatenate([jnp.zeros_like(vs[ch]), vs[ch]], axis=1)], axis=0)
         for ch in chunks]
    m1 = [_mm(a_rbk[ch], z[ch]) for ch in chunks]
    m2 = [_mm_tn(z[ch], bk_last[ch]) for ch in chunks]

    rk = rk_ref[0]
    lng = lng_ref[0]
    lnb = lnb_ref[0]
    s = s_sc[...]
    for ch in chunks:
        p = jnp.where(eye, g_last[ch], 0.0) + m2[ch][:PAIR]
        ys = _mm_nt(rt[ch] + m1[ch][:, :PAIR], s) + m1[ch][:, PAIR:]
        s = _mm(s, p) + m2[ch][PAIR:]
        y = ys[:CHUNK] + ys[CHUNK:]

        sl = sls[ch]
        mean = _pair_sum(y, first) * (1.0 / HEAD)
        d = y - mean
        var = _pair_sum(d * d, first) * (1.0 / HEAD)
        bonus = _pair_sum(r_ref[0, 0, sl, :] * k_ref[0, 0, sl, :] * rk, first)
        y_ref[0, 0, sl, :] = d * lax.rsqrt(var + RWKV_LN_EPS) * lng + lnb + bonus * v_ref[0, 0, sl, :]
    s_sc[...] = s


def _wkv_scan(r, lw, k, v, kk, a, r_k, ln_g, ln_b, *, lb=512):
    batch, npair, t, _ = r.shape
    lb = min(lb, t)
    seq_spec = pl.BlockSpec((1, 1, lb, PAIR), lambda b, p, i: (b, p, i, 0))
    par_spec = pl.BlockSpec((1, 1, PAIR), lambda b, p, i: (p, 0, 0))
    par = lambda x: x.reshape(npair, 1, PAIR)
    return pl.pallas_call(
        _wkv_kernel,
        grid=(batch, npair, t // lb),
        in_specs=[seq_spec] * 6 + [par_spec] * 3,
        out_specs=seq_spec,
        out_shape=jax.ShapeDtypeStruct(r.shape, F32),
        scratch_shapes=[pltpu.VMEM((PAIR, PAIR), F32)],
        compiler_params=_cparams(("parallel", "parallel", "arbitrary")),
        name="wkv_scan",
    )(r, lw, k, v, kk, a, par(r_k), par(ln_g), par(ln_b))


def _out_proj_kernel(*refs, gated):
    if gated:
        y_ref, g_ref, w_ref, x_ref, o_ref = refs
    else:
        y_ref, w_ref, x_ref, o_ref = refs
    y = jnp.concatenate([y_ref[0, p] for p in range(y_ref.shape[1])], axis=-1)
    if gated:
        y = y * g_ref[...]
    o_ref[...] = x_ref[...] + jnp.dot(y.astype(BF16), w_ref[...], preferred_element_type=F32)


def _out_proj(y, g, w_o, x, *, tm=512):
    batch, nblk, t, _ = y.shape
    m, c = x.shape
    tm = min(tm, t)
    bps = t // tm
    row_spec = pl.BlockSpec((tm, c), lambda i: (i, 0))
    gated = g is not None
    return pl.pallas_call(
        functools.partial(_out_proj_kernel, gated=gated),
        grid=(m // tm,),
        in_specs=([pl.BlockSpec((1, nblk, tm, LANES), lambda i: (i // bps, 0, i % bps, 0))]
                  + ([row_spec] if gated else []) + [_const_spec((c, c)), row_spec]),
        out_specs=row_spec,
        out_shape=jax.ShapeDtypeStruct((m, c), F32),
        compiler_params=_cparams(("parallel",)),
        name="out_proj",
    )(*([y] + ([g] if gated else []) + [w_o.astype(BF16), x]))


def _kv_proj_kernel(x_ref, ng_ref, w_ref, kg_ref, k_out, v_out, *, blocks_per_seq, n_heads):
    tm, c = x_ref.shape
    kv = _mm(_rms(x_ref[...], ng_ref[...], NORM_EPS), w_ref[...])
    first = _half_mask((tm, LANES))
    kg = kg_ref[...]
    lane = lax.broadcasted_iota(jnp.int32, (tm, LANES), 1)
    pos = (pl.program_id(0) % blocks_per_seq) * tm + lax.broadcasted_iota(jnp.int32, (tm, LANES), 0)
    base = jnp.where(lane == 0, (pos // HEAD * HEAD).astype(F32),
                     jnp.where(lane == 1, (pos % HEAD).astype(F32),
                               jnp.where(lane == 2, float(HEAD), 0.0)))
    for h in range(n_heads):
        sl = slice(h * LANES, (h + 1) * LANES)
        kh = kv[:, sl]
        ms = _pair_sum(kh * kh, first) * (1.0 / HEAD)
        kn = kh * lax.rsqrt(ms + NORM_EPS) * kg
        slope = 2.0 ** (-8.0 * (h + 1) / n_heads)
        k_out[0, h] = jnp.concatenate([kn.astype(BF16), (base * slope).astype(BF16)], axis=-1)
        v_out[0, h, 0] = kv[:, c + h * LANES:c + (h + 1) * LANES].T.astype(BF16)


def _kv_proj(x, batch, ng, w_kv, k_norm):
    m, c = x.shape
    t = m // batch
    tm = min(ATT, t)
    bps = t // tm
    nh = c // LANES
    kg = jnp.concatenate([k_norm, k_norm]).reshape(1, LANES)
    idx = lambda i: (i // bps, 0, i % bps, 0)
    return pl.pallas_call(
        functools.partial(_kv_proj_kernel, blocks_per_seq=bps, n_heads=nh),
        grid=(m // tm,),
        in_specs=[pl.BlockSpec((tm, c), lambda i: (i, 0)), _const_spec((1, c)),
                  _const_spec((c, 2 * c)), _const_spec((1, LANES))],
        out_specs=[pl.BlockSpec((1, nh, tm, 2 * LANES), idx),
                   pl.BlockSpec((1, nh, 1, LANES, tm), lambda i: (i // bps, 0, i % bps, 0, 0))],
        out_shape=[jax.ShapeDtypeStruct((batch, nh, t, 2 * LANES), BF16),
                   jax.ShapeDtypeStruct((batch, nh, t // tm, LANES, tm), BF16)],
        compiler_params=_cparams(("parallel",)),
        name="kv_proj",
    )(x, ng.reshape(1, c), w_kv.astype(BF16), kg)


def _q_proj_kernel(x_ref, ng_ref, w_ref, qg_ref, q_out, *, n_heads):
    tm, c = x_ref.shape
    q = _mm(_rms(x_ref[...], ng_ref[...], NORM_EPS), w_ref[...])
    first = _half_mask((tm, LANES))
    qg = qg_ref[...] * (HEAD ** -0.5)
    for h in range(n_heads):
        qh = q[:, h * LANES:(h + 1) * LANES]
        ms = _pair_sum(qh * qh, first) * (1.0 / HEAD)
        q_out[0, h] = (qh * lax.rsqrt(ms + NORM_EPS) * qg).astype(BF16)


def _q_proj(x, batch, ng, w_q, q_norm, *, tm=512):
    m, c = x.shape
    t = m // batch
    tm = min(tm, t)
    bps = t // tm
    nh = c // LANES
    qg = jnp.concatenate([q_norm, q_norm]).reshape(1, LANES)
    return pl.pallas_call(
        functools.partial(_q_proj_kernel, n_heads=nh),
        grid=(m // tm,),
        in_specs=[pl.BlockSpec((tm, c), lambda i: (i, 0)), _const_spec((1, c)),
                  _const_spec((c, c)), _const_spec((1, LANES))],
        out_specs=pl.BlockSpec((1, nh, tm, LANES), lambda i: (i // bps, 0, i % bps, 0)),
        out_shape=jax.ShapeDtypeStruct((batch, nh, t, LANES), BF16),
        compiler_params=_cparams(("parallel",)),
        name="q_proj",
    )(x, ng.reshape(1, c), w_q.astype(BF16), qg)


def _sublane_all(op, x):
    for shift in (4, 2, 1):
        x = op(x, pltpu.roll(x, shift, 0))
    return x


def _diff_attn_kernel(q_ref, k_ref, vt_ref, lam_ref, sg_ref, o_ref, m_sc, l_sc, acc_sc, sa_sc, sb_sc, *,
                      lam_init, cg):
    bq = q_ref.shape[2]
    dv = vt_ref.shape[3]
    ncg = 2 * bq // cg
    i = pl.program_id(2)
    q = q_ref[0, 0]
    first = _half_mask((bq, LANES))
    lane = lax.broadcasted_iota(jnp.int32, (bq, LANES), 1)
    feat = jnp.where(lane < 2, 1.0, jnp.where(lane == 2, -(i * (bq // HEAD)).astype(F32), 0.0)).astype(BF16)
    zero = jnp.zeros_like(q)
    qs = jnp.concatenate([jnp.concatenate([jnp.where(first, q, zero), feat], axis=1),
                          jnp.concatenate([jnp.where(first, zero, q), feat], axis=1)], axis=0)

    m_sc[...] = jnp.full_like(m_sc, NEG_BIG)
    l_sc[...] = jnp.zeros_like(l_sc)
    acc_sc[...] = jnp.zeros_like(acc_sc)

    groups = range(ncg)

    def scores(j, buf):
        kt = k_ref[0, 0, pl.ds(pl.multiple_of(j * bq, bq), bq), :]
        for g in groups:
            buf[g] = lax.dot_general(kt, qs[g * cg:(g + 1) * cg], (((1,), (1,)), ((), ())),
                                     preferred_element_type=F32)

    def tile(j, buf, diag):
        q0 = [(g * cg) % bq for g in groups]
        nk = [min(bq, q0[g] + cg) if diag else bq for g in groups]
        p, alpha = [], []
        for g in groups:
            sg = buf[g, :nk[g], :]
            if diag:
                key = lax.broadcasted_iota(jnp.int32, sg.shape, 0)
                qry = lax.broadcasted_iota(jnp.int32, sg.shape, 1) + q0[g]
                sg = jnp.where(key <= qry, sg, NEG_BIG)
            sg = sg.reshape(nk[g] // 8, 8, cg)
            m_prev = m_sc[g]
            m_new = jnp.maximum(m_prev, _sublane_all(jnp.maximum, jnp.max(sg, axis=0)))
            pg = jnp.exp(sg - m_new[None])
            ag = jnp.exp(m_prev - m_new)
            l_sc[g] = ag * l_sc[g] + jnp.sum(pg, axis=0)
            m_sc[g] = m_new
            p.append(pg.reshape(nk[g], cg).astype(BF16))
            alpha.append(ag)
        pv = [jnp.dot(vt_ref[0, 0, j, :, :nk[g]], p[g], preferred_element_type=F32) for g in groups]
        for g in groups:
            acc_sc[g] = (alpha[g][None] * acc_sc[g].reshape(dv // 8, 8, cg) + pv[g].reshape(dv // 8, 8, cg)
                         ).reshape(dv, cg)

    scores(0, sa_sc)

    def body(jj, carry):
        j = 2 * jj
        scores(j + 1, sb_sc)
        tile(j, sa_sc, False)
        scores(j + 2, sa_sc)
        tile(j + 1, sb_sc, False)
        return carry

    lax.fori_loop(0, i // 2, body, 0)

    @pl.when(i % 2 == 0)
    def _():
        tile(i, sa_sc, True)

    @pl.when(i % 2 == 1)
    def _():
        scores(i, sb_sc)
        tile(i - 1, sa_sc, False)
        tile(i, sb_sc, True)

    o_t = []
    for g in range(ncg):
        l = _sublane_all(jnp.add, l_sc[g])
        o_t.append((acc_sc[g].reshape(dv // 8, 8, cg) / l[None]).reshape(dv, cg))
    half = ncg // 2
    o1 = jnp.concatenate(o_t[:half], axis=1)
    o2 = jnp.concatenate(o_t[half:], axis=1)
    lam = lam_ref[...]
    lam_full = (jnp.exp(jnp.sum(lam[0:1] * lam[1:2], axis=-1, keepdims=True))
                - jnp.exp(jnp.sum(lam[2:3] * lam[3:4], axis=-1, keepdims=True)) + lam_init)
    o = (o1 - lam_full * o2).T
    o_ref[0, 0] = (_rms(o, sg_ref[...], SUBLN_EPS) * (1.0 - lam_init)).astype(BF16)


def _diff_attn(q, k, vt, lam, subln, lam_init):
    batch, nh, t, _ = q.shape
    bq = vt.shape[-1]
    cg = min(256, bq)
    ncg = 2 * bq // cg
    return pl.pallas_call(
        functools.partial(_diff_attn_kernel, lam_init=lam_init, cg=cg),
        grid=(batch, nh, t // bq),
        in_specs=[pl.BlockSpec((1, 1, bq, LANES), lambda b, h, i: (b, h, i, 0)),
                  pl.BlockSpec((1, 1, t, 2 * LANES), lambda b, h, i: (b, h, 0, 0)),
                  pl.BlockSpec((1, 1, t // bq, LANES, bq), lambda b, h, i: (b, h, 0, 0, 0)),
                  _const_spec(lam.shape), _const_spec((1, LANES))],
        out_specs=pl.BlockSpec((1, 1, bq, LANES), lambda b, h, i: (b, h, i, 0)),
        out_shape=jax.ShapeDtypeStruct((batch, nh, t, LANES), BF16),
        scratch_shapes=[pltpu.VMEM((ncg, 8, cg), F32), pltpu.VMEM((ncg, 8, cg), F32),
                        pltpu.VMEM((ncg, LANES, cg), F32),
                        pltpu.VMEM((ncg, bq, cg), F32), pltpu.VMEM((ncg, bq, cg), F32)],
        compiler_params=_cparams(("parallel", "parallel", "arbitrary")),
        name="diff_attn",
    )(q, k, vt, lam, subln.reshape(1, LANES))


def kernel(x, ffn_norm, ffn_w_in, ffn_w_out, mix_norm, rwkv_mu, rwkv_w_rkv, rwkv_w0, rwkv_w1, rwkv_w2, rwkv_a0, rwkv_a1, rwkv_a2, rwkv_g1, rwkv_g2, rwkv_k_k, rwkv_k_a, rwkv_r_k, rwkv_ln_g, rwkv_ln_b, rwkv_w_o, kv_norm, w_kv, k_norm, diff_w_q, diff_q_norm, diff_lambda, diff_subln, diff_w_o):
    batch, t, c = x.shape
    x = x.reshape(batch * t, c)

    x = _ffn(x, ffn_norm[0, 0], ffn_w_in[0, 0], ffn_w_out[0, 0])
    r, lw, k, v, kk, a, g = _rwkv_pre(x, batch, mix_norm[0], rwkv_mu[0], rwkv_w_rkv[0], rwkv_w0[0], rwkv_w1[0],
                                      rwkv_w2[0], rwkv_a0[0], rwkv_a1[0], rwkv_a2[0], rwkv_g1[0], rwkv_g2[0],
                                      rwkv_k_k[0], rwkv_k_a[0])
    y = _wkv_scan(r, lw, k, v, kk, a, rwkv_r_k[0], rwkv_ln_g[0], rwkv_ln_b[0])
    x = _out_proj(y, g, rwkv_w_o[0], x)
    x = _ffn(x, ffn_norm[0, 1], ffn_w_in[0, 1], ffn_w_out[0, 1])
    k_sh, v_sh = _kv_proj(x, batch, kv_norm, w_kv, k_norm)

    x = _ffn(x, ffn_norm[1, 0], ffn_w_in[1, 0], ffn_w_out[1, 0])
    q = _q_proj(x, batch, mix_norm[1], diff_w_q[0], diff_q_norm[0])
    lam_init = 0.8 - 0.6 * math.exp(-0.3 * 1)
    o = _diff_attn(q, k_sh, v_sh, diff_lambda[0], diff_subln[0], lam_init)
    x = _out_proj(o, None, diff_w_o[0], x)
    x = _ffn(x, ffn_norm[1, 1], ffn_w_in[1, 1], ffn_w_out[1, 1])
    return x.reshape(batch, t, c)
```

```python
import functools
import math

import jax
import jax.numpy as jnp
from jax import lax
from jax.experimental import pallas as pl
from jax.experimental.pallas import tpu as pltpu

F32 = jnp.float32
BF16 = jnp.bfloat16

LANES = 128
HEAD = 64
PAIR = 2 * HEAD
CHUNK = 64
ATT = 512
NORM_EPS = 1e-6
SUBLN_EPS = 1e-5
RWKV_LN_EPS = 64e-5
VMEM_LIMIT = 56 * 1024 * 1024
NEG_BIG = -1e30
BF16_ROWS = 16
VT_ROWS = LANES + BF16_ROWS


def _bf16_parts(x, n):
    parts = []
    for _ in range(n):
        m, e = math.frexp(x)
        p = math.ldexp(round(m * 256.0) / 256.0, e)
        parts.append(p)
        x -= p
    return tuple(parts)


LOG2E = math.log2(math.e)
LOG2E_PARTS = _bf16_parts(LOG2E, 3)


def _cparams(sem):
    return pltpu.CompilerParams(dimension_semantics=sem, vmem_limit_bytes=VMEM_LIMIT)


def _const_spec(shape):
    nd = len(shape)
    return pl.BlockSpec(shape, lambda *_: (0,) * nd)


def _mm(a, b):
    return jnp.dot(a.astype(BF16), b.astype(BF16), preferred_element_type=F32)


def _mm_nt(a, b):
    return lax.dot_general(a.astype(BF16), b.astype(BF16), (((1,), (1,)), ((), ())),
                           preferred_element_type=F32)


def _mm_tn(a, b):
    return lax.dot_general(a.astype(BF16), b.astype(BF16), (((0,), (0,)), ((), ())),
                           preferred_element_type=F32)


def _rms(x, g, eps):
    return x * lax.rsqrt(jnp.mean(x * x, axis=-1, keepdims=True) + eps) * g


def _half_mask(shape):
    return lax.broadcasted_iota(jnp.int32, shape, len(shape) - 1) % PAIR < HEAD


def _pair_sum(x, first):
    s1 = jnp.sum(jnp.where(first, x, 0.0), axis=-1, keepdims=True)
    s2 = jnp.sum(jnp.where(first, 0.0, x), axis=-1, keepdims=True)
    return jnp.where(first, s1, s2)


def _ffn_kernel(x_ref, g_ref, win_ref, wout_ref, o_ref, h_sc, *, d_ff, tf):
    x = x_ref[...]
    xn = _rms(x, g_ref[...], NORM_EPS).astype(BF16)
    for f0 in range(0, d_ff, tf):
        gate = jnp.dot(xn, win_ref[:, f0:f0 + tf], preferred_element_type=F32)
        up = jnp.dot(xn, win_ref[:, d_ff + f0:d_ff + f0 + tf], preferred_element_type=F32)
        h_sc[:, f0:f0 + tf] = (gate * jax.nn.sigmoid(gate) * up).astype(BF16)
    y = jnp.dot(h_sc[...], wout_ref[...], preferred_element_type=F32)
    o_ref[...] = x + 0.5 * y


def _ffn(x, g, w_in, w_out, *, tm=512, tf=256):
    m, c = x.shape
    d_ff = w_out.shape[0]
    tm = min(tm, m)
    return pl.pallas_call(
        functools.partial(_ffn_kernel, d_ff=d_ff, tf=tf),
        grid=(m // tm,),
        in_specs=[pl.BlockSpec((tm, c), lambda i: (i, 0)),
                  _const_spec((1, c)), _const_spec((c, 2 * d_ff)), _const_spec((d_ff, c))],
        out_specs=pl.BlockSpec((tm, c), lambda i: (i, 0)),
        out_shape=jax.ShapeDtypeStruct((m, c), F32),
        scratch_shapes=[pltpu.VMEM((tm, d_ff), BF16)],
        compiler_params=_cparams(("parallel",)),
        name="ffn",
    )(x, g.reshape(1, c), w_in.astype(BF16), w_out.astype(BF16))


def _rwkv_pre_kernel(x_ref, xp_ref, ng_ref, mu_ref, wrkv_ref, w0_ref, w1_ref, w2_ref, a0_ref, a1_ref,
                     a2_ref, g1_ref, g2_ref, kk_ref, ka_ref,
                     r_out, lw_out, k_out, v_out, kk_out, a_out, g_out, *, blocks_per_seq):
    tm, c = x_ref.shape
    ng = ng_ref[...]
    h = _rms(x_ref[...], ng, NORM_EPS)
    prev = _rms(xp_ref[7:8, :], ng, NORM_EPS)
    prev = jnp.where(pl.program_id(0) % blocks_per_seq == 0, 0.0, prev)
    row = lax.broadcasted_iota(jnp.int32, (tm, c), 0)
    h_prev = jnp.where(row == 0, prev, pltpu.roll(h, 1, 0))
    dx = h_prev - h

    def mix(i):
        return h + dx * mu_ref[i:i + 1, :]

    r = _mm(mix(0), wrkv_ref[0])
    k = _mm(mix(1), wrkv_ref[1])
    v = _mm(mix(2), wrkv_ref[2])
    z = w0_ref[...] + _mm(jnp.tanh(_mm(mix(3), w1_ref[...])), w2_ref[...])
    w_log = -jax.nn.softplus(-z) - 0.5
    lw = -jnp.exp(w_log)
    a = jax.nn.sigmoid(a0_ref[...] + _mm(_mm(mix(4), a1_ref[...]), a2_ref[...]))
    g_out[...] = _mm(jax.nn.sigmoid(_mm(mix(5), g1_ref[...])), g2_ref[...])
    kk = k * kk_ref[...]
    k = k * (1.0 + (a - 1.0) * ka_ref[...])
    first = _half_mask((tm, PAIR))
    for p in range(c // PAIR):
        sl = slice(p * PAIR, (p + 1) * PAIR)
        kkp = kk[:, sl]
        ss = _pair_sum(kkp * kkp, first)
        r_out[0, p] = r[:, sl]
        lw_out[0, p] = lw[:, sl]
        k_out[0, p] = k[:, sl]
        v_out[0, p] = v[:, sl]
        kk_out[0, p] = kkp * lax.rsqrt(jnp.maximum(ss, 1e-24))
        a_out[0, p] = a[:, sl]


def _rwkv_pre(x, batch, ng, mu, w_rkv, w0, w1, w2, a0, a1, a2, g1, g2, k_k, k_a, *, tm=512):
    m, c = x.shape
    t = m // batch
    tm = min(tm, t)
    bps = t // tm
    npair = c // PAIR
    row = lambda a: a.reshape(1, c)
    pair_spec = pl.BlockSpec((1, npair, tm, PAIR), lambda i: (i // bps, 0, i % bps, 0))
    pair_shape = jax.ShapeDtypeStruct((batch, npair, t, PAIR), F32)
    ws = [w_rkv.astype(BF16), row(w0), w1.astype(BF16), w2.astype(BF16), row(a0), a1.astype(BF16),
          a2.astype(BF16), g1.astype(BF16), g2.astype(BF16), row(k_k), row(k_a)]
    return pl.pallas_call(
        functools.partial(_rwkv_pre_kernel, blocks_per_seq=bps),
        grid=(m // tm,),
        in_specs=[pl.BlockSpec((tm, c), lambda i: (i, 0)),
                  pl.BlockSpec((8, c), lambda i: (jnp.maximum(i * (tm // 8) - 1, 0), 0)),
                  _const_spec((1, c)), _const_spec(mu.shape)] + [_const_spec(w.shape) for w in ws],
        out_specs=[pair_spec] * 6 + [pl.BlockSpec((tm, c), lambda i: (i, 0))],
        out_shape=[pair_shape] * 6 + [jax.ShapeDtypeStruct((m, c), F32)],
        compiler_params=_cparams(("parallel",)),
        name="rwkv_pre",
    )(x, x, row(ng), mu, *ws)


def _stack(x, first):
    return jnp.concatenate([jnp.where(first, x, 0.0), jnp.where(first, 0.0, x)], axis=0)


def _wkv_kernel(r_ref, lw_ref, k_ref, v_ref, kk_ref, a_ref, rk_ref, lng_ref, lnb_ref, y_ref,
                s_sc, carry_sc, bv_sc, *, nblk):
    lb = r_ref.shape[2]
    nc = lb // CHUNK
    l2 = 2 * CHUNK
    i = pl.program_id(2)

    first = _half_mask((CHUNK, PAIR))
    ri = lax.broadcasted_iota(jnp.int32, (l2, l2), 0)
    ci = lax.broadcasted_iota(jnp.int32, (l2, l2), 1)
    same = (ri // CHUNK) == (ci // CHUNK)
    strict = same & (ri % CHUNK > ci % CHUNK)
    incl = same & (ri % CHUNK >= ci % CHUNK)
    eye = ri == ci
    ident = jnp.where(eye, 1.0, 0.0)
    rk = rk_ref[0]
    lng = lng_ref[0]
    lnb = lnb_ref[0]
    chs = range(nc)

    def prepare_stages():
        w = {}

        def prep():
            tri = jnp.where(lax.broadcasted_iota(jnp.int32, (CHUNK, CHUNK), 1)
                            <= lax.broadcasted_iota(jnp.int32, (CHUNK, CHUNK), 0), 1.0, 0.0).astype(BF16)
            lw_wide = jnp.concatenate([lw_ref[0, 0, ch * CHUNK:(ch + 1) * CHUNK, :] for ch in chs], axis=1)
            lw_hi = lw_wide.astype(BF16)
            lw_lo = (lw_wide - lw_hi.astype(F32)).astype(BF16)
            c_wide = (jnp.dot(tri, lw_hi, preferred_element_type=F32)
                      + jnp.dot(tri, lw_lo, preferred_element_type=F32))
            for ch in chs:
                sl = slice(ch * CHUNK, (ch + 1) * CHUNK)
                r = r_ref[0, 0, sl, :]
                k = k_ref[0, 0, sl, :]
                v = v_ref[0, 0, sl, :]
                kk = kk_ref[0, 0, sl, :]
                c = c_wide[:, ch * PAIR:(ch + 1) * PAIR]
                c_last = c[CHUNK - 1:CHUNK, :]
                e_neg = jnp.exp(-c)
                e_last = jnp.exp(c_last - c)
                b = kk * a_ref[0, 0, sl, :]
                rt = _stack(r * jnp.exp(c), first)
                at = _stack(-kk * jnp.exp(c - lw_ref[0, 0, sl, :]), first)
                w[ch] = dict(
                    rt=rt, at=at, vs=_stack(v, first), g_last=jnp.exp(c_last),
                    bonus=_pair_sum(r * k * rk, first) * v,
                    ar=jnp.concatenate([at, rt], axis=0),
                    bkt=jnp.concatenate([_stack(b * e_neg, first), _stack(k * e_neg, first)], axis=0),
                    bk_last=jnp.concatenate([_stack(b * e_last, first), _stack(k * e_last, first)], axis=0))

        def couplings():
            for ch in chs:
                d = w[ch]
                g = _mm_nt(d["ar"], d["bkt"])
                d["n"] = jnp.where(strict, g[:l2, :l2], 0.0)
                d["a_ak"] = jnp.where(strict, g[:l2, l2:], 0.0)
                d["a_rbk"] = jnp.concatenate([jnp.where(incl, g[l2:, :l2], 0.0),
                                              jnp.where(incl, g[l2:, l2:], 0.0)], axis=1)

        def start_inverse():
            for ch in chs:
                d = w[ch]
                d["aakv"] = _mm(d["a_ak"], d["vs"])
                d["tinv"] = ident + d["n"]
                d["pw"] = d["n"]

        def square():
            for ch in chs:
                w[ch]["pw"] = _mm(w[ch]["pw"], w[ch]["pw"])

        def extend():
            for ch in chs:
                w[ch]["tinv"] = w[ch]["tinv"] + _mm(w[ch]["tinv"], w[ch]["pw"])

        def solve():
            for ch in chs:
                d = w[ch]
                x = _mm(d["tinv"], jnp.concatenate([d["at"], d["aakv"]], axis=1))
                d["z"] = jnp.concatenate([x, jnp.concatenate([jnp.zeros_like(d["vs"]), d["vs"]], axis=1)],
                                         axis=0)

        def finish():
            for ch in chs:
                d = w[ch]
                m1 = _mm(d["a_rbk"], d["z"])
                m2 = _mm_tn(d["z"], d["bk_last"])
                carry_sc[ch, 0] = d["rt"] + m1[:, :PAIR]
                carry_sc[ch, 1] = m1[:, PAIR:]
                carry_sc[ch, 2] = jnp.where(eye, d["g_last"], 0.0) + m2[:PAIR]
                carry_sc[ch, 3] = m2[PAIR:]
                bv_sc[ch] = d["bonus"]

        rounds = int(math.log2(CHUNK)) - 1
        return [prep, couplings, start_inverse] + [square, extend] * rounds + [solve, finish]

    def advance(ch, s):
        ys = _mm_nt(carry_sc[ch, 0], s) + carry_sc[ch, 1]
        s = _mm(s, carry_sc[ch, 2]) + carry_sc[ch, 3]
        y = ys[:CHUNK] + ys[CHUNK:]
        mean = _pair_sum(y, first) * (1.0 / HEAD)
        d = y - mean
        var = _pair_sum(d * d, first) * (1.0 / HEAD)
        y_ref[0, 0, ch * CHUNK:(ch + 1) * CHUNK, :] = d * lax.rsqrt(var + RWKV_LN_EPS) * lng + lnb + bv_sc[ch]
        return s

    @pl.when(i == 0)
    def _():
        s_sc[...] = jnp.zeros_like(s_sc)
        for stage in prepare_stages():
            stage()

    @pl.when((i > 0) & (i < nblk))
    def _():
        s = s_sc[...]
        todo = list(chs)
        stages = prepare_stages()
        per_stage = -(-nc // (len(stages) - 2))
        for n_stage, stage in enumerate(stages):
            if n_stage == len(stages) - 1:
                while todo:
                    s = advance(todo.pop(0), s)
            stage()
            if 0 < n_stage < len(stages) - 1:
                for _ in range(min(per_stage, len(todo))):
                    s = advance(todo.pop(0), s)
        s_sc[...] = s

    @pl.when(i == nblk)
    def _():
        s = s_sc[...]
        for ch in chs:
            s = advance(ch, s)
        s_sc[...] = s


def _wkv_scan(r, lw, k, v, kk, a, r_k, ln_g, ln_b, *, lb=1024):
    batch, npair, t, _ = r.shape
    lb = min(lb, t)
    nblk = t // lb
    nc = lb // CHUNK
    in_spec = pl.BlockSpec((1, 1, lb, PAIR), lambda b, p, i: (b, p, jnp.minimum(i, nblk - 1), 0))
    out_spec = pl.BlockSpec((1, 1, lb, PAIR), lambda b, p, i: (b, p, jnp.maximum(i - 1, 0), 0))
    par_spec = pl.BlockSpec((1, 1, PAIR), lambda b, p, i: (p, 0, 0))
    par = lambda x: x.reshape(npair, 1, PAIR)
    return pl.pallas_call(
        functools.partial(_wkv_kernel, nblk=nblk),
        grid=(batch, npair, nblk + 1),
        in_specs=[in_spec] * 6 + [par_spec] * 3,
        out_specs=out_spec,
        out_shape=jax.ShapeDtypeStruct(r.shape, F32),
        scratch_shapes=[pltpu.VMEM((PAIR, PAIR), F32), pltpu.VMEM((nc, 4, PAIR, PAIR), F32),
                        pltpu.VMEM((nc, CHUNK, PAIR), F32)],
        compiler_params=_cparams(("parallel", "parallel", "arbitrary")),
        name="wkv_scan",
    )(r, lw, k, v, kk, a, par(r_k), par(ln_g), par(ln_b))


def _out_proj_kernel(*refs, gated):
    if gated:
        y_ref, g_ref, w_ref, x_ref, o_ref = refs
    else:
        y_ref, w_ref, x_ref, o_ref = refs
    y = jnp.concatenate([y_ref[0, p] for p in range(y_ref.shape[1])], axis=-1)
    if gated:
        y = y * g_ref[...]
    o_ref[...] = x_ref[...] + jnp.dot(y.astype(BF16), w_ref[...], preferred_element_type=F32)


def _out_proj(y, g, w_o, x, *, tm=512):
    batch, nblk, t, _ = y.shape
    m, c = x.shape
    tm = min(tm, t)
    bps = t // tm
    row_spec = pl.BlockSpec((tm, c), lambda i: (i, 0))
    gated = g is not None
    return pl.pallas_call(
        functools.partial(_out_proj_kernel, gated=gated),
        grid=(m // tm,),
        in_specs=([pl.BlockSpec((1, nblk, tm, LANES), lambda i: (i // bps, 0, i % bps, 0))]
                  + ([row_spec] if gated else []) + [_const_spec((c, c)), row_spec]),
        out_specs=row_spec,
        out_shape=jax.ShapeDtypeStruct((m, c), F32),
        compiler_params=_cparams(("parallel",)),
        name="out_proj",
    )(*([y] + ([g] if gated else []) + [w_o.astype(BF16), x]))


def _kv_proj_kernel(x_ref, ng_ref, w_ref, kg_ref, k_out, v_out, *, blocks_per_seq, n_heads):
    tm, c = x_ref.shape
    kv = _mm(_rms(x_ref[...], ng_ref[...], NORM_EPS), w_ref[...])
    first = _half_mask((tm, LANES))
    kg = kg_ref[...]
    lane = lax.broadcasted_iota(jnp.int32, (tm, LANES), 1)
    pos = lax.broadcasted_iota(jnp.int32, (tm, LANES), 0)
    base = jnp.where(lane >= 2 * len(LOG2E_PARTS), 0.0,
                     jnp.where(lane % 2 == 0, (pos // HEAD * HEAD).astype(F32), (pos % HEAD).astype(F32)))
    ones = jnp.ones((VT_ROWS - LANES, tm), BF16)
    for h in range(n_heads):
        sl = slice(h * LANES, (h + 1) * LANES)
        kh = kv[:, sl]
        ms = _pair_sum(kh * kh, first) * (1.0 / HEAD)
        kn = kh * lax.rsqrt(ms + NORM_EPS) * kg
        slope = 2.0 ** (-8.0 * (h + 1) / n_heads)
        k_out[0, h] = jnp.concatenate([kn.astype(BF16), (base * slope).astype(BF16)], axis=-1)
        vt = kv[:, c + h * LANES:c + (h + 1) * LANES].T.astype(BF16)
        v_out[0, h, 0] = jnp.concatenate([vt, ones], axis=0)


def _kv_proj(x, batch, ng, w_kv, k_norm):
    m, c = x.shape
    t = m // batch
    tm = min(ATT, t)
    bps = t // tm
    nh = c // LANES
    kg = jnp.concatenate([k_norm, k_norm]).reshape(1, LANES)
    idx = lambda i: (i // bps, 0, i % bps, 0)
    return pl.pallas_call(
        functools.partial(_kv_proj_kernel, blocks_per_seq=bps, n_heads=nh),
        grid=(m // tm,),
        in_specs=[pl.BlockSpec((tm, c), lambda i: (i, 0)), _const_spec((1, c)),
                  _const_spec((c, 2 * c)), _const_spec((1, LANES))],
        out_specs=[pl.BlockSpec((1, nh, tm, 2 * LANES), idx),
                   pl.BlockSpec((1, nh, 1, VT_ROWS, tm), lambda i: (i // bps, 0, i % bps, 0, 0))],
        out_shape=[jax.ShapeDtypeStruct((batch, nh, t, 2 * LANES), BF16),
                   jax.ShapeDtypeStruct((batch, nh, t // tm, VT_ROWS, tm), BF16)],
        compiler_params=_cparams(("parallel",)),
        name="kv_proj",
    )(x, ng.reshape(1, c), w_kv.astype(BF16), kg)


def _q_proj_kernel(x_ref, ng_ref, w_ref, qg_ref, q_out, *, n_heads):
    tm, c = x_ref.shape
    q = _mm(_rms(x_ref[...], ng_ref[...], NORM_EPS), w_ref[...])
    first = _half_mask((tm, LANES))
    qg = qg_ref[...] * (HEAD ** -0.5 * LOG2E)
    for h in range(n_heads):
        qh = q[:, h * LANES:(h + 1) * LANES]
        ms = _pair_sum(qh * qh, first) * (1.0 / HEAD)
        q_out[0, h] = (qh * lax.rsqrt(ms + NORM_EPS) * qg).astype(BF16)


def _q_proj(x, batch, ng, w_q, q_norm, *, tm=512):
    m, c = x.shape
    t = m // batch
    tm = min(tm, t)
    bps = t // tm
    nh = c // LANES
    qg = jnp.concatenate([q_norm, q_norm]).reshape(1, LANES)
    return pl.pallas_call(
        functools.partial(_q_proj_kernel, n_heads=nh),
        grid=(m // tm,),
        in_specs=[pl.BlockSpec((tm, c), lambda i: (i, 0)), _const_spec((1, c)),
                  _const_spec((c, c)), _const_spec((1, LANES))],
        out_specs=pl.BlockSpec((1, nh, tm, LANES), lambda i: (i // bps, 0, i % bps, 0)),
        out_shape=jax.ShapeDtypeStruct((batch, nh, t, LANES), BF16),
        compiler_params=_cparams(("parallel",)),
        name="q_proj",
    )(x, ng.reshape(1, c), w_q.astype(BF16), qg)


def _sublane_all(op, x):
    for shift in (4, 2, 1):
        x = op(x, pltpu.roll(x, shift, 0))
    return x


def _diff_attn_kernel(q_ref, k_ref, vt_ref, lam_ref, sg_ref, o_ref, m_sc, acc_sc, sa_sc, sb_sc, *,
                      lam_init, cg, n_heads):
    bq = q_ref.shape[2]
    dv = LANES
    rows = vt_ref.shape[3]
    ncg = 2 * bq // cg
    i = pl.program_id(2)
    q = q_ref[0, 0]
    first = _half_mask((bq, LANES))
    lane = lax.broadcasted_iota(jnp.int32, (bq, LANES), 1)
    feat = jnp.zeros((bq, LANES), F32)
    for n, part in enumerate(LOG2E_PARTS):
        feat = jnp.where(lane // 2 == n, part, feat)
    feat = feat.astype(BF16)
    zero = jnp.zeros_like(q)
    qs = jnp.concatenate([jnp.concatenate([jnp.where(first, q, zero), feat], axis=1),
                          jnp.concatenate([jnp.where(first, zero, q), feat], axis=1)], axis=0)
    head = (pl.program_id(1) + 1).astype(F32)
    block_bias = jnp.exp2(jnp.full((8, cg), -8.0 / n_heads, F32) * head) * (LOG2E * bq)

    m_sc[...] = jnp.full_like(m_sc, NEG_BIG)
    acc_sc[...] = jnp.zeros_like(acc_sc)

    groups = range(ncg)

    def scores(j, buf):
        kt = k_ref[0, 0, pl.ds(pl.multiple_of(j * bq, bq), bq), :]
        for g in groups:
            buf[g] = lax.dot_general(kt, qs[g * cg:(g + 1) * cg], (((1,), (1,)), ((), ())),
                                     preferred_element_type=F32)

    def tile(j, buf, diag):
        q0 = [(g * cg) % bq for g in groups]
        nk = [min(bq, q0[g] + cg) if diag else bq for g in groups]
        offset = 0.0 if diag else block_bias * (j - i).astype(F32)
        p, alpha = [], []
        for g in groups:
            sg = buf[g, :nk[g], :]
            if diag:
                key = lax.broadcasted_iota(jnp.int32, sg.shape, 0)
                qry = lax.broadcasted_iota(jnp.int32, sg.shape, 1) + q0[g]
                sg = jnp.where(key <= qry, sg, NEG_BIG)
            sg = sg.reshape(nk[g] // 8, 8, cg)
            m_prev = m_sc[g]
            m_new = jnp.maximum(m_prev, _sublane_all(jnp.maximum, jnp.max(sg, axis=0)) + offset)
            p.append(jnp.exp2(sg - (m_new - offset)[None]).reshape(nk[g], cg).astype(BF16))
            alpha.append(jnp.exp2(m_prev - m_new))
            m_sc[g] = m_new
        pv = [jnp.dot(vt_ref[0, 0, j, :, :nk[g]], p[g], preferred_element_type=F32) for g in groups]
        for g in groups:
            acc_sc[g] = (alpha[g][None] * acc_sc[g].reshape(rows // 8, 8, cg) + pv[g].reshape(rows // 8, 8, cg)
                         ).reshape(rows, cg)

    scores(0, sa_sc)

    def body(jj, carry):
        j = 2 * jj
        scores(j + 1, sb_sc)
        tile(j, sa_sc, False)
        scores(j + 2, sa_sc)
        tile(j + 1, sb_sc, False)
        return carry

    lax.fori_loop(0, i // 2, body, 0)

    @pl.when(i % 2 == 0)
    def _():
        tile(i, sa_sc, True)

    @pl.when(i % 2 == 1)
    def _():
        scores(i, sb_sc)
        tile(i - 1, sa_sc, False)
        tile(i, sb_sc, True)

    o_t = []
    for g in groups:
        l = acc_sc[g, dv:dv + 8, :]
        o_t.append((acc_sc[g, :dv, :].reshape(dv // 8, 8, cg) / l[None]).reshape(dv, cg))
    half = ncg // 2
    o1 = jnp.concatenate(o_t[:half], axis=1)
    o2 = jnp.concatenate(o_t[half:], axis=1)
    lam = lam_ref[...]
    lam_full = (jnp.exp(jnp.sum(lam[0:1] * lam[1:2], axis=-1, keepdims=True))
                - jnp.exp(jnp.sum(lam[2:3] * lam[3:4], axis=-1, keepdims=True)) + lam_init)
    o = (o1 - lam_full * o2).T
    o_ref[0, 0] = (_rms(o, sg_ref[...], SUBLN_EPS) * (1.0 - lam_init)).astype(BF16)


def _diff_attn(q, k, vt, lam, subln, lam_init):
    batch, nh, t, _ = q.shape
    rows, bq = vt.shape[-2:]
    cg = min(256, bq)
    ncg = 2 * bq // cg
    return pl.pallas_call(
        functools.partial(_diff_attn_kernel, lam_init=lam_init, cg=cg, n_heads=nh),
        grid=(batch, nh, t // bq),
        in_specs=[pl.BlockSpec((1, 1, bq, LANES), lambda b, h, i: (b, h, i, 0)),
                  pl.BlockSpec((1, 1, t, 2 * LANES), lambda b, h, i: (b, h, 0, 0)),
                  pl.BlockSpec((1, 1, t // bq, rows, bq), lambda b, h, i: (b, h, 0, 0, 0)),
                  _const_spec(lam.shape), _const_spec((1, LANES))],
        out_specs=pl.BlockSpec((1, 1, bq, LANES), lambda b, h, i: (b, h, i, 0)),
        out_shape=jax.ShapeDtypeStruct((batch, nh, t, LANES), BF16),
        scratch_shapes=[pltpu.VMEM((ncg, 8, cg), F32), pltpu.VMEM((ncg, rows, cg), F32),
                        pltpu.VMEM((ncg, bq, cg), F32), pltpu.VMEM((ncg, bq, cg), F32)],
        compiler_params=_cparams(("parallel", "parallel", "arbitrary")),
        name="diff_attn",
    )(q, k, vt, lam, subln.reshape(1, LANES))


def kernel(x, ffn_norm, ffn_w_in, ffn_w_out, mix_norm, rwkv_mu, rwkv_w_rkv, rwkv_w0, rwkv_w1, rwkv_w2, rwkv_a0, rwkv_a1, rwkv_a2, rwkv_g1, rwkv_g2, rwkv_k_k, rwkv_k_a, rwkv_r_k, rwkv_ln_g, rwkv_ln_b, rwkv_w_o, kv_norm, w_kv, k_norm, diff_w_q, diff_q_norm, diff_lambda, diff_subln, diff_w_o):
    batch, t, c = x.shape
    x = x.reshape(batch * t, c)

    x = _ffn(x, ffn_norm[0, 0], ffn_w_in[0, 0], ffn_w_out[0, 0])
    r, lw, k, v, kk, a, g = _rwkv_pre(x, batch, mix_norm[0], rwkv_mu[0], rwkv_w_rkv[0], rwkv_w0[0], rwkv_w1[0],
                                      rwkv_w2[0], rwkv_a0[0], rwkv_a1[0], rwkv_a2[0], rwkv_g1[0], rwkv_g2[0],
                                      rwkv_k_k[0], rwkv_k_a[0])
    y = _wkv_scan(r, lw, k, v, kk, a, rwkv_r_k[0], rwkv_ln_g[0], rwkv_ln_b[0])
    x = _out_proj(y, g, rwkv_w_o[0], x)
    x = _ffn(x, ffn_norm[0, 1], ffn_w_in[0, 1], ffn_w_out[0, 1])
    k_sh, v_sh = _kv_proj(x, batch, kv_norm, w_kv, k_norm)

    x = _ffn(x, ffn_norm[1, 0], ffn_w_in[1, 0], ffn_w_out[1, 0])
    q = _q_proj(x, batch, mix_norm[1], diff_w_q[0], diff_q_norm[0])
    lam_init = 0.8 - 0.6 * math.exp(-0.3 * 1)
    o = _diff_attn(q, k_sh, v_sh, diff_lambda[0], diff_subln[0], lam_init)
    x = _out_proj(o, None, diff_w_o[0], x)
    x = _ffn(x, ffn_norm[1, 1], ffn_w_in[1, 1], ffn_w_out[1, 1])
    return x.reshape(batch, t, c)
```

```python
import functools
import math

import jax
import jax.numpy as jnp
from jax import lax
from jax.experimental import pallas as pl
from jax.experimental.pallas import tpu as pltpu

F32 = jnp.float32
BF16 = jnp.bfloat16

LANES = 128
HEAD = 64
PAIR = 2 * HEAD
CHUNK = 64
ATT = 512
FFN_TF = 256
NORM_EPS = 1e-6
SUBLN_EPS = 1e-5
RWKV_LN_EPS = 64e-5
VMEM_LIMIT = 56 * 1024 * 1024
NEG_BIG = -1e30
BF16_ROWS = 16
VT_ROWS = LANES + BF16_ROWS


def _bf16_parts(x, n):
    parts = []
    for _ in range(n):
        m, e = math.frexp(x)
        p = math.ldexp(round(m * 256.0) / 256.0, e)
        parts.append(p)
        x -= p
    return tuple(parts)


LOG2E = math.log2(math.e)
LOG2E_PARTS = _bf16_parts(LOG2E, 3)


def _cparams(sem):
    return pltpu.CompilerParams(dimension_semantics=sem, vmem_limit_bytes=VMEM_LIMIT)


def _const_spec(shape):
    nd = len(shape)
    return pl.BlockSpec(shape, lambda *_: (0,) * nd, pipeline_mode=pl.Buffered(1))


def _mm(a, b):
    return jnp.dot(a.astype(BF16), b.astype(BF16), preferred_element_type=F32)


def _mm_nt(a, b):
    return lax.dot_general(a.astype(BF16), b.astype(BF16), (((1,), (1,)), ((), ())),
                           preferred_element_type=F32)


def _mm_tn(a, b):
    return lax.dot_general(a.astype(BF16), b.astype(BF16), (((0,), (0,)), ((), ())),
                           preferred_element_type=F32)


def _rms(x, g, eps):
    return x * lax.rsqrt(jnp.mean(x * x, axis=-1, keepdims=True) + eps) * g


def _half_mask(shape):
    return lax.broadcasted_iota(jnp.int32, shape, len(shape) - 1) % PAIR < HEAD


def _pair_sum(x, first):
    s1 = jnp.sum(jnp.where(first, x, 0.0), axis=-1, keepdims=True)
    s2 = jnp.sum(jnp.where(first, 0.0, x), axis=-1, keepdims=True)
    return jnp.where(first, s1, s2)


def _ffn_value(x, g_ref, win_ref, wout_ref, h_sc):
    d_ff = wout_ref.shape[0]
    xn = _rms(x, g_ref[...], NORM_EPS).astype(BF16)
    for f0 in range(0, d_ff, FFN_TF):
        gate = jnp.dot(xn, win_ref[:, f0:f0 + FFN_TF], preferred_element_type=F32)
        up = jnp.dot(xn, win_ref[:, d_ff + f0:d_ff + f0 + FFN_TF], preferred_element_type=F32)
        h_sc[:, f0:f0 + FFN_TF] = (gate * jax.nn.sigmoid(gate) * up).astype(BF16)
    return x + 0.5 * jnp.dot(h_sc[...], wout_ref[...], preferred_element_type=F32)


def _keys_values(x, ng_ref, w_ref, kg_ref, k_out, v_out):
    tm, c = x.shape
    n_heads = c // LANES
    kv = _mm(_rms(x, ng_ref[...], NORM_EPS), w_ref[...])
    first = _half_mask((tm, LANES))
    kg = kg_ref[...]
    lane = lax.broadcasted_iota(jnp.int32, (tm, LANES), 1)
    pos = lax.broadcasted_iota(jnp.int32, (tm, LANES), 0)
    base = jnp.where(lane >= 2 * len(LOG2E_PARTS), 0.0,
                     jnp.where(lane % 2 == 0, (pos // HEAD * HEAD).astype(F32), (pos % HEAD).astype(F32)))
    ones = jnp.ones((VT_ROWS - LANES, tm), BF16)
    for h in range(n_heads):
        kh = kv[:, h * LANES:(h + 1) * LANES]
        ms = _pair_sum(kh * kh, first) * (1.0 / HEAD)
        kn = kh * lax.rsqrt(ms + NORM_EPS) * kg
        slope = 2.0 ** (-8.0 * (h + 1) / n_heads)
        k_out[0, h] = jnp.concatenate([kn.astype(BF16), (base * slope).astype(BF16)], axis=-1)
        vt = kv[:, c + h * LANES:c + (h + 1) * LANES].T.astype(BF16)
        v_out[0, h, 0] = jnp.concatenate([vt, ones], axis=0)


def _queries(x, ng_ref, w_ref, qg_ref, q_out):
    tm, c = x.shape
    q = _mm(_rms(x, ng_ref[...], NORM_EPS), w_ref[...])
    first = _half_mask((tm, LANES))
    qg = qg_ref[...] * (HEAD ** -0.5 * LOG2E)
    for h in range(c // LANES):
        qh = q[:, h * LANES:(h + 1) * LANES]
        ms = _pair_sum(qh * qh, first) * (1.0 / HEAD)
        q_out[0, h] = (qh * lax.rsqrt(ms + NORM_EPS) * qg).astype(BF16)


def _block_kernel(*refs, pre, post):
    refs = list(refs)
    x = refs.pop(0)[...]
    if pre is not None:
        y_ref = refs.pop(0)
        y = jnp.concatenate([y_ref[0, p] for p in range(y_ref.shape[1])], axis=-1)
        if pre == "gated":
            y = y * refs.pop(0)[...]
        x = x + jnp.dot(y.astype(BF16), refs.pop(0)[...], preferred_element_type=F32)
    g_ref, win_ref, wout_ref = refs[:3]
    refs = refs[3:]
    post_refs = [refs.pop(0) for _ in range(3)] if post is not None else []
    o_ref = refs.pop(0)
    h_sc = refs.pop()
    x = _ffn_value(x, g_ref, win_ref, wout_ref, h_sc)
    o_ref[...] = x
    if post == "kv":
        _keys_values(x, *post_refs, *refs)
    elif post == "q":
        _queries(x, *post_refs, *refs)


def _block(x, batch, g, w_in, w_out, *, pre=None, post=None):
    m, c = x.shape
    t = m // batch
    d_ff = w_out.shape[0]
    tm = min(ATT, t)
    bps = t // tm
    nh = c // LANES
    row_spec = pl.BlockSpec((tm, c), lambda i: (i, 0))
    head_idx = lambda i: (i // bps, 0, i % bps, 0)
    args, in_specs = [x], [row_spec]
    pre_kind = None
    if pre is not None:
        y, gate, w_o = pre
        pre_kind = "plain" if gate is None else "gated"
        args.append(y)
        in_specs.append(pl.BlockSpec((1, y.shape[1], tm, LANES), head_idx))
        if gate is not None:
            args.append(gate)
            in_specs.append(row_spec)
        args.append(w_o.astype(BF16))
        in_specs.append(_const_spec((c, c)))
    args += [g.reshape(1, c), w_in.astype(BF16), w_out.astype(BF16)]
    in_specs += [_const_spec((1, c)), _const_spec((c, 2 * d_ff)), _const_spec((d_ff, c))]
    out_specs, out_shape = [row_spec], [jax.ShapeDtypeStruct((m, c), F32)]
    post_kind = None
    if post is not None:
        post_kind, ng, w, head_norm = post
        args += [ng.reshape(1, c), w.astype(BF16), jnp.concatenate([head_norm, head_norm]).reshape(1, LANES)]
        in_specs += [_const_spec((1, c)), _const_spec(w.shape), _const_spec((1, LANES))]
        if post_kind == "kv":
            out_specs += [pl.BlockSpec((1, nh, tm, 2 * LANES), head_idx),
                          pl.BlockSpec((1, nh, 1, VT_ROWS, tm), lambda i: (i // bps, 0, i % bps, 0, 0))]
            out_shape += [jax.ShapeDtypeStruct((batch, nh, t, 2 * LANES), BF16),
                          jax.ShapeDtypeStruct((batch, nh, t // tm, VT_ROWS, tm), BF16)]
        else:
            out_specs.append(pl.BlockSpec((1, nh, tm, LANES), head_idx))
            out_shape.append(jax.ShapeDtypeStruct((batch, nh, t, LANES), BF16))
    return pl.pallas_call(
        functools.partial(_block_kernel, pre=pre_kind, post=post_kind),
        grid=(m // tm,),
        in_specs=in_specs,
        out_specs=out_specs,
        out_shape=out_shape,
        scratch_shapes=[pltpu.VMEM((tm, d_ff), BF16)],
        compiler_params=_cparams(("parallel",)),
        name="block_" + (pre_kind or "x") + "_" + (post_kind or "x"),
    )(*args)


def _rwkv_pre_kernel(x_ref, xp_ref, ng_ref, mu_ref, wrkv_ref, w0_ref, w1_ref, w2_ref, a0_ref, a1_ref,
                     a2_ref, g1_ref, g2_ref, kk_ref, ka_ref,
                     r_out, lw_out, k_out, v_out, kk_out, a_out, g_out, *, blocks_per_seq):
    tm, c = x_ref.shape
    ng = ng_ref[...]
    h = _rms(x_ref[...], ng, NORM_EPS)
    prev = _rms(xp_ref[7:8, :], ng, NORM_EPS)
    prev = jnp.where(pl.program_id(0) % blocks_per_seq == 0, 0.0, prev)
    row = lax.broadcasted_iota(jnp.int32, (tm, c), 0)
    h_prev = jnp.where(row == 0, prev, pltpu.roll(h, 1, 0))
    dx = h_prev - h

    def mix(i):
        return h + dx * mu_ref[i:i + 1, :]

    r = _mm(mix(0), wrkv_ref[0])
    k = _mm(mix(1), wrkv_ref[1])
    v = _mm(mix(2), wrkv_ref[2])
    z = w0_ref[...] + _mm(jnp.tanh(_mm(mix(3), w1_ref[...])), w2_ref[...])
    w_log = -jax.nn.softplus(-z) - 0.5
    lw = -jnp.exp(w_log)
    a = jax.nn.sigmoid(a0_ref[...] + _mm(_mm(mix(4), a1_ref[...]), a2_ref[...]))
    g_out[...] = _mm(jax.nn.sigmoid(_mm(mix(5), g1_ref[...])), g2_ref[...])
    kk = k * kk_ref[...]
    k = k * (1.0 + (a - 1.0) * ka_ref[...])
    first = _half_mask((tm, PAIR))
    for p in range(c // PAIR):
        sl = slice(p * PAIR, (p + 1) * PAIR)
        kkp = kk[:, sl]
        ss = _pair_sum(kkp * kkp, first)
        r_out[0, p] = r[:, sl]
        lw_out[0, p] = lw[:, sl]
        k_out[0, p] = k[:, sl]
        v_out[0, p] = v[:, sl]
        kk_out[0, p] = kkp * lax.rsqrt(jnp.maximum(ss, 1e-24))
        a_out[0, p] = a[:, sl]


def _rwkv_pre(x, batch, ng, mu, w_rkv, w0, w1, w2, a0, a1, a2, g1, g2, k_k, k_a, *, tm=512):
    m, c = x.shape
    t = m // batch
    tm = min(tm, t)
    bps = t // tm
    npair = c // PAIR
    row = lambda a: a.reshape(1, c)
    pair_spec = pl.BlockSpec((1, npair, tm, PAIR), lambda i: (i // bps, 0, i % bps, 0))
    pair_shape = jax.ShapeDtypeStruct((batch, npair, t, PAIR), F32)
    ws = [w_rkv.astype(BF16), row(w0), w1.astype(BF16), w2.astype(BF16), row(a0), a1.astype(BF16),
          a2.astype(BF16), g1.astype(BF16), g2.astype(BF16), row(k_k), row(k_a)]
    return pl.pallas_call(
        functools.partial(_rwkv_pre_kernel, blocks_per_seq=bps),
        grid=(m // tm,),
        in_specs=[pl.BlockSpec((tm, c), lambda i: (i, 0)),
                  pl.BlockSpec((8, c), lambda i: (jnp.maximum(i * (tm // 8) - 1, 0), 0)),
                  _const_spec((1, c)), _const_spec(mu.shape)] + [_const_spec(w.shape) for w in ws],
        out_specs=[pair_spec] * 6 + [pl.BlockSpec((tm, c), lambda i: (i, 0))],
        out_shape=[pair_shape] * 6 + [jax.ShapeDtypeStruct((m, c), F32)],
        compiler_params=_cparams(("parallel",)),
        name="rwkv_pre",
    )(x, x, row(ng), mu, *ws)


def _stack(x, first):
    return jnp.concatenate([jnp.where(first, x, 0.0), jnp.where(first, 0.0, x)], axis=0)


def _wkv_kernel(r_ref, lw_ref, k_ref, v_ref, kk_ref, a_ref, rk_ref, lng_ref, lnb_ref, y_ref,
                s_sc, carry_sc, bv_sc, *, nblk):
    lb = r_ref.shape[2]
    nc = lb // CHUNK
    l2 = 2 * CHUNK
    i = pl.program_id(2)

    first = _half_mask((CHUNK, PAIR))
    ri = lax.broadcasted_iota(jnp.int32, (l2, l2), 0)
    ci = lax.broadcasted_iota(jnp.int32, (l2, l2), 1)
    same = (ri // CHUNK) == (ci // CHUNK)
    strict = same & (ri % CHUNK > ci % CHUNK)
    incl = same & (ri % CHUNK >= ci % CHUNK)
    eye = ri == ci
    ident = jnp.where(eye, 1.0, 0.0)
    rk = rk_ref[0]
    lng = lng_ref[0]
    lnb = lnb_ref[0]
    chs = range(nc)

    def prepare_stages():
        w = {}

        def prep():
            tri = jnp.where(lax.broadcasted_iota(jnp.int32, (CHUNK, CHUNK), 1)
                            <= lax.broadcasted_iota(jnp.int32, (CHUNK, CHUNK), 0), 1.0, 0.0).astype(BF16)
            lw_wide = jnp.concatenate([lw_ref[0, 0, ch * CHUNK:(ch + 1) * CHUNK, :] for ch in chs], axis=1)
            lw_hi = lw_wide.astype(BF16)
            lw_lo = (lw_wide - lw_hi.astype(F32)).astype(BF16)
            c_wide = (jnp.dot(tri, lw_hi, preferred_element_type=F32)
                      + jnp.dot(tri, lw_lo, preferred_element_type=F32))
            for ch in chs:
                sl = slice(ch * CHUNK, (ch + 1) * CHUNK)
                r = r_ref[0, 0, sl, :]
                k = k_ref[0, 0, sl, :]
                v = v_ref[0, 0, sl, :]
                kk = kk_ref[0, 0, sl, :]
                c = c_wide[:, ch * PAIR:(ch + 1) * PAIR]
                c_last = c[CHUNK - 1:CHUNK, :]
                e_neg = jnp.exp(-c)
                e_last = jnp.exp(c_last - c)
                b = kk * a_ref[0, 0, sl, :]
                rt = _stack(r * jnp.exp(c), first)
                at = _stack(-kk * jnp.exp(c - lw_ref[0, 0, sl, :]), first)
                w[ch] = dict(
                    rt=rt, at=at, vs=_stack(v, first), g_last=jnp.exp(c_last),
                    bonus=_pair_sum(r * k * rk, first) * v,
                    ar=jnp.concatenate([at, rt], axis=0),
                    bkt=jnp.concatenate([_stack(b * e_neg, first), _stack(k * e_neg, first)], axis=0),
                    bk_last=jnp.concatenate([_stack(b * e_last, first), _stack(k * e_last, first)], axis=0))

        def couplings():
            for ch in chs:
                d = w[ch]
                g = _mm_nt(d["ar"], d["bkt"])
                d["n"] = jnp.where(strict, g[:l2, :l2], 0.0)
                d["a_ak"] = jnp.where(strict, g[:l2, l2:], 0.0)
                d["a_rbk"] = jnp.concatenate([jnp.where(incl, g[l2:, :l2], 0.0),
                                              jnp.where(incl, g[l2:, l2:], 0.0)], axis=1)

        def start_inverse():
            for ch in chs:
                d = w[ch]
                d["aakv"] = _mm(d["a_ak"], d["vs"])
                d["tinv"] = ident + d["n"]
                d["pw"] = d["n"]

        def square():
            for ch in chs:
                w[ch]["pw"] = _mm(w[ch]["pw"], w[ch]["pw"])

        def extend():
            for ch in chs:
                w[ch]["tinv"] = w[ch]["tinv"] + _mm(w[ch]["tinv"], w[ch]["pw"])

        def solve():
            for ch in chs:
                d = w[ch]
                x = _mm(d["tinv"], jnp.concatenate([d["at"], d["aakv"]], axis=1))
                d["z"] = jnp.concatenate([x, jnp.concatenate([jnp.zeros_like(d["vs"]), d["vs"]], axis=1)],
                                         axis=0)

        def finish():
            for ch in chs:
                d = w[ch]
                m1 = _mm(d["a_rbk"], d["z"])
                m2 = _mm_tn(d["z"], d["bk_last"])
                carry_sc[ch, 0] = d["rt"] + m1[:, :PAIR]
                carry_sc[ch, 1] = m1[:, PAIR:]
                carry_sc[ch, 2] = jnp.where(eye, d["g_last"], 0.0) + m2[:PAIR]
                carry_sc[ch, 3] = m2[PAIR:]
                bv_sc[ch] = d["bonus"]

        rounds = int(math.log2(CHUNK)) - 1
        return [prep, couplings, start_inverse] + [square, extend] * rounds + [solve, finish]

    def advance(ch, s):
        ys = _mm_nt(carry_sc[ch, 0], s) + carry_sc[ch, 1]
        s = _mm(s, carry_sc[ch, 2]) + carry_sc[ch, 3]
        y = ys[:CHUNK] + ys[CHUNK:]
        mean = _pair_sum(y, first) * (1.0 / HEAD)
        d = y - mean
        var = _pair_sum(d * d, first) * (1.0 / HEAD)
        y_ref[0, 0, ch * CHUNK:(ch + 1) * CHUNK, :] = d * lax.rsqrt(var + RWKV_LN_EPS) * lng + lnb + bv_sc[ch]
        return s

    @pl.when(i == 0)
    def _():
        s_sc[...] = jnp.zeros_like(s_sc)
        for stage in prepare_stages():
            stage()

    @pl.when((i > 0) & (i < nblk))
    def _():
        s = s_sc[...]
        todo = list(chs)
        stages = prepare_stages()
        per_stage = -(-nc // (len(stages) - 2))
        for n_stage, stage in enumerate(stages):
            if n_stage == len(stages) - 1:
                while todo:
                    s = advance(todo.pop(0), s)
            stage()
            if 0 < n_stage < len(stages) - 1:
                for _ in range(min(per_stage, len(todo))):
                    s = advance(todo.pop(0), s)
        s_sc[...] = s

    @pl.when(i == nblk)
    def _():
        s = s_sc[...]
        for ch in chs:
            s = advance(ch, s)
        s_sc[...] = s


def _wkv_scan(r, lw, k, v, kk, a, r_k, ln_g, ln_b, *, lb=1024):
    batch, npair, t, _ = r.shape
    lb = min(lb, t)
    nblk = t // lb
    nc = lb // CHUNK
    in_spec = pl.BlockSpec((1, 1, lb, PAIR), lambda b, p, i: (b, p, jnp.minimum(i, nblk - 1), 0))
    out_spec = pl.BlockSpec((1, 1, lb, PAIR), lambda b, p, i: (b, p, jnp.maximum(i - 1, 0), 0))
    par_spec = pl.BlockSpec((1, 1, PAIR), lambda b, p, i: (p, 0, 0))
    par = lambda x: x.reshape(npair, 1, PAIR)
    return pl.pallas_call(
        functools.partial(_wkv_kernel, nblk=nblk),
        grid=(batch, npair, nblk + 1),
        in_specs=[in_spec] * 6 + [par_spec] * 3,
        out_specs=out_spec,
        out_shape=jax.ShapeDtypeStruct(r.shape, F32),
        scratch_shapes=[pltpu.VMEM((PAIR, PAIR), F32), pltpu.VMEM((nc, 4, PAIR, PAIR), F32),
                        pltpu.VMEM((nc, CHUNK, PAIR), F32)],
        compiler_params=_cparams(("parallel", "parallel", "arbitrary")),
        name="wkv_scan",
    )(r, lw, k, v, kk, a, par(r_k), par(ln_g), par(ln_b))


def _sublane_all(op, x):
    for shift in (4, 2, 1):
        x = op(x, pltpu.roll(x, shift, 0))
    return x


def _diff_attn_kernel(q_ref, k_ref, vt_ref, lam_ref, sg_ref, o_ref, m_sc, acc_sc, sa_sc, sb_sc, *,
                      lam_init, cg, n_heads, unroll):
    bq = q_ref.shape[2]
    dv = LANES
    rows = vt_ref.shape[3]
    ncg = 2 * bq // cg
    i = pl.program_id(2)
    q = q_ref[0, 0]
    first = _half_mask((bq, LANES))
    lane = lax.broadcasted_iota(jnp.int32, (bq, LANES), 1)
    feat = jnp.zeros((bq, LANES), F32)
    for n, part in enumerate(LOG2E_PARTS):
        feat = jnp.where(lane // 2 == n, part, feat)
    feat = feat.astype(BF16)
    zero = jnp.zeros_like(q)
    qs = jnp.concatenate([jnp.concatenate([jnp.where(first, q, zero), feat], axis=1),
                          jnp.concatenate([jnp.where(first, zero, q), feat], axis=1)], axis=0)
    head = (pl.program_id(1) + 1).astype(F32)
    block_bias = jnp.exp2(jnp.full((8, cg), -8.0 / n_heads, F32) * head) * (LOG2E * bq)

    m_sc[...] = jnp.full_like(m_sc, NEG_BIG)
    acc_sc[...] = jnp.zeros_like(acc_sc)

    groups = range(ncg)

    def scores(j, buf):
        kt = k_ref[0, 0, pl.ds(pl.multiple_of(j * bq, bq), bq), :]
        for g in groups:
            buf[g] = lax.dot_general(kt, qs[g * cg:(g + 1) * cg], (((1,), (1,)), ((), ())),
                                     preferred_element_type=F32)

    def tile(j, buf, diag):
        q0 = [(g * cg) % bq for g in groups]
        nk = [min(bq, q0[g] + cg) if diag else bq for g in groups]
        offset = 0.0 if diag else block_bias * (j - i).astype(F32)
        p, alpha = [], []
        for g in groups:
            sg = buf[g, :nk[g], :]
            if diag:
                key = lax.broadcasted_iota(jnp.int32, sg.shape, 0)
                qry = lax.broadcasted_iota(jnp.int32, sg.shape, 1) + q0[g]
                sg = jnp.where(key <= qry, sg, NEG_BIG)
            sg = sg.reshape(nk[g] // 8, 8, cg)
            m_prev = m_sc[g]
            m_new = jnp.maximum(m_prev, _sublane_all(jnp.maximum, jnp.max(sg, axis=0)) + offset)
            p.append(jnp.exp2(sg - (m_new - offset)[None]).reshape(nk[g], cg).astype(BF16))
            alpha.append(jnp.exp2(m_prev - m_new))
            m_sc[g] = m_new
        pv = [jnp.dot(vt_ref[0, 0, j, :, :nk[g]], p[g], preferred_element_type=F32) for g in groups]
        for g in groups:
            acc_sc[g] = (alpha[g][None] * acc_sc[g].reshape(rows // 8, 8, cg) + pv[g].reshape(rows // 8, 8, cg)
                         ).reshape(rows, cg)

    bufs = (sa_sc, sb_sc)
    scores(0, sa_sc)

    def run(first_block, count, last_is_diag):
        for n in range(count):
            last = n == count - 1
            if not (last and last_is_diag):
                scores(first_block + n + 1, bufs[(n + 1) % 2])
            tile(first_block + n, bufs[n % 2], last and last_is_diag)

    def body(jj, carry):
        run(unroll * jj, unroll, False)
        return carry

    lax.fori_loop(0, i // unroll, body, 0)
    for rem in range(unroll):
        @pl.when(i % unroll == rem)
        def _():
            run(i - rem, rem + 1, True)

    o_t = []
    for g in groups:
        l = acc_sc[g, dv:dv + 8, :]
        o_t.append((acc_sc[g, :dv, :].reshape(dv // 8, 8, cg) / l[None]).reshape(dv, cg))
    half = ncg // 2
    o1 = jnp.concatenate(o_t[:half], axis=1)
    o2 = jnp.concatenate(o_t[half:], axis=1)
    lam = lam_ref[...]
    lam_full = (jnp.exp(jnp.sum(lam[0:1] * lam[1:2], axis=-1, keepdims=True))
                - jnp.exp(jnp.sum(lam[2:3] * lam[3:4], axis=-1, keepdims=True)) + lam_init)
    o = (o1 - lam_full * o2).T
    o_ref[0, 0] = (_rms(o, sg_ref[...], SUBLN_EPS) * (1.0 - lam_init)).astype(BF16)


def _diff_attn(q, k, vt, lam, subln, lam_init, *, unroll=4):
    assert unroll % 2 == 0
    batch, nh, t, _ = q.shape
    rows, bq = vt.shape[-2:]
    cg = min(256, bq)
    ncg = 2 * bq // cg
    return pl.pallas_call(
        functools.partial(_diff_attn_kernel, lam_init=lam_init, cg=cg, n_heads=nh, unroll=unroll),
        grid=(batch, nh, t // bq),
        in_specs=[pl.BlockSpec((1, 1, bq, LANES), lambda b, h, i: (b, h, i, 0)),
                  pl.BlockSpec((1, 1, t, 2 * LANES), lambda b, h, i: (b, h, 0, 0)),
                  pl.BlockSpec((1, 1, t // bq, rows, bq), lambda b, h, i: (b, h, 0, 0, 0)),
                  _const_spec(lam.shape), _const_spec((1, LANES))],
        out_specs=pl.BlockSpec((1, 1, bq, LANES), lambda b, h, i: (b, h, i, 0)),
        out_shape=jax.ShapeDtypeStruct((batch, nh, t, LANES), BF16),
        scratch_shapes=[pltpu.VMEM((ncg, 8, cg), F32), pltpu.VMEM((ncg, rows, cg), F32),
                        pltpu.VMEM((ncg, bq, cg), F32), pltpu.VMEM((ncg, bq, cg), F32)],
        compiler_params=_cparams(("parallel", "parallel", "arbitrary")),
        name="diff_attn",
    )(q, k, vt, lam, subln.reshape(1, LANES))


def kernel(x, ffn_norm, ffn_w_in, ffn_w_out, mix_norm, rwkv_mu, rwkv_w_rkv, rwkv_w0, rwkv_w1, rwkv_w2, rwkv_a0, rwkv_a1, rwkv_a2, rwkv_g1, rwkv_g2, rwkv_k_k, rwkv_k_a, rwkv_r_k, rwkv_ln_g, rwkv_ln_b, rwkv_w_o, kv_norm, w_kv, k_norm, diff_w_q, diff_q_norm, diff_lambda, diff_subln, diff_w_o):
    batch, t, c = x.shape
    x = x.reshape(batch * t, c)

    x, = _block(x, batch, ffn_norm[0, 0], ffn_w_in[0, 0], ffn_w_out[0, 0])
    r, lw, k, v, kk, a, g = _rwkv_pre(x, batch, mix_norm[0], rwkv_mu[0], rwkv_w_rkv[0], rwkv_w0[0], rwkv_w1[0],
                                      rwkv_w2[0], rwkv_a0[0], rwkv_a1[0], rwkv_a2[0], rwkv_g1[0], rwkv_g2[0],
                                      rwkv_k_k[0], rwkv_k_a[0])
    y = _wkv_scan(r, lw, k, v, kk, a, rwkv_r_k[0], rwkv_ln_g[0], rwkv_ln_b[0])
    x, k_sh, v_sh = _block(x, batch, ffn_norm[0, 1], ffn_w_in[0, 1], ffn_w_out[0, 1],
                           pre=(y, g, rwkv_w_o[0]), post=("kv", kv_norm, w_kv, k_norm))

    x, q = _block(x, batch, ffn_norm[1, 0], ffn_w_in[1, 0], ffn_w_out[1, 0],
                  post=("q", mix_norm[1], diff_w_q[0], diff_q_norm[0]))
    lam_init = 0.8 - 0.6 * math.exp(-0.3 * 1)
    o = _diff_attn(q, k_sh, v_sh, diff_lambda[0], diff_subln[0], lam_init)
    x, = _block(x, batch, ffn_norm[1, 1], ffn_w_in[1, 1], ffn_w_out[1, 1], pre=(o, None, diff_w_o[0]))
    return x.reshape(batch, t, c)
```

```python
import functools
import math

import jax
import jax.numpy as jnp
from jax import lax
from jax.experimental import pallas as pl
from jax.experimental.pallas import tpu as pltpu

F32 = jnp.float32
BF16 = jnp.bfloat16

LANES = 128
HEAD = 64
PAIR = 2 * HEAD
CHUNK = 64
ATT = 512
FFN_TF = 256
NORM_EPS = 1e-6
SUBLN_EPS = 1e-5
RWKV_LN_EPS = 64e-5
VMEM_LIMIT = 56 * 1024 * 1024
NEG_BIG = -1e30
BF16_ROWS = 16
VT_ROWS = LANES + BF16_ROWS


def _bf16_parts(x, n):
    parts = []
    for _ in range(n):
        m, e = math.frexp(x)
        p = math.ldexp(round(m * 256.0) / 256.0, e)
        parts.append(p)
        x -= p
    return tuple(parts)


LOG2E = math.log2(math.e)
LOG2E_PARTS = _bf16_parts(LOG2E, 3)


def _cparams(sem):
    return pltpu.CompilerParams(dimension_semantics=sem, vmem_limit_bytes=VMEM_LIMIT)


def _const_spec(shape):
    nd = len(shape)
    return pl.BlockSpec(shape, lambda *_: (0,) * nd, pipeline_mode=pl.Buffered(1))


def _mm(a, b):
    return jnp.dot(a.astype(BF16), b.astype(BF16), preferred_element_type=F32)


def _mm_nt(a, b):
    return lax.dot_general(a.astype(BF16), b.astype(BF16), (((1,), (1,)), ((), ())),
                           preferred_element_type=F32)


def _mm_tn(a, b):
    return lax.dot_general(a.astype(BF16), b.astype(BF16), (((0,), (0,)), ((), ())),
                           preferred_element_type=F32)


def _rms(x, g, eps):
    return x * lax.rsqrt(jnp.mean(x * x, axis=-1, keepdims=True) + eps) * g


def _half_mask(shape):
    return lax.broadcasted_iota(jnp.int32, shape, len(shape) - 1) % PAIR < HEAD


def _pair_sum(x, first):
    s1 = jnp.sum(jnp.where(first, x, 0.0), axis=-1, keepdims=True)
    s2 = jnp.sum(jnp.where(first, 0.0, x), axis=-1, keepdims=True)
    return jnp.where(first, s1, s2)


def _ffn_value(x, g_ref, win_ref, wout_ref, h_sc):
    d_ff = wout_ref.shape[0]
    xn = _rms(x, g_ref[...], NORM_EPS).astype(BF16)
    for f0 in range(0, d_ff, FFN_TF):
        gate = jnp.dot(xn, win_ref[:, f0:f0 + FFN_TF], preferred_element_type=F32)
        up = jnp.dot(xn, win_ref[:, d_ff + f0:d_ff + f0 + FFN_TF], preferred_element_type=F32)
        h_sc[:, f0:f0 + FFN_TF] = (gate * jax.nn.sigmoid(gate) * up).astype(BF16)
    return x + 0.5 * jnp.dot(h_sc[...], wout_ref[...], preferred_element_type=F32)


def _keys_values(x, ng_ref, w_ref, kg_ref, k_out, v_out):
    tm, c = x.shape
    n_heads = c // LANES
    kv = _mm(_rms(x, ng_ref[...], NORM_EPS), w_ref[...])
    first = _half_mask((tm, LANES))
    kg = kg_ref[...]
    lane = lax.broadcasted_iota(jnp.int32, (tm, LANES), 1)
    pos = lax.broadcasted_iota(jnp.int32, (tm, LANES), 0)
    base = jnp.where(lane >= 2 * len(LOG2E_PARTS), 0.0,
                     jnp.where(lane % 2 == 0, (pos // HEAD * HEAD).astype(F32), (pos % HEAD).astype(F32)))
    ones = jnp.ones((VT_ROWS - LANES, tm), BF16)
    for h in range(n_heads):
        kh = kv[:, h * LANES:(h + 1) * LANES]
        ms = _pair_sum(kh * kh, first) * (1.0 / HEAD)
        kn = kh * lax.rsqrt(ms + NORM_EPS) * kg
        slope = 2.0 ** (-8.0 * (h + 1) / n_heads)
        k_out[0, h] = jnp.concatenate([kn.astype(BF16), (base * slope).astype(BF16)], axis=-1)
        vt = kv[:, c + h * LANES:c + (h + 1) * LANES].T.astype(BF16)
        v_out[0, h, 0] = jnp.concatenate([vt, ones], axis=0)


def _queries(x, ng_ref, w_ref, qg_ref, q_out):
    tm, c = x.shape
    q = _mm(_rms(x, ng_ref[...], NORM_EPS), w_ref[...])
    first = _half_mask((tm, LANES))
    qg = qg_ref[...] * (HEAD ** -0.5 * LOG2E)
    for h in range(c // LANES):
        qh = q[:, h * LANES:(h + 1) * LANES]
        ms = _pair_sum(qh * qh, first) * (1.0 / HEAD)
        q_out[0, h] = (qh * lax.rsqrt(ms + NORM_EPS) * qg).astype(BF16)


def _block_kernel(*refs, pre, post):
    refs = list(refs)
    x = refs.pop(0)[...]
    if pre is not None:
        y_ref = refs.pop(0)
        y = jnp.concatenate([y_ref[0, p] for p in range(y_ref.shape[1])], axis=-1)
        if pre == "gated":
            y = y * refs.pop(0)[...]
        x = x + jnp.dot(y.astype(BF16), refs.pop(0)[...], preferred_element_type=F32)
    g_ref, win_ref, wout_ref = refs[:3]
    refs = refs[3:]
    post_refs = [refs.pop(0) for _ in range(3)] if post is not None else []
    o_ref = refs.pop(0)
    h_sc = refs.pop()
    x = _ffn_value(x, g_ref, win_ref, wout_ref, h_sc)
    o_ref[...] = x
    if post == "kv":
        _keys_values(x, *post_refs, *refs)
    elif post == "q":
        _queries(x, *post_refs, *refs)


def _block(x, batch, g, w_in, w_out, *, pre=None, post=None):
    m, c = x.shape
    t = m // batch
    d_ff = w_out.shape[0]
    tm = min(ATT, t)
    bps = t // tm
    nh = c // LANES
    row_spec = pl.BlockSpec((tm, c), lambda i: (i, 0))
    head_idx = lambda i: (i // bps, 0, i % bps, 0)
    args, in_specs = [x], [row_spec]
    pre_kind = None
    if pre is not None:
        y, gate, w_o = pre
        pre_kind = "plain" if gate is None else "gated"
        args.append(y)
        in_specs.append(pl.BlockSpec((1, y.shape[1], tm, LANES), head_idx))
        if gate is not None:
            args.append(gate)
            in_specs.append(row_spec)
        args.append(w_o.astype(BF16))
        in_specs.append(_const_spec((c, c)))
    args += [g.reshape(1, c), w_in.astype(BF16), w_out.astype(BF16)]
    in_specs += [_const_spec((1, c)), _const_spec((c, 2 * d_ff)), _const_spec((d_ff, c))]
    out_specs, out_shape = [row_spec], [jax.ShapeDtypeStruct((m, c), F32)]
    post_kind = None
    if post is not None:
        post_kind, ng, w, head_norm = post
        args += [ng.reshape(1, c), w.astype(BF16), jnp.concatenate([head_norm, head_norm]).reshape(1, LANES)]
        in_specs += [_const_spec((1, c)), _const_spec(w.shape), _const_spec((1, LANES))]
        if post_kind == "kv":
            out_specs += [pl.BlockSpec((1, nh, tm, 2 * LANES), head_idx),
                          pl.BlockSpec((1, nh, 1, VT_ROWS, tm), lambda i: (i // bps, 0, i % bps, 0, 0))]
            out_shape += [jax.ShapeDtypeStruct((batch, nh, t, 2 * LANES), BF16),
                          jax.ShapeDtypeStruct((batch, nh, t // tm, VT_ROWS, tm), BF16)]
        else:
            out_specs.append(pl.BlockSpec((1, nh, tm, LANES), head_idx))
            out_shape.append(jax.ShapeDtypeStruct((batch, nh, t, LANES), BF16))
    return pl.pallas_call(
        functools.partial(_block_kernel, pre=pre_kind, post=post_kind),
        grid=(m // tm,),
        in_specs=in_specs,
        out_specs=out_specs,
        out_shape=out_shape,
        scratch_shapes=[pltpu.VMEM((tm, d_ff), BF16)],
        compiler_params=_cparams(("parallel",)),
        name="block_" + (pre_kind or "x") + "_" + (post_kind or "x"),
    )(*args)


def _rwkv_pre_kernel(x_ref, xp_ref, ng_ref, mu_ref, wrkv_ref, w0_ref, w1_ref, w2_ref, a0_ref, a1_ref,
                     a2_ref, g1_ref, g2_ref, kk_ref, ka_ref,
                     r_out, lw_out, k_out, v_out, kk_out, a_out, g_out, *, blocks_per_seq):
    tm, c = x_ref.shape
    ng = ng_ref[...]
    h = _rms(x_ref[...], ng, NORM_EPS)
    prev = _rms(xp_ref[7:8, :], ng, NORM_EPS)
    prev = jnp.where(pl.program_id(0) % blocks_per_seq == 0, 0.0, prev)
    row = lax.broadcasted_iota(jnp.int32, (tm, c), 0)
    h_prev = jnp.where(row == 0, prev, pltpu.roll(h, 1, 0))
    dx = h_prev - h

    def mix(i):
        return h + dx * mu_ref[i:i + 1, :]

    r = _mm(mix(0), wrkv_ref[0])
    k = _mm(mix(1), wrkv_ref[1])
    v = _mm(mix(2), wrkv_ref[2])
    z = w0_ref[...] + _mm(jnp.tanh(_mm(mix(3), w1_ref[...])), w2_ref[...])
    w_log = -jax.nn.softplus(-z) - 0.5
    lw = -jnp.exp(w_log)
    a = jax.nn.sigmoid(a0_ref[...] + _mm(_mm(mix(4), a1_ref[...]), a2_ref[...]))
    g_out[...] = _mm(jax.nn.sigmoid(_mm(mix(5), g1_ref[...])), g2_ref[...])
    kk = k * kk_ref[...]
    k = k * (1.0 + (a - 1.0) * ka_ref[...])
    first = _half_mask((tm, PAIR))
    for p in range(c // PAIR):
        sl = slice(p * PAIR, (p + 1) * PAIR)
        kkp = kk[:, sl]
        ss = _pair_sum(kkp * kkp, first)
        r_out[0, p] = r[:, sl]
        lw_out[0, p] = lw[:, sl]
        k_out[0, p] = k[:, sl]
        v_out[0, p] = v[:, sl]
        kk_out[0, p] = kkp * lax.rsqrt(jnp.maximum(ss, 1e-24))
        a_out[0, p] = a[:, sl]


def _rwkv_pre(x, batch, ng, mu, w_rkv, w0, w1, w2, a0, a1, a2, g1, g2, k_k, k_a, *, tm=512):
    m, c = x.shape
    t = m // batch
    tm = min(tm, t)
    bps = t // tm
    npair = c // PAIR
    row = lambda a: a.reshape(1, c)
    pair_spec = pl.BlockSpec((1, npair, tm, PAIR), lambda i: (i // bps, 0, i % bps, 0))
    pair_shape = jax.ShapeDtypeStruct((batch, npair, t, PAIR), F32)
    ws = [w_rkv.astype(BF16), row(w0), w1.astype(BF16), w2.astype(BF16), row(a0), a1.astype(BF16),
          a2.astype(BF16), g1.astype(BF16), g2.astype(BF16), row(k_k), row(k_a)]
    return pl.pallas_call(
        functools.partial(_rwkv_pre_kernel, blocks_per_seq=bps),
        grid=(m // tm,),
        in_specs=[pl.BlockSpec((tm, c), lambda i: (i, 0)),
                  pl.BlockSpec((8, c), lambda i: (jnp.maximum(i * (tm // 8) - 1, 0), 0)),
                  _const_spec((1, c)), _const_spec(mu.shape)] + [_const_spec(w.shape) for w in ws],
        out_specs=[pair_spec] * 6 + [pl.BlockSpec((tm, c), lambda i: (i, 0))],
        out_shape=[pair_shape] * 6 + [jax.ShapeDtypeStruct((m, c), F32)],
        compiler_params=_cparams(("parallel",)),
        name="rwkv_pre",
    )(x, x, row(ng), mu, *ws)


def _stack(x, first):
    return jnp.concatenate([jnp.where(first, x, 0.0), jnp.where(first, 0.0, x)], axis=0)


def _wkv_kernel(r_ref, lw_ref, k_ref, v_ref, kk_ref, a_ref, rk_ref, lng_ref, lnb_ref, y_ref,
                s_sc, carry_sc, bv_sc, *, nblk):
    lb = r_ref.shape[2]
    nc = lb // CHUNK
    l2 = 2 * CHUNK
    i = pl.program_id(2)

    first = _half_mask((CHUNK, PAIR))
    ri = lax.broadcasted_iota(jnp.int32, (l2, l2), 0)
    ci = lax.broadcasted_iota(jnp.int32, (l2, l2), 1)
    same = (ri // CHUNK) == (ci // CHUNK)
    strict = same & (ri % CHUNK > ci % CHUNK)
    incl = same & (ri % CHUNK >= ci % CHUNK)
    eye = ri == ci
    ident = jnp.where(eye, 1.0, 0.0)
    rk = rk_ref[0]
    lng = lng_ref[0]
    lnb = lnb_ref[0]
    chs = range(nc)

    def prepare_stages():
        w = {}

        def prep():
            tri = jnp.where(lax.broadcasted_iota(jnp.int32, (CHUNK, CHUNK), 1)
                            <= lax.broadcasted_iota(jnp.int32, (CHUNK, CHUNK), 0), 1.0, 0.0).astype(BF16)
            lw_wide = jnp.concatenate([lw_ref[0, 0, ch * CHUNK:(ch + 1) * CHUNK, :] for ch in chs], axis=1)
            lw_hi = lw_wide.astype(BF16)
            lw_lo = (lw_wide - lw_hi.astype(F32)).astype(BF16)
            c_wide = (jnp.dot(tri, lw_hi, preferred_element_type=F32)
                      + jnp.dot(tri, lw_lo, preferred_element_type=F32))
            for ch in chs:
                sl = slice(ch * CHUNK, (ch + 1) * CHUNK)
                r = r_ref[0, 0, sl, :]
                k = k_ref[0, 0, sl, :]
                v = v_ref[0, 0, sl, :]
                kk = kk_ref[0, 0, sl, :]
                c = c_wide[:, ch * PAIR:(ch + 1) * PAIR]
                c_last = c[CHUNK - 1:CHUNK, :]
                e_neg = jnp.exp(-c)
                e_last = jnp.exp(c_last - c)
                b = kk * a_ref[0, 0, sl, :]
                rt = _stack(r * jnp.exp(c), first)
                at = _stack(-kk * jnp.exp(c - lw_ref[0, 0, sl, :]), first)
                w[ch] = dict(
                    rt=rt, at=at, vs=_stack(v, first), g_last=jnp.exp(c_last),
                    bonus=_pair_sum(r * k * rk, first) * v,
                    ar=jnp.concatenate([at, rt], axis=0),
                    bkt=jnp.concatenate([_stack(b * e_neg, first), _stack(k * e_neg, first)], axis=0),
                    bk_last=jnp.concatenate([_stack(b * e_last, first), _stack(k * e_last, first)], axis=0))

        def couplings():
            for ch in chs:
                d = w[ch]
                g = _mm_nt(d["ar"], d["bkt"])
                d["n"] = jnp.where(strict, g[:l2, :l2], 0.0)
                d["a_ak"] = jnp.where(strict, g[:l2, l2:], 0.0)
                d["a_rbk"] = jnp.concatenate([jnp.where(incl, g[l2:, :l2], 0.0),
                                              jnp.where(incl, g[l2:, l2:], 0.0)], axis=1)

        def start_inverse():
            for ch in chs:
                d = w[ch]
                d["aakv"] = _mm(d["a_ak"], d["vs"])
                d["tinv"] = ident + d["n"]
                d["pw"] = d["n"]

        def square():
            for ch in chs:
                w[ch]["pw"] = _mm(w[ch]["pw"], w[ch]["pw"])

        def extend():
            for ch in chs:
                w[ch]["tinv"] = w[ch]["tinv"] + _mm(w[ch]["tinv"], w[ch]["pw"])

        def solve():
            for ch in chs:
                d = w[ch]
                x = _mm(d["tinv"], jnp.concatenate([d["at"], d["aakv"]], axis=1))
                d["z"] = jnp.concatenate([x, jnp.concatenate([jnp.zeros_like(d["vs"]), d["vs"]], axis=1)],
                                         axis=0)

        def finish():
            for ch in chs:
                d = w[ch]
                m1 = _mm(d["a_rbk"], d["z"])
                m2 = _mm_tn(d["z"], d["bk_last"])
                carry_sc[ch, 0] = d["rt"] + m1[:, :PAIR]
                carry_sc[ch, 1] = m1[:, PAIR:]
                carry_sc[ch, 2] = jnp.where(eye, d["g_last"], 0.0) + m2[:PAIR]
                carry_sc[ch, 3] = m2[PAIR:]
                bv_sc[ch] = d["bonus"]

        rounds = int(math.log2(CHUNK)) - 1
        return [prep, couplings, start_inverse] + [square, extend] * rounds + [solve, finish]

    def advance(ch, s):
        ys = _mm_nt(carry_sc[ch, 0], s) + carry_sc[ch, 1]
        s = _mm(s, carry_sc[ch, 2]) + carry_sc[ch, 3]
        y = ys[:CHUNK] + ys[CHUNK:]
        mean = _pair_sum(y, first) * (1.0 / HEAD)
        d = y - mean
        var = _pair_sum(d * d, first) * (1.0 / HEAD)
        y_ref[0, 0, ch * CHUNK:(ch + 1) * CHUNK, :] = d * lax.rsqrt(var + RWKV_LN_EPS) * lng + lnb + bv_sc[ch]
        return s

    @pl.when(i == 0)
    def _():
        s_sc[...] = jnp.zeros_like(s_sc)
        for stage in prepare_stages():
            stage()

    @pl.when((i > 0) & (i < nblk))
    def _():
        s = s_sc[...]
        todo = list(chs)
        stages = prepare_stages()
        per_stage = -(-nc // (len(stages) - 2))
        for n_stage, stage in enumerate(stages):
            if n_stage == len(stages) - 1:
                while todo:
                    s = advance(todo.pop(0), s)
            stage()
            if 0 < n_stage < len(stages) - 1:
                for _ in range(min(per_stage, len(todo))):
                    s = advance(todo.pop(0), s)
        s_sc[...] = s

    @pl.when(i == nblk)
    def _():
        s = s_sc[...]
        for ch in chs:
            s = advance(ch, s)
        s_sc[...] = s


def _wkv_scan(r, lw, k, v, kk, a, r_k, ln_g, ln_b, *, lb=1024):
    batch, npair, t, _ = r.shape
    lb = min(lb, t)
    nblk = t // lb
    nc = lb // CHUNK
    in_spec = pl.BlockSpec((1, 1, lb, PAIR), lambda b, p, i: (b, p, jnp.minimum(i, nblk - 1), 0))
    out_spec = pl.BlockSpec((1, 1, lb, PAIR), lambda b, p, i: (b, p, jnp.maximum(i - 1, 0), 0))
    par_spec = pl.BlockSpec((1, 1, PAIR), lambda b, p, i: (p, 0, 0))
    par = lambda x: x.reshape(npair, 1, PAIR)
    return pl.pallas_call(
        functools.partial(_wkv_kernel, nblk=nblk),
        grid=(batch, npair, nblk + 1),
        in_specs=[in_spec] * 6 + [par_spec] * 3,
        out_specs=out_spec,
        out_shape=jax.ShapeDtypeStruct(r.shape, F32),
        scratch_shapes=[pltpu.VMEM((PAIR, PAIR), F32), pltpu.VMEM((nc, 4, PAIR, PAIR), F32),
                        pltpu.VMEM((nc, CHUNK, PAIR), F32)],
        compiler_params=_cparams(("parallel", "parallel", "arbitrary")),
        name="wkv_scan",
    )(r, lw, k, v, kk, a, par(r_k), par(ln_g), par(ln_b))


def _sublane_all(op, x):
    for shift in (4, 2, 1):
        x = op(x, pltpu.roll(x, shift, 0))
    return x


def _diff_attn_kernel(q_ref, k_ref, vt_ref, lam_ref, sg_ref, o_ref, m_sc, acc_sc, sa_sc, sb_sc, *,
                      lam_init, cg, n_heads, unroll):
    bq = vt_ref.shape[4]
    dv = LANES
    rows = vt_ref.shape[3]
    ncg = 2 * bq // cg
    nq = q_ref.shape[2] // bq
    head = (pl.program_id(1) + 1).astype(F32)
    block_bias = jnp.exp2(jnp.full((8, cg), -8.0 / n_heads, F32) * head) * (LOG2E * bq)
    groups = range(ncg)

    def q_block(i, carry):
        q = q_ref[0, 0, pl.ds(pl.multiple_of(i * bq, bq), bq), :]
        first = _half_mask((bq, LANES))
        lane = lax.broadcasted_iota(jnp.int32, (bq, LANES), 1)
        feat = jnp.zeros((bq, LANES), F32)
        for n, part in enumerate(LOG2E_PARTS):
            feat = jnp.where(lane // 2 == n, part, feat)
        feat = feat.astype(BF16)
        zero = jnp.zeros_like(q)
        qs = jnp.concatenate([jnp.concatenate([jnp.where(first, q, zero), feat], axis=1),
                              jnp.concatenate([jnp.where(first, zero, q), feat], axis=1)], axis=0)

        m_sc[...] = jnp.full_like(m_sc, NEG_BIG)
        acc_sc[...] = jnp.zeros_like(acc_sc)

        def scores(j, buf):
            kt = k_ref[0, 0, pl.ds(pl.multiple_of(j * bq, bq), bq), :]
            for g in groups:
                buf[g] = lax.dot_general(kt, qs[g * cg:(g + 1) * cg], (((1,), (1,)), ((), ())),
                                         preferred_element_type=F32)

        def tile(j, buf, diag):
            q0 = [(g * cg) % bq for g in groups]
            nk = [min(bq, q0[g] + cg) if diag else bq for g in groups]
            offset = 0.0 if diag else block_bias * (j - i).astype(F32)
            p, alpha = [], []
            for g in groups:
                sg = buf[g, :nk[g], :]
                if diag:
                    key = lax.broadcasted_iota(jnp.int32, sg.shape, 0)
                    qry = lax.broadcasted_iota(jnp.int32, sg.shape, 1) + q0[g]
                    sg = jnp.where(key <= qry, sg, NEG_BIG)
                sg = sg.reshape(nk[g] // 8, 8, cg)
                m_prev = m_sc[g]
                m_new = jnp.maximum(m_prev, _sublane_all(jnp.maximum, jnp.max(sg, axis=0)) + offset)
                p.append(jnp.exp2(sg - (m_new - offset)[None]).reshape(nk[g], cg).astype(BF16))
                alpha.append(jnp.exp2(m_prev - m_new))
                m_sc[g] = m_new
            pv = [jnp.dot(vt_ref[0, 0, j, :, :nk[g]], p[g], preferred_element_type=F32) for g in groups]
            for g in groups:
                acc_sc[g] = (alpha[g][None] * acc_sc[g].reshape(rows // 8, 8, cg) + pv[g].reshape(rows // 8, 8, cg)
                             ).reshape(rows, cg)

        bufs = (sa_sc, sb_sc)
        scores(0, sa_sc)

        def run(first_block, count, last_is_diag):
            for n in range(count):
                last = n == count - 1
                if not (last and last_is_diag):
                    scores(first_block + n + 1, bufs[(n + 1) % 2])
                tile(first_block + n, bufs[n % 2], last and last_is_diag)

        def body(jj, carry):
            run(unroll * jj, unroll, False)
            return carry

        lax.fori_loop(0, i // unroll, body, 0)
        for rem in range(unroll):
            @pl.when(i % unroll == rem)
            def _():
                run(i - rem, rem + 1, True)

        o_t = []
        for g in groups:
            l = acc_sc[g, dv:dv + 8, :]
            o_t.append((acc_sc[g, :dv, :].reshape(dv // 8, 8, cg) / l[None]).reshape(dv, cg))
        half = ncg // 2
        o1 = jnp.concatenate(o_t[:half], axis=1)
        o2 = jnp.concatenate(o_t[half:], axis=1)
        lam = lam_ref[...]
        lam_full = (jnp.exp(jnp.sum(lam[0:1] * lam[1:2], axis=-1, keepdims=True))
                    - jnp.exp(jnp.sum(lam[2:3] * lam[3:4], axis=-1, keepdims=True)) + lam_init)
        o = (o1 - lam_full * o2).T
        o_ref[0, 0, pl.ds(pl.multiple_of(i * bq, bq), bq), :] = (_rms(o, sg_ref[...], SUBLN_EPS)
                                                                 * (1.0 - lam_init)).astype(BF16)
        return carry

    lax.fori_loop(0, nq, q_block, 0)


def _diff_attn(q, k, vt, lam, subln, lam_init, *, unroll=8):
    assert unroll % 2 == 0
    batch, nh, t, _ = q.shape
    rows, bq = vt.shape[-2:]
    cg = min(256, bq)
    ncg = 2 * bq // cg
    return pl.pallas_call(
        functools.partial(_diff_attn_kernel, lam_init=lam_init, cg=cg, n_heads=nh, unroll=unroll),
        grid=(batch, nh),
        in_specs=[pl.BlockSpec((1, 1, t, LANES), lambda b, h: (b, h, 0, 0)),
                  pl.BlockSpec((1, 1, t, 2 * LANES), lambda b, h: (b, h, 0, 0)),
                  pl.BlockSpec((1, 1, t // bq, rows, bq), lambda b, h: (b, h, 0, 0, 0)),
                  _const_spec(lam.shape), _const_spec((1, LANES))],
        out_specs=pl.BlockSpec((1, 1, t, LANES), lambda b, h: (b, h, 0, 0)),
        out_shape=jax.ShapeDtypeStruct((batch, nh, t, LANES), BF16),
        scratch_shapes=[pltpu.VMEM((ncg, 8, cg), F32), pltpu.VMEM((ncg, rows, cg), F32),
                        pltpu.VMEM((ncg, bq, cg), F32), pltpu.VMEM((ncg, bq, cg), F32)],
        compiler_params=_cparams(("parallel", "parallel")),
        name="diff_attn",
    )(q, k, vt, lam, subln.reshape(1, LANES))


def kernel(x, ffn_norm, ffn_w_in, ffn_w_out, mix_norm, rwkv_mu, rwkv_w_rkv, rwkv_w0, rwkv_w1, rwkv_w2, rwkv_a0, rwkv_a1, rwkv_a2, rwkv_g1, rwkv_g2, rwkv_k_k, rwkv_k_a, rwkv_r_k, rwkv_ln_g, rwkv_ln_b, rwkv_w_o, kv_norm, w_kv, k_norm, diff_w_q, diff_q_norm, diff_lambda, diff_subln, diff_w_o):
    batch, t, c = x.shape
    x = x.reshape(batch * t, c)

    x, = _block(x, batch, ffn_norm[0, 0], ffn_w_in[0, 0], ffn_w_out[0, 0])
    r, lw, k, v, kk, a, g = _rwkv_pre(x, batch, mix_norm[0], rwkv_mu[0], rwkv_w_rkv[0], rwkv_w0[0], rwkv_w1[0],
                                      rwkv_w2[0], rwkv_a0[0], rwkv_a1[0], rwkv_a2[0], rwkv_g1[0], rwkv_g2[0],
                                      rwkv_k_k[0], rwkv_k_a[0])
    y = _wkv_scan(r, lw, k, v, kk, a, rwkv_r_k[0], rwkv_ln_g[0], rwkv_ln_b[0])
    x, k_sh, v_sh = _block(x, batch, ffn_norm[0, 1], ffn_w_in[0, 1], ffn_w_out[0, 1],
                           pre=(y, g, rwkv_w_o[0]), post=("kv", kv_norm, w_kv, k_norm))

    x, q = _block(x, batch, ffn_norm[1, 0], ffn_w_in[1, 0], ffn_w_out[1, 0],
                  post=("q", mix_norm[1], diff_w_q[0], diff_q_norm[0]))
    lam_init = 0.8 - 0.6 * math.exp(-0.3 * 1)
    o = _diff_attn(q, k_sh, v_sh, diff_lambda[0], diff_subln[0], lam_init)
    x, = _block(x, batch, ffn_norm[1, 1], ffn_w_in[1, 1], ffn_w_out[1, 1], pre=(o, None, diff_w_o[0]))
    return x.reshape(batch, t, c)
```

```python
import functools
import math

import jax
import jax.numpy as jnp
from jax import lax
from jax.experimental import pallas as pl
from jax.experimental.pallas import tpu as pltpu

F32 = jnp.float32
BF16 = jnp.bfloat16

LANES = 128
HEAD = 64
PAIR = 2 * HEAD
CHUNK = 64
ATT = 512
FFN_TF = 256
NORM_EPS = 1e-6
SUBLN_EPS = 1e-5
RWKV_LN_EPS = 64e-5
VMEM_LIMIT = 56 * 1024 * 1024
NEG_BIG = -1e30
BF16_ROWS = 16
VT_ROWS = LANES + BF16_ROWS


def _bf16_parts(x, n):
    parts = []
    for _ in range(n):
        m, e = math.frexp(x)
        p = math.ldexp(round(m * 256.0) / 256.0, e)
        parts.append(p)
        x -= p
    return tuple(parts)


LOG2E = math.log2(math.e)
LOG2E_PARTS = _bf16_parts(LOG2E, 3)


def _cparams(sem):
    return pltpu.CompilerParams(dimension_semantics=sem, vmem_limit_bytes=VMEM_LIMIT)


def _const_spec(shape):
    nd = len(shape)
    return pl.BlockSpec(shape, lambda *_: (0,) * nd, pipeline_mode=pl.Buffered(1))


def _mm(a, b):
    return jnp.dot(a.astype(BF16), b.astype(BF16), preferred_element_type=F32)


def _mm_nt(a, b):
    return lax.dot_general(a.astype(BF16), b.astype(BF16), (((1,), (1,)), ((), ())),
                           preferred_element_type=F32)


def _mm_tn(a, b):
    return lax.dot_general(a.astype(BF16), b.astype(BF16), (((0,), (0,)), ((), ())),
                           preferred_element_type=F32)


def _rms(x, g, eps):
    return x * lax.rsqrt(jnp.mean(x * x, axis=-1, keepdims=True) + eps) * g


def _half_mask(shape):
    return lax.broadcasted_iota(jnp.int32, shape, len(shape) - 1) % PAIR < HEAD


def _pair_sum(x, first):
    s1 = jnp.sum(jnp.where(first, x, 0.0), axis=-1, keepdims=True)
    s2 = jnp.sum(jnp.where(first, 0.0, x), axis=-1, keepdims=True)
    return jnp.where(first, s1, s2)


def _ffn_value(x, g_ref, win_ref, wout_ref, h_sc):
    d_ff = wout_ref.shape[0]
    xn = _rms(x, g_ref[...], NORM_EPS).astype(BF16)
    for f0 in range(0, d_ff, FFN_TF):
        gate = jnp.dot(xn, win_ref[:, f0:f0 + FFN_TF], preferred_element_type=F32)
        up = jnp.dot(xn, win_ref[:, d_ff + f0:d_ff + f0 + FFN_TF], preferred_element_type=F32)
        h_sc[:, f0:f0 + FFN_TF] = (gate * jax.nn.sigmoid(gate) * up).astype(BF16)
    return x + 0.5 * jnp.dot(h_sc[...], wout_ref[...], preferred_element_type=F32)


def _keys_values(x, ng_ref, w_ref, kg_ref, k_out, v_out):
    tm, c = x.shape
    n_heads = c // LANES
    kv = _mm(_rms(x, ng_ref[...], NORM_EPS), w_ref[...])
    first = _half_mask((tm, LANES))
    kg = kg_ref[...]
    lane = lax.broadcasted_iota(jnp.int32, (tm, LANES), 1)
    pos = lax.broadcasted_iota(jnp.int32, (tm, LANES), 0)
    base = jnp.where(lane >= 2 * len(LOG2E_PARTS), 0.0,
                     jnp.where(lane % 2 == 0, (pos // HEAD * HEAD).astype(F32), (pos % HEAD).astype(F32)))
    ones = jnp.ones((VT_ROWS - LANES, tm), BF16)
    for h in range(n_heads):
        kh = kv[:, h * LANES:(h + 1) * LANES]
        ms = _pair_sum(kh * kh, first) * (1.0 / HEAD)
        kn = kh * lax.rsqrt(ms + NORM_EPS) * kg
        slope = 2.0 ** (-8.0 * (h + 1) / n_heads)
        k_out[0, h] = jnp.concatenate([kn.astype(BF16), (base * slope).astype(BF16)], axis=-1)
        vt = kv[:, c + h * LANES:c + (h + 1) * LANES].T.astype(BF16)
        v_out[0, h, 0] = jnp.concatenate([vt, ones], axis=0)


def _queries(x, ng_ref, w_ref, qg_ref, q_out):
    tm, c = x.shape
    q = _mm(_rms(x, ng_ref[...], NORM_EPS), w_ref[...])
    first = _half_mask((tm, LANES))
    qg = qg_ref[...] * (HEAD ** -0.5 * LOG2E)
    for h in range(c // LANES):
        qh = q[:, h * LANES:(h + 1) * LANES]
        ms = _pair_sum(qh * qh, first) * (1.0 / HEAD)
        q_out[0, h] = (qh * lax.rsqrt(ms + NORM_EPS) * qg).astype(BF16)


def _block_kernel(*refs, pre, post):
    refs = list(refs)
    x = refs.pop(0)[...]
    if pre is not None:
        y_ref = refs.pop(0)
        y = jnp.concatenate([y_ref[0, p] for p in range(y_ref.shape[1])], axis=-1)
        if pre == "gated":
            y = y * refs.pop(0)[...]
        x = x + jnp.dot(y.astype(BF16), refs.pop(0)[...], preferred_element_type=F32)
    g_ref, win_ref, wout_ref = refs[:3]
    refs = refs[3:]
    post_refs = [refs.pop(0) for _ in range(3)] if post is not None else []
    o_ref = refs.pop(0)
    h_sc = refs.pop()
    x = _ffn_value(x, g_ref, win_ref, wout_ref, h_sc)
    o_ref[...] = x
    if post == "kv":
        _keys_values(x, *post_refs, *refs)
    elif post == "q":
        _queries(x, *post_refs, *refs)


def _block(x, batch, g, w_in, w_out, *, pre=None, post=None):
    m, c = x.shape
    t = m // batch
    d_ff = w_out.shape[0]
    tm = min(ATT, t)
    bps = t // tm
    nh = c // LANES
    row_spec = pl.BlockSpec((tm, c), lambda i: (i, 0))
    head_idx = lambda i: (i // bps, 0, i % bps, 0)
    args, in_specs = [x], [row_spec]
    pre_kind = None
    if pre is not None:
        y, gate, w_o = pre
        pre_kind = "plain" if gate is None else "gated"
        args.append(y)
        in_specs.append(pl.BlockSpec((1, y.shape[1], tm, LANES), head_idx))
        if gate is not None:
            args.append(gate)
            in_specs.append(row_spec)
        args.append(w_o.astype(BF16))
        in_specs.append(_const_spec((c, c)))
    args += [g.reshape(1, c), w_in.astype(BF16), w_out.astype(BF16)]
    in_specs += [_const_spec((1, c)), _const_spec((c, 2 * d_ff)), _const_spec((d_ff, c))]
    out_specs, out_shape = [row_spec], [jax.ShapeDtypeStruct((m, c), F32)]
    post_kind = None
    if post is not None:
        post_kind, ng, w, head_norm = post
        args += [ng.reshape(1, c), w.astype(BF16), jnp.concatenate([head_norm, head_norm]).reshape(1, LANES)]
        in_specs += [_const_spec((1, c)), _const_spec(w.shape), _const_spec((1, LANES))]
        if post_kind == "kv":
            out_specs += [pl.BlockSpec((1, nh, tm, 2 * LANES), head_idx),
                          pl.BlockSpec((1, nh, 1, VT_ROWS, tm), lambda i: (i // bps, 0, i % bps, 0, 0))]
            out_shape += [jax.ShapeDtypeStruct((batch, nh, t, 2 * LANES), BF16),
                          jax.ShapeDtypeStruct((batch, nh, t // tm, VT_ROWS, tm), BF16)]
        else:
            out_specs.append(pl.BlockSpec((1, nh, tm, LANES), head_idx))
            out_shape.append(jax.ShapeDtypeStruct((batch, nh, t, LANES), BF16))
    return pl.pallas_call(
        functools.partial(_block_kernel, pre=pre_kind, post=post_kind),
        grid=(m // tm,),
        in_specs=in_specs,
        out_specs=out_specs,
        out_shape=out_shape,
        scratch_shapes=[pltpu.VMEM((tm, d_ff), BF16)],
        compiler_params=_cparams(("parallel",)),
        name="block_" + (pre_kind or "x") + "_" + (post_kind or "x"),
    )(*args)


def _rwkv_pre_kernel(x_ref, xp_ref, ng_ref, mu_ref, wrkv_ref, w0_ref, w1_ref, w2_ref, a0_ref, a1_ref,
                     a2_ref, g1_ref, g2_ref, kk_ref, ka_ref,
                     r_out, lw_out, k_out, v_out, kk_out, a_out, g_out, *, blocks_per_seq):
    tm, c = x_ref.shape
    ng = ng_ref[...]
    h = _rms(x_ref[...], ng, NORM_EPS)
    prev = _rms(xp_ref[7:8, :], ng, NORM_EPS)
    prev = jnp.where(pl.program_id(0) % blocks_per_seq == 0, 0.0, prev)
    row = lax.broadcasted_iota(jnp.int32, (tm, c), 0)
    h_prev = jnp.where(row == 0, prev, pltpu.roll(h, 1, 0))
    dx = h_prev - h

    def mix(i):
        return h + dx * mu_ref[i:i + 1, :]

    r = _mm(mix(0), wrkv_ref[0])
    k = _mm(mix(1), wrkv_ref[1])
    v = _mm(mix(2), wrkv_ref[2])
    z = w0_ref[...] + _mm(jnp.tanh(_mm(mix(3), w1_ref[...])), w2_ref[...])
    w_log = -jax.nn.softplus(-z) - 0.5
    lw = -jnp.exp(w_log)
    a = jax.nn.sigmoid(a0_ref[...] + _mm(_mm(mix(4), a1_ref[...]), a2_ref[...]))
    g_out[...] = _mm(jax.nn.sigmoid(_mm(mix(5), g1_ref[...])), g2_ref[...])
    kk = k * kk_ref[...]
    k = k * (1.0 + (a - 1.0) * ka_ref[...])
    first = _half_mask((tm, PAIR))
    for p in range(c // PAIR):
        sl = slice(p * PAIR, (p + 1) * PAIR)
        kkp = kk[:, sl]
        ss = _pair_sum(kkp * kkp, first)
        r_out[0, p] = r[:, sl]
        lw_out[0, p] = lw[:, sl]
        k_out[0, p] = k[:, sl]
        v_out[0, p] = v[:, sl]
        kk_out[0, p] = kkp * lax.rsqrt(jnp.maximum(ss, 1e-24))
        a_out[0, p] = a[:, sl]


def _rwkv_pre(x, batch, ng, mu, w_rkv, w0, w1, w2, a0, a1, a2, g1, g2, k_k, k_a, *, tm=512):
    m, c = x.shape
    t = m // batch
    tm = min(tm, t)
    bps = t // tm
    npair = c // PAIR
    row = lambda a: a.reshape(1, c)
    pair_spec = pl.BlockSpec((1, npair, tm, PAIR), lambda i: (i // bps, 0, i % bps, 0))
    pair_shape = jax.ShapeDtypeStruct((batch, npair, t, PAIR), F32)
    ws = [w_rkv.astype(BF16), row(w0), w1.astype(BF16), w2.astype(BF16), row(a0), a1.astype(BF16),
          a2.astype(BF16), g1.astype(BF16), g2.astype(BF16), row(k_k), row(k_a)]
    return pl.pallas_call(
        functools.partial(_rwkv_pre_kernel, blocks_per_seq=bps),
        grid=(m // tm,),
        in_specs=[pl.BlockSpec((tm, c), lambda i: (i, 0)),
                  pl.BlockSpec((8, c), lambda i: (jnp.maximum(i * (tm // 8) - 1, 0), 0)),
                  _const_spec((1, c)), _const_spec(mu.shape)] + [_const_spec(w.shape) for w in ws],
        out_specs=[pair_spec] * 6 + [pl.BlockSpec((tm, c), lambda i: (i, 0))],
        out_shape=[pair_shape] * 6 + [jax.ShapeDtypeStruct((m, c), F32)],
        compiler_params=_cparams(("parallel",)),
        name="rwkv_pre",
    )(x, x, row(ng), mu, *ws)


def _stack(x, first):
    return jnp.concatenate([jnp.where(first, x, 0.0), jnp.where(first, 0.0, x)], axis=0)


def _wkv_kernel(r_ref, lw_ref, k_ref, v_ref, kk_ref, a_ref, rk_ref, lng_ref, lnb_ref, y_ref,
                s_sc, carry_sc, bv_sc, *, nblk):
    lb = r_ref.shape[2]
    nc = lb // CHUNK
    l2 = 2 * CHUNK
    i = pl.program_id(2)

    first = _half_mask((CHUNK, PAIR))
    ri = lax.broadcasted_iota(jnp.int32, (l2, l2), 0)
    ci = lax.broadcasted_iota(jnp.int32, (l2, l2), 1)
    same = (ri // CHUNK) == (ci // CHUNK)
    strict = same & (ri % CHUNK > ci % CHUNK)
    incl = same & (ri % CHUNK >= ci % CHUNK)
    eye = ri == ci
    ident = jnp.where(eye, 1.0, 0.0)
    rk = rk_ref[0]
    lng = lng_ref[0]
    lnb = lnb_ref[0]
    chs = range(nc)

    def prepare_stages():
        w = {}

        def prep():
            tri = jnp.where(lax.broadcasted_iota(jnp.int32, (CHUNK, CHUNK), 1)
                            <= lax.broadcasted_iota(jnp.int32, (CHUNK, CHUNK), 0), 1.0, 0.0).astype(BF16)
            lw_wide = jnp.concatenate([lw_ref[0, 0, ch * CHUNK:(ch + 1) * CHUNK, :] for ch in chs], axis=1)
            lw_hi = lw_wide.astype(BF16)
            lw_lo = (lw_wide - lw_hi.astype(F32)).astype(BF16)
            c_wide = (jnp.dot(tri, lw_hi, preferred_element_type=F32)
                      + jnp.dot(tri, lw_lo, preferred_element_type=F32))
            for ch in chs:
                sl = slice(ch * CHUNK, (ch + 1) * CHUNK)
                r = r_ref[0, 0, sl, :]
                k = k_ref[0, 0, sl, :]
                v = v_ref[0, 0, sl, :]
                kk = kk_ref[0, 0, sl, :]
                c = c_wide[:, ch * PAIR:(ch + 1) * PAIR]
                c_last = c[CHUNK - 1:CHUNK, :]
                e_neg = jnp.exp(-c)
                e_last = jnp.exp(c_last - c)
                b = kk * a_ref[0, 0, sl, :]
                rt = _stack(r * jnp.exp(c), first)
                at = _stack(-kk * jnp.exp(c - lw_ref[0, 0, sl, :]), first)
                w[ch] = dict(
                    rt=rt, at=at, vs=_stack(v, first), g_last=jnp.exp(c_last),
                    bonus=_pair_sum(r * k * rk, first) * v,
                    ar=jnp.concatenate([at, rt], axis=0),
                    bkt=jnp.concatenate([_stack(b * e_neg, first), _stack(k * e_neg, first)], axis=0),
                    bk_last=jnp.concatenate([_stack(b * e_last, first), _stack(k * e_last, first)], axis=0))

        def couplings():
            for ch in chs:
                d = w[ch]
                g = _mm_nt(d["ar"], d["bkt"])
                d["n"] = jnp.where(strict, g[:l2, :l2], 0.0)
                d["a_ak"] = jnp.where(strict, g[:l2, l2:], 0.0)
                d["a_rbk"] = jnp.concatenate([jnp.where(incl, g[l2:, :l2], 0.0),
                                              jnp.where(incl, g[l2:, l2:], 0.0)], axis=1)

        def start_inverse():
            for ch in chs:
                d = w[ch]
                d["aakv"] = _mm(d["a_ak"], d["vs"])
                d["tinv"] = ident + d["n"]
                d["pw"] = d["n"]

        def square():
            for ch in chs:
                w[ch]["pw"] = _mm(w[ch]["pw"], w[ch]["pw"])

        def extend():
            for ch in chs:
                w[ch]["tinv"] = w[ch]["tinv"] + _mm(w[ch]["tinv"], w[ch]["pw"])

        def solve():
            for ch in chs:
                d = w[ch]
                x = _mm(d["tinv"], jnp.concatenate([d["at"], d["aakv"]], axis=1))
                d["z"] = jnp.concatenate([x, jnp.concatenate([jnp.zeros_like(d["vs"]), d["vs"]], axis=1)],
                                         axis=0)

        def finish():
            for ch in chs:
                d = w[ch]
                m1 = _mm(d["a_rbk"], d["z"])
                m2 = _mm_tn(d["z"], d["bk_last"])
                carry_sc[ch, 0] = d["rt"] + m1[:, :PAIR]
                carry_sc[ch, 1] = m1[:, PAIR:]
                carry_sc[ch, 2] = jnp.where(eye, d["g_last"], 0.0) + m2[:PAIR]
                carry_sc[ch, 3] = m2[PAIR:]
                bv_sc[ch] = d["bonus"]

        rounds = int(math.log2(CHUNK)) - 1
        return [prep, couplings, start_inverse] + [square, extend] * rounds + [solve, finish]

    def advance(ch, s):
        ys = _mm_nt(carry_sc[ch, 0], s) + carry_sc[ch, 1]
        s = _mm(s, carry_sc[ch, 2]) + carry_sc[ch, 3]
        y = ys[:CHUNK] + ys[CHUNK:]
        mean = _pair_sum(y, first) * (1.0 / HEAD)
        d = y - mean
        var = _pair_sum(d * d, first) * (1.0 / HEAD)
        y_ref[0, 0, ch * CHUNK:(ch + 1) * CHUNK, :] = d * lax.rsqrt(var + RWKV_LN_EPS) * lng + lnb + bv_sc[ch]
        return s

    @pl.when(i == 0)
    def _():
        s_sc[...] = jnp.zeros_like(s_sc)
        for stage in prepare_stages():
            stage()

    @pl.when((i > 0) & (i < nblk))
    def _():
        s = s_sc[...]
        todo = list(chs)
        stages = prepare_stages()
        per_stage = -(-nc // (len(stages) - 2))
        for n_stage, stage in enumerate(stages):
            if n_stage == len(stages) - 1:
                while todo:
                    s = advance(todo.pop(0), s)
            stage()
            if 0 < n_stage < len(stages) - 1:
                for _ in range(min(per_stage, len(todo))):
                    s = advance(todo.pop(0), s)
        s_sc[...] = s

    @pl.when(i == nblk)
    def _():
        s = s_sc[...]
        for ch in chs:
            s = advance(ch, s)
        s_sc[...] = s


def _wkv_scan(r, lw, k, v, kk, a, r_k, ln_g, ln_b, *, lb=1024):
    batch, npair, t, _ = r.shape
    lb = min(lb, t)
    nblk = t // lb
    nc = lb // CHUNK
    in_spec = pl.BlockSpec((1, 1, lb, PAIR), lambda b, p, i: (b, p, jnp.minimum(i, nblk - 1), 0))
    out_spec = pl.BlockSpec((1, 1, lb, PAIR), lambda b, p, i: (b, p, jnp.maximum(i - 1, 0), 0))
    par_spec = pl.BlockSpec((1, 1, PAIR), lambda b, p, i: (p, 0, 0))
    par = lambda x: x.reshape(npair, 1, PAIR)
    return pl.pallas_call(
        functools.partial(_wkv_kernel, nblk=nblk),
        grid=(batch, npair, nblk + 1),
        in_specs=[in_spec] * 6 + [par_spec] * 3,
        out_specs=out_spec,
        out_shape=jax.ShapeDtypeStruct(r.shape, F32),
        scratch_shapes=[pltpu.VMEM((PAIR, PAIR), F32), pltpu.VMEM((nc, 4, PAIR, PAIR), F32),
                        pltpu.VMEM((nc, CHUNK, PAIR), F32)],
        compiler_params=_cparams(("parallel", "parallel", "arbitrary")),
        name="wkv_scan",
    )(r, lw, k, v, kk, a, par(r_k), par(ln_g), par(ln_b))


def _sublane_all(op, x):
    for shift in (4, 2, 1):
        x = op(x, pltpu.roll(x, shift, 0))
    return x


def _diff_attn_kernel(q_ref, k_ref, vt_ref, lam_ref, sg_ref, o_ref, m_sc, acc_sc, sa_sc, sb_sc, qs_sc, *,
                      lam_init, cg, n_heads, unroll):
    bq = vt_ref.shape[4]
    dv = LANES
    rows = vt_ref.shape[3]
    ncg = 2 * bq // cg
    nq = q_ref.shape[2] // bq
    head = (pl.program_id(1) + 1).astype(F32)
    block_bias = jnp.exp2(jnp.full((8, cg), -8.0 / n_heads, F32) * head) * (LOG2E * bq)
    groups = range(ncg)

    def stack_queries(i):
        q = q_ref[0, 0, pl.ds(pl.multiple_of(i * bq, bq), bq), :]
        first = _half_mask((bq, LANES))
        lane = lax.broadcasted_iota(jnp.int32, (bq, LANES), 1)
        feat = jnp.zeros((bq, LANES), F32)
        for n, part in enumerate(LOG2E_PARTS):
            feat = jnp.where(lane // 2 == n, part, feat)
        feat = feat.astype(BF16)
        zero = jnp.zeros_like(q)
        qs_sc[:bq] = jnp.concatenate([jnp.where(first, q, zero), feat], axis=1)
        qs_sc[bq:] = jnp.concatenate([jnp.where(first, zero, q), feat], axis=1)

    def scores(j, buf):
        kt = k_ref[0, 0, pl.ds(pl.multiple_of(j * bq, bq), bq), :]
        for g in groups:
            buf[g] = lax.dot_general(kt, qs_sc[g * cg:(g + 1) * cg], (((1,), (1,)), ((), ())),
                                     preferred_element_type=F32)

    def q_block(i, carry):
        m_sc[...] = jnp.full_like(m_sc, NEG_BIG)
        acc_sc[...] = jnp.zeros_like(acc_sc)

        def tile(j, buf, diag):
            q0 = [(g * cg) % bq for g in groups]
            nk = [min(bq, q0[g] + cg) if diag else bq for g in groups]
            offset = 0.0 if diag else block_bias * (j - i).astype(F32)
            p, alpha = [], []
            for g in groups:
                sg = buf[g, :nk[g], :]
                if diag:
                    key = lax.broadcasted_iota(jnp.int32, sg.shape, 0)
                    qry = lax.broadcasted_iota(jnp.int32, sg.shape, 1) + q0[g]
                    sg = jnp.where(key <= qry, sg, NEG_BIG)
                sg = sg.reshape(nk[g] // 8, 8, cg)
                m_prev = m_sc[g]
                m_new = jnp.maximum(m_prev, _sublane_all(jnp.maximum, jnp.max(sg, axis=0)) + offset)
                p.append(jnp.exp2(sg - (m_new - offset)[None]).reshape(nk[g], cg).astype(BF16))
                alpha.append(jnp.exp2(m_prev - m_new))
                m_sc[g] = m_new
            pv = [jnp.dot(vt_ref[0, 0, j, :, :nk[g]], p[g], preferred_element_type=F32) for g in groups]
            for g in groups:
                acc_sc[g] = (alpha[g][None] * acc_sc[g].reshape(rows // 8, 8, cg) + pv[g].reshape(rows // 8, 8, cg)
                             ).reshape(rows, cg)

        bufs = (sa_sc, sb_sc)

        def run(first_block, count, last_is_diag):
            for n in range(count):
                last = n == count - 1
                if not (last and last_is_diag):
                    scores(first_block + n + 1, bufs[(n + 1) % 2])
                tile(first_block + n, bufs[n % 2], last and last_is_diag)

        def body(jj, carry):
            run(unroll * jj, unroll, False)
            return carry

        lax.fori_loop(0, i // unroll, body, 0)
        for rem in range(unroll):
            @pl.when(i % unroll == rem)
            def _():
                run(i - rem, rem + 1, True)

        stack_queries(jnp.minimum(i + 1, nq - 1))
        scores(0, sa_sc)

        o_t = []
        for g in groups:
            l = acc_sc[g, dv:dv + 8, :]
            o_t.append((acc_sc[g, :dv, :].reshape(dv // 8, 8, cg) / l[None]).reshape(dv, cg))
        half = ncg // 2
        o1 = jnp.concatenate(o_t[:half], axis=1)
        o2 = jnp.concatenate(o_t[half:], axis=1)
        lam = lam_ref[...]
        lam_full = (jnp.exp(jnp.sum(lam[0:1] * lam[1:2], axis=-1, keepdims=True))
                    - jnp.exp(jnp.sum(lam[2:3] * lam[3:4], axis=-1, keepdims=True)) + lam_init)
        o = (o1 - lam_full * o2).T
        o_ref[0, 0, pl.ds(pl.multiple_of(i * bq, bq), bq), :] = (_rms(o, sg_ref[...], SUBLN_EPS)
                                                                 * (1.0 - lam_init)).astype(BF16)
        return carry

    stack_queries(0)
    scores(0, sa_sc)
    lax.fori_loop(0, nq, q_block, 0)


def _diff_attn(q, k, vt, lam, subln, lam_init, *, unroll=8):
    assert unroll % 2 == 0
    batch, nh, t, _ = q.shape
    rows, bq = vt.shape[-2:]
    cg = min(256, bq)
    ncg = 2 * bq // cg
    return pl.pallas_call(
        functools.partial(_diff_attn_kernel, lam_init=lam_init, cg=cg, n_heads=nh, unroll=unroll),
        grid=(batch, nh),
        in_specs=[pl.BlockSpec((1, 1, t, LANES), lambda b, h: (b, h, 0, 0)),
                  pl.BlockSpec((1, 1, t, 2 * LANES), lambda b, h: (b, h, 0, 0)),
                  pl.BlockSpec((1, 1, t // bq, rows, bq), lambda b, h: (b, h, 0, 0, 0)),
                  _const_spec(lam.shape), _const_spec((1, LANES))],
        out_specs=pl.BlockSpec((1, 1, t, LANES), lambda b, h: (b, h, 0, 0)),
        out_shape=jax.ShapeDtypeStruct((batch, nh, t, LANES), BF16),
        scratch_shapes=[pltpu.VMEM((ncg, 8, cg), F32), pltpu.VMEM((ncg, rows, cg), F32),
                        pltpu.VMEM((ncg, bq, cg), F32), pltpu.VMEM((ncg, bq, cg), F32),
                        pltpu.VMEM((2 * bq, 2 * LANES), BF16)],
        compiler_params=_cparams(("parallel", "parallel")),
        name="diff_attn",
    )(q, k, vt, lam, subln.reshape(1, LANES))


def kernel(x, ffn_norm, ffn_w_in, ffn_w_out, mix_norm, rwkv_mu, rwkv_w_rkv, rwkv_w0, rwkv_w1, rwkv_w2, rwkv_a0, rwkv_a1, rwkv_a2, rwkv_g1, rwkv_g2, rwkv_k_k, rwkv_k_a, rwkv_r_k, rwkv_ln_g, rwkv_ln_b, rwkv_w_o, kv_norm, w_kv, k_norm, diff_w_q, diff_q_norm, diff_lambda, diff_subln, diff_w_o):
    batch, t, c = x.shape
    x = x.reshape(batch * t, c)

    x, = _block(x, batch, ffn_norm[0, 0], ffn_w_in[0, 0], ffn_w_out[0, 0])
    r, lw, k, v, kk, a, g = _rwkv_pre(x, batch, mix_norm[0], rwkv_mu[0], rwkv_w_rkv[0], rwkv_w0[0], rwkv_w1[0],
                                      rwkv_w2[0], rwkv_a0[0], rwkv_a1[0], rwkv_a2[0], rwkv_g1[0], rwkv_g2[0],
                                      rwkv_k_k[0], rwkv_k_a[0])
    y = _wkv_scan(r, lw, k, v, kk, a, rwkv_r_k[0], rwkv_ln_g[0], rwkv_ln_b[0])
    x, k_sh, v_sh = _block(x, batch, ffn_norm[0, 1], ffn_w_in[0, 1], ffn_w_out[0, 1],
                           pre=(y, g, rwkv_w_o[0]), post=("kv", kv_norm, w_kv, k_norm))

    x, q = _block(x, batch, ffn_norm[1, 0], ffn_w_in[1, 0], ffn_w_out[1, 0],
                  post=("q", mix_norm[1], diff_w_q[0], diff_q_norm[0]))
    lam_init = 0.8 - 0.6 * math.exp(-0.3 * 1)
    o = _diff_attn(q, k_sh, v_sh, diff_lambda[0], diff_subln[0], lam_init)
    x, = _block(x, batch, ffn_norm[1, 1], ffn_w_in[1, 1], ffn_w_out[1, 1], pre=(o, None, diff_w_o[0]))
    return x.reshape(batch, t, c)
```

```python
import functools
import math

import jax
import jax.numpy as jnp
from jax import lax
from jax.experimental import pallas as pl
from jax.experimental.pallas import tpu as pltpu

F32 = jnp.float32
BF16 = jnp.bfloat16

LANES = 128
HEAD = 64
PAIR = 2 * HEAD
CHUNK = 64
ATT = 512
FFN_TF = 256
NORM_EPS = 1e-6
SUBLN_EPS = 1e-5
RWKV_LN_EPS = 64e-5
VMEM_LIMIT = 56 * 1024 * 1024
NEG_BIG = -1e30
BF16_ROWS = 16
VT_ROWS = LANES + BF16_ROWS


def _bf16_parts(x, n):
    parts = []
    for _ in range(n):
        m, e = math.frexp(x)
        p = math.ldexp(round(m * 256.0) / 256.0, e)
        parts.append(p)
        x -= p
    return tuple(parts)


LOG2E = math.log2(math.e)
LOG2E_PARTS = _bf16_parts(LOG2E, 3)


def _cparams(sem):
    return pltpu.CompilerParams(dimension_semantics=sem, vmem_limit_bytes=VMEM_LIMIT)


def _const_spec(shape):
    nd = len(shape)
    return pl.BlockSpec(shape, lambda *_: (0,) * nd, pipeline_mode=pl.Buffered(1))


def _mm(a, b):
    return jnp.dot(a.astype(BF16), b.astype(BF16), preferred_element_type=F32)


def _mm_nt(a, b):
    return lax.dot_general(a.astype(BF16), b.astype(BF16), (((1,), (1,)), ((), ())),
                           preferred_element_type=F32)


def _mm_tn(a, b):
    return lax.dot_general(a.astype(BF16), b.astype(BF16), (((0,), (0,)), ((), ())),
                           preferred_element_type=F32)


def _rms(x, g, eps):
    return x * lax.rsqrt(jnp.mean(x * x, axis=-1, keepdims=True) + eps) * g


def _half_mask(shape):
    return lax.broadcasted_iota(jnp.int32, shape, len(shape) - 1) % PAIR < HEAD


def _pair_sum(x, first):
    s1 = jnp.sum(jnp.where(first, x, 0.0), axis=-1, keepdims=True)
    s2 = jnp.sum(jnp.where(first, 0.0, x), axis=-1, keepdims=True)
    return jnp.where(first, s1, s2)


def _ffn_value(x, g_ref, win_ref, wout_ref, h_sc):
    d_ff = wout_ref.shape[0]
    xn = _rms(x, g_ref[...], NORM_EPS).astype(BF16)
    for f0 in range(0, d_ff, FFN_TF):
        gate = jnp.dot(xn, win_ref[:, f0:f0 + FFN_TF], preferred_element_type=F32)
        up = jnp.dot(xn, win_ref[:, d_ff + f0:d_ff + f0 + FFN_TF], preferred_element_type=F32)
        h_sc[:, f0:f0 + FFN_TF] = (gate * jax.nn.sigmoid(gate) * up).astype(BF16)
    return x + 0.5 * jnp.dot(h_sc[...], wout_ref[...], preferred_element_type=F32)


def _keys_values(x, ng_ref, w_ref, kg_ref, k_out, v_out):
    tm, c = x.shape
    n_heads = c // LANES
    kv = _mm(_rms(x, ng_ref[...], NORM_EPS), w_ref[...])
    first = _half_mask((tm, LANES))
    kg = kg_ref[...]
    lane = lax.broadcasted_iota(jnp.int32, (tm, LANES), 1)
    pos = lax.broadcasted_iota(jnp.int32, (tm, LANES), 0)
    base = jnp.where(lane >= 2 * len(LOG2E_PARTS), 0.0,
                     jnp.where(lane % 2 == 0, (pos // HEAD * HEAD).astype(F32), (pos % HEAD).astype(F32)))
    ones = jnp.ones((VT_ROWS - LANES, tm), BF16)
    for h in range(n_heads):
        kh = kv[:, h * LANES:(h + 1) * LANES]
        ms = _pair_sum(kh * kh, first) * (1.0 / HEAD)
        kn = kh * lax.rsqrt(ms + NORM_EPS) * kg
        slope = 2.0 ** (-8.0 * (h + 1) / n_heads)
        k_out[0, h] = jnp.concatenate([kn.astype(BF16), (base * slope).astype(BF16)], axis=-1)
        vt = kv[:, c + h * LANES:c + (h + 1) * LANES].T.astype(BF16)
        v_out[0, h, 0] = jnp.concatenate([vt, ones], axis=0)


def _queries(x, ng_ref, w_ref, qg_ref, q_out):
    tm, c = x.shape
    q = _mm(_rms(x, ng_ref[...], NORM_EPS), w_ref[...])
    first = _half_mask((tm, LANES))
    qg = qg_ref[...] * (HEAD ** -0.5 * LOG2E)
    for h in range(c // LANES):
        qh = q[:, h * LANES:(h + 1) * LANES]
        ms = _pair_sum(qh * qh, first) * (1.0 / HEAD)
        q_out[0, h] = (qh * lax.rsqrt(ms + NORM_EPS) * qg).astype(BF16)


def _block_kernel(*refs, pre, post):
    refs = list(refs)
    x = refs.pop(0)[...]
    if pre is not None:
        y_ref = refs.pop(0)
        y = jnp.concatenate([y_ref[0, p] for p in range(y_ref.shape[1])], axis=-1)
        if pre == "gated":
            y = y * refs.pop(0)[...]
        x = x + jnp.dot(y.astype(BF16), refs.pop(0)[...], preferred_element_type=F32)
    g_ref, win_ref, wout_ref = refs[:3]
    refs = refs[3:]
    post_refs = [refs.pop(0) for _ in range(3)] if post is not None else []
    o_ref = refs.pop(0)
    h_sc = refs.pop()
    x = _ffn_value(x, g_ref, win_ref, wout_ref, h_sc)
    o_ref[...] = x
    if post == "kv":
        _keys_values(x, *post_refs, *refs)
    elif post == "q":
        _queries(x, *post_refs, *refs)


def _block(x, batch, g, w_in, w_out, *, pre=None, post=None):
    m, c = x.shape
    t = m // batch
    d_ff = w_out.shape[0]
    tm = min(ATT, t)
    bps = t // tm
    nh = c // LANES
    row_spec = pl.BlockSpec((tm, c), lambda i: (i, 0))
    head_idx = lambda i: (i // bps, 0, i % bps, 0)
    args, in_specs = [x], [row_spec]
    pre_kind = None
    if pre is not None:
        y, gate, w_o = pre
        pre_kind = "plain" if gate is None else "gated"
        args.append(y)
        in_specs.append(pl.BlockSpec((1, y.shape[1], tm, LANES), head_idx))
        if gate is not None:
            args.append(gate)
            in_specs.append(row_spec)
        args.append(w_o.astype(BF16))
        in_specs.append(_const_spec((c, c)))
    args += [g.reshape(1, c), w_in.astype(BF16), w_out.astype(BF16)]
    in_specs += [_const_spec((1, c)), _const_spec((c, 2 * d_ff)), _const_spec((d_ff, c))]
    out_specs, out_shape = [row_spec], [jax.ShapeDtypeStruct((m, c), F32)]
    post_kind = None
    if post is not None:
        post_kind, ng, w, head_norm = post
        args += [ng.reshape(1, c), w.astype(BF16), jnp.concatenate([head_norm, head_norm]).reshape(1, LANES)]
        in_specs += [_const_spec((1, c)), _const_spec(w.shape), _const_spec((1, LANES))]
        if post_kind == "kv":
            out_specs += [pl.BlockSpec((1, nh, tm, 2 * LANES), head_idx),
                          pl.BlockSpec((1, nh, 1, VT_ROWS, tm), lambda i: (i // bps, 0, i % bps, 0, 0))]
            out_shape += [jax.ShapeDtypeStruct((batch, nh, t, 2 * LANES), BF16),
                          jax.ShapeDtypeStruct((batch, nh, t // tm, VT_ROWS, tm), BF16)]
        else:
            out_specs.append(pl.BlockSpec((1, nh, tm, LANES), head_idx))
            out_shape.append(jax.ShapeDtypeStruct((batch, nh, t, LANES), BF16))
    return pl.pallas_call(
        functools.partial(_block_kernel, pre=pre_kind, post=post_kind),
        grid=(m // tm,),
        in_specs=in_specs,
        out_specs=out_specs,
        out_shape=out_shape,
        scratch_shapes=[pltpu.VMEM((tm, d_ff), BF16)],
        compiler_params=_cparams(("parallel",)),
        name="block_" + (pre_kind or "x") + "_" + (post_kind or "x"),
    )(*args)


def _rwkv_pre_kernel(x_ref, xp_ref, ng_ref, mu_ref, wrkv_ref, w0_ref, w1_ref, w2_ref, a0_ref, a1_ref,
                     a2_ref, g1_ref, g2_ref, kk_ref, ka_ref,
                     r_out, lw_out, k_out, v_out, kk_out, a_out, g_out, *, blocks_per_seq):
    tm, c = x_ref.shape
    ng = ng_ref[...]
    h = _rms(x_ref[...], ng, NORM_EPS)
    prev = _rms(xp_ref[7:8, :], ng, NORM_EPS)
    prev = jnp.where(pl.program_id(0) % blocks_per_seq == 0, 0.0, prev)
    row = lax.broadcasted_iota(jnp.int32, (tm, c), 0)
    h_prev = jnp.where(row == 0, prev, pltpu.roll(h, 1, 0))
    dx = h_prev - h

    def mix(i):
        return h + dx * mu_ref[i:i + 1, :]

    r = _mm(mix(0), wrkv_ref[0])
    k = _mm(mix(1), wrkv_ref[1])
    v = _mm(mix(2), wrkv_ref[2])
    z = w0_ref[...] + _mm(jnp.tanh(_mm(mix(3), w1_ref[...])), w2_ref[...])
    lw = -math.exp(-0.5) * jax.nn.sigmoid(z)
    a = jax.nn.sigmoid(a0_ref[...] + _mm(_mm(mix(4), a1_ref[...]), a2_ref[...]))
    g_out[...] = _mm(jax.nn.sigmoid(_mm(mix(5), g1_ref[...])), g2_ref[...])
    kk = k * kk_ref[...]
    k = k * (1.0 + (a - 1.0) * ka_ref[...])
    first = _half_mask((tm, PAIR))
    for p in range(c // PAIR):
        sl = slice(p * PAIR, (p + 1) * PAIR)
        kkp = kk[:, sl]
        ss = _pair_sum(kkp * kkp, first)
        r_out[0, p] = r[:, sl]
        lw_out[0, p] = lw[:, sl]
        k_out[0, p] = k[:, sl]
        v_out[0, p] = v[:, sl]
        kk_out[0, p] = kkp * lax.rsqrt(jnp.maximum(ss, 1e-24))
        a_out[0, p] = a[:, sl]


def _rwkv_pre(x, batch, ng, mu, w_rkv, w0, w1, w2, a0, a1, a2, g1, g2, k_k, k_a, *, tm=512):
    m, c = x.shape
    t = m // batch
    tm = min(tm, t)
    bps = t // tm
    npair = c // PAIR
    row = lambda a: a.reshape(1, c)
    pair_spec = pl.BlockSpec((1, npair, tm, PAIR), lambda i: (i // bps, 0, i % bps, 0))
    pair_shape = jax.ShapeDtypeStruct((batch, npair, t, PAIR), F32)
    ws = [w_rkv.astype(BF16), row(w0), w1.astype(BF16), w2.astype(BF16), row(a0), a1.astype(BF16),
          a2.astype(BF16), g1.astype(BF16), g2.astype(BF16), row(k_k), row(k_a)]
    return pl.pallas_call(
        functools.partial(_rwkv_pre_kernel, blocks_per_seq=bps),
        grid=(m // tm,),
        in_specs=[pl.BlockSpec((tm, c), lambda i: (i, 0)),
                  pl.BlockSpec((8, c), lambda i: (jnp.maximum(i * (tm // 8) - 1, 0), 0)),
                  _const_spec((1, c)), _const_spec(mu.shape)] + [_const_spec(w.shape) for w in ws],
        out_specs=[pair_spec] * 6 + [pl.BlockSpec((tm, c), lambda i: (i, 0))],
        out_shape=[pair_shape] * 6 + [jax.ShapeDtypeStruct((m, c), F32)],
        compiler_params=_cparams(("parallel",)),
        name="rwkv_pre",
    )(x, x, row(ng), mu, *ws)


def _stack(x, first):
    return jnp.concatenate([jnp.where(first, x, 0.0), jnp.where(first, 0.0, x)], axis=0)


def _wkv_kernel(r_ref, lw_ref, k_ref, v_ref, kk_ref, a_ref, rk_ref, lng_ref, lnb_ref, y_ref,
                s_sc, carry_sc, bv_sc, *, nblk):
    lb = r_ref.shape[2]
    nc = lb // CHUNK
    l2 = 2 * CHUNK
    i = pl.program_id(2)

    first = _half_mask((CHUNK, PAIR))
    ri = lax.broadcasted_iota(jnp.int32, (l2, l2), 0)
    ci = lax.broadcasted_iota(jnp.int32, (l2, l2), 1)
    same = (ri // CHUNK) == (ci // CHUNK)
    strict = same & (ri % CHUNK > ci % CHUNK)
    incl = same & (ri % CHUNK >= ci % CHUNK)
    eye = ri == ci
    ident = jnp.where(eye, 1.0, 0.0)
    rk = rk_ref[0]
    lng = lng_ref[0]
    lnb = lnb_ref[0]
    chs = range(nc)

    def prepare_stages():
        w = {}

        def prep():
            tri = jnp.where(lax.broadcasted_iota(jnp.int32, (CHUNK, CHUNK), 1)
                            <= lax.broadcasted_iota(jnp.int32, (CHUNK, CHUNK), 0), 1.0, 0.0).astype(BF16)
            lw_wide = jnp.concatenate([lw_ref[0, 0, ch * CHUNK:(ch + 1) * CHUNK, :] for ch in chs], axis=1)
            lw_hi = lw_wide.astype(BF16)
            lw_lo = (lw_wide - lw_hi.astype(F32)).astype(BF16)
            c_wide = (jnp.dot(tri, lw_hi, preferred_element_type=F32)
                      + jnp.dot(tri, lw_lo, preferred_element_type=F32))
            for ch in chs:
                sl = slice(ch * CHUNK, (ch + 1) * CHUNK)
                r = r_ref[0, 0, sl, :]
                k = k_ref[0, 0, sl, :]
                v = v_ref[0, 0, sl, :]
                kk = kk_ref[0, 0, sl, :]
                c = c_wide[:, ch * PAIR:(ch + 1) * PAIR]
                c_last = c[CHUNK - 1:CHUNK, :]
                e_neg = jnp.exp(-c)
                e_last = jnp.exp(c_last - c)
                b = kk * a_ref[0, 0, sl, :]
                rt = _stack(r * jnp.exp(c), first)
                at = _stack(-kk * jnp.exp(c - lw_ref[0, 0, sl, :]), first)
                w[ch] = dict(
                    rt=rt, at=at, vs=_stack(v, first), g_last=jnp.exp(c_last),
                    bonus=_pair_sum(r * k * rk, first) * v,
                    ar=jnp.concatenate([at, rt], axis=0),
                    bkt=jnp.concatenate([_stack(b * e_neg, first), _stack(k * e_neg, first)], axis=0),
                    bk_last=jnp.concatenate([_stack(b * e_last, first), _stack(k * e_last, first)], axis=0))

        def couplings():
            for ch in chs:
                d = w[ch]
                g = _mm_nt(d["ar"], d["bkt"])
                d["n"] = jnp.where(strict, g[:l2, :l2], 0.0)
                d["a_ak"] = jnp.where(strict, g[:l2, l2:], 0.0)
                d["a_rbk"] = jnp.concatenate([jnp.where(incl, g[l2:, :l2], 0.0),
                                              jnp.where(incl, g[l2:, l2:], 0.0)], axis=1)

        def start_inverse():
            for ch in chs:
                d = w[ch]
                d["aakv"] = _mm(d["a_ak"], d["vs"])
                d["tinv"] = ident + d["n"]
                d["pw"] = d["n"]

        def square():
            for ch in chs:
                w[ch]["pw"] = _mm(w[ch]["pw"], w[ch]["pw"])

        def extend():
            for ch in chs:
                w[ch]["tinv"] = w[ch]["tinv"] + _mm(w[ch]["tinv"], w[ch]["pw"])

        def solve():
            for ch in chs:
                d = w[ch]
                x = _mm(d["tinv"], jnp.concatenate([d["at"], d["aakv"]], axis=1))
                d["z"] = jnp.concatenate([x, jnp.concatenate([jnp.zeros_like(d["vs"]), d["vs"]], axis=1)],
                                         axis=0)

        def finish():
            for ch in chs:
                d = w[ch]
                m1 = _mm(d["a_rbk"], d["z"])
                m2 = _mm_tn(d["z"], d["bk_last"])
                carry_sc[ch, 0] = d["rt"] + m1[:, :PAIR]
                carry_sc[ch, 1] = m1[:, PAIR:]
                carry_sc[ch, 2] = jnp.where(eye, d["g_last"], 0.0) + m2[:PAIR]
                carry_sc[ch, 3] = m2[PAIR:]
                bv_sc[ch] = d["bonus"]

        rounds = int(math.log2(CHUNK)) - 1
        return [prep, couplings, start_inverse] + [square, extend] * rounds + [solve, finish]

    def advance(ch, s):
        ys = _mm_nt(carry_sc[ch, 0], s) + carry_sc[ch, 1]
        s = _mm(s, carry_sc[ch, 2]) + carry_sc[ch, 3]
        y = ys[:CHUNK] + ys[CHUNK:]
        mean = _pair_sum(y, first) * (1.0 / HEAD)
        d = y - mean
        var = _pair_sum(d * d, first) * (1.0 / HEAD)
        y_ref[0, 0, ch * CHUNK:(ch + 1) * CHUNK, :] = d * lax.rsqrt(var + RWKV_LN_EPS) * lng + lnb + bv_sc[ch]
        return s

    @pl.when(i == 0)
    def _():
        s_sc[...] = jnp.zeros_like(s_sc)
        for stage in prepare_stages():
            stage()

    @pl.when((i > 0) & (i < nblk))
    def _():
        s = s_sc[...]
        todo = list(chs)
        stages = prepare_stages()
        per_stage = -(-nc // (len(stages) - 2))
        for n_stage, stage in enumerate(stages):
            if n_stage == len(stages) - 1:
                while todo:
                    s = advance(todo.pop(0), s)
            stage()
            if 0 < n_stage < len(stages) - 1:
                for _ in range(min(per_stage, len(todo))):
                    s = advance(todo.pop(0), s)
        s_sc[...] = s

    @pl.when(i == nblk)
    def _():
        s = s_sc[...]
        for ch in chs:
            s = advance(ch, s)
        s_sc[...] = s


def _wkv_scan(r, lw, k, v, kk, a, r_k, ln_g, ln_b, *, lb=1024):
    batch, npair, t, _ = r.shape
    lb = min(lb, t)
    nblk = t // lb
    nc = lb // CHUNK
    in_spec = pl.BlockSpec((1, 1, lb, PAIR), lambda b, p, i: (b, p, jnp.minimum(i, nblk - 1), 0))
    out_spec = pl.BlockSpec((1, 1, lb, PAIR), lambda b, p, i: (b, p, jnp.maximum(i - 1, 0), 0))
    par_spec = pl.BlockSpec((1, 1, PAIR), lambda b, p, i: (p, 0, 0))
    par = lambda x: x.reshape(npair, 1, PAIR)
    return pl.pallas_call(
        functools.partial(_wkv_kernel, nblk=nblk),
        grid=(batch, npair, nblk + 1),
        in_specs=[in_spec] * 6 + [par_spec] * 3,
        out_specs=out_spec,
        out_shape=jax.ShapeDtypeStruct(r.shape, F32),
        scratch_shapes=[pltpu.VMEM((PAIR, PAIR), F32), pltpu.VMEM((nc, 4, PAIR, PAIR), F32),
                        pltpu.VMEM((nc, CHUNK, PAIR), F32)],
        compiler_params=_cparams(("parallel", "parallel", "arbitrary")),
        name="wkv_scan",
    )(r, lw, k, v, kk, a, par(r_k), par(ln_g), par(ln_b))


def _sublane_all(op, x):
    for shift in (4, 2, 1):
        x = op(x, pltpu.roll(x, shift, 0))
    return x


def _diff_attn_kernel(q_ref, k_ref, vt_ref, lam_ref, sg_ref, o_ref, m_sc, acc_sc, sa_sc, sb_sc, qs_sc, *,
                      lam_init, cg, n_heads, unroll):
    bq = vt_ref.shape[4]
    dv = LANES
    rows = vt_ref.shape[3]
    ncg = 2 * bq // cg
    nq = q_ref.shape[2] // bq
    head = (pl.program_id(1) + 1).astype(F32)
    block_bias = jnp.exp2(jnp.full((8, cg), -8.0 / n_heads, F32) * head) * (LOG2E * bq)
    groups = range(ncg)

    def stack_queries(i):
        q = q_ref[0, 0, pl.ds(pl.multiple_of(i * bq, bq), bq), :]
        first = _half_mask((bq, LANES))
        lane = lax.broadcasted_iota(jnp.int32, (bq, LANES), 1)
        feat = jnp.zeros((bq, LANES), F32)
        for n, part in enumerate(LOG2E_PARTS):
            feat = jnp.where(lane // 2 == n, part, feat)
        feat = feat.astype(BF16)
        zero = jnp.zeros_like(q)
        qs_sc[:bq] = jnp.concatenate([jnp.where(first, q, zero), feat], axis=1)
        qs_sc[bq:] = jnp.concatenate([jnp.where(first, zero, q), feat], axis=1)

    def scores(j, buf):
        kt = k_ref[0, 0, pl.ds(pl.multiple_of(j * bq, bq), bq), :]
        for g in groups:
            buf[g] = lax.dot_general(kt, qs_sc[g * cg:(g + 1) * cg], (((1,), (1,)), ((), ())),
                                     preferred_element_type=F32)

    def q_block(i, carry):
        m_sc[...] = jnp.full_like(m_sc, NEG_BIG)
        acc_sc[...] = jnp.zeros_like(acc_sc)

        def tile(j, buf, diag):
            q0 = [(g * cg) % bq for g in groups]
            nk = [min(bq, q0[g] + cg) if diag else bq for g in groups]
            offset = 0.0 if diag else block_bias * lax.convert_element_type(j - i, F32)
            p, alpha = [], []
            for g in groups:
                sg = buf[g, :nk[g], :]
                if diag:
                    key = lax.broadcasted_iota(jnp.int32, sg.shape, 0)
                    qry = lax.broadcasted_iota(jnp.int32, sg.shape, 1) + q0[g]
                    sg = jnp.where(key <= qry, sg, NEG_BIG)
                sg = sg.reshape(nk[g] // 8, 8, cg)
                m_prev = m_sc[g]
                m_new = jnp.maximum(m_prev, _sublane_all(jnp.maximum, jnp.max(sg, axis=0)) + offset)
                p.append(jnp.exp2(sg - (m_new - offset)[None]).reshape(nk[g], cg).astype(BF16))
                alpha.append(jnp.exp2(m_prev - m_new))
                m_sc[g] = m_new
            pv = [jnp.dot(vt_ref[0, 0, j, :, :nk[g]], p[g], preferred_element_type=F32) for g in groups]
            for g in groups:
                acc_sc[g] = (alpha[g][None] * acc_sc[g].reshape(rows // 8, 8, cg) + pv[g].reshape(rows // 8, 8, cg)
                             ).reshape(rows, cg)

        bufs = (sa_sc, sb_sc)

        def run(first_block, count, last_is_diag):
            for n in range(count):
                last = n == count - 1
                if not (last and last_is_diag):
                    scores(first_block + n + 1, bufs[(n + 1) % 2])
                tile(first_block + n, bufs[n % 2], last and last_is_diag)

        def body(jj, carry):
            run(unroll * jj, unroll, False)
            return carry

        lax.fori_loop(0, i // unroll, body, 0)
        for rem in range(unroll):
            @pl.when(i % unroll == rem)
            def _():
                run(i - rem, rem + 1, True)

        stack_queries(jnp.minimum(i + 1, nq - 1))
        scores(0, sa_sc)

        o_t = []
        for g in groups:
            l = acc_sc[g, dv:dv + 8, :]
            o_t.append((acc_sc[g, :dv, :].reshape(dv // 8, 8, cg) / l[None]).reshape(dv, cg))
        half = ncg // 2
        o1 = jnp.concatenate(o_t[:half], axis=1)
        o2 = jnp.concatenate(o_t[half:], axis=1)
        lam = lam_ref[...]
        lam_full = (jnp.exp(jnp.sum(lam[0:1] * lam[1:2], axis=-1, keepdims=True))
                    - jnp.exp(jnp.sum(lam[2:3] * lam[3:4], axis=-1, keepdims=True)) + lam_init)
        o = (o1 - lam_full * o2).T
        o_ref[0, 0, pl.ds(pl.multiple_of(i * bq, bq), bq), :] = (_rms(o, sg_ref[...], SUBLN_EPS)
                                                                 * (1.0 - lam_init)).astype(BF16)
        return carry

    stack_queries(0)
    scores(0, sa_sc)
    lax.fori_loop(0, nq, q_block, 0)


def _diff_attn(q, k, vt, lam, subln, lam_init, *, unroll=8):
    assert unroll % 2 == 0
    batch, nh, t, _ = q.shape
    rows, bq = vt.shape[-2:]
    cg = min(256, bq)
    ncg = 2 * bq // cg
    return pl.pallas_call(
        functools.partial(_diff_attn_kernel, lam_init=lam_init, cg=cg, n_heads=nh, unroll=unroll),
        grid=(batch, nh),
        in_specs=[pl.BlockSpec((1, 1, t, LANES), lambda b, h: (b, h, 0, 0)),
                  pl.BlockSpec((1, 1, t, 2 * LANES), lambda b, h: (b, h, 0, 0)),
                  pl.BlockSpec((1, 1, t // bq, rows, bq), lambda b, h: (b, h, 0, 0, 0)),
                  _const_spec(lam.shape), _const_spec((1, LANES))],
        out_specs=pl.BlockSpec((1, 1, t, LANES), lambda b, h: (b, h, 0, 0)),
        out_shape=jax.ShapeDtypeStruct((batch, nh, t, LANES), BF16),
        scratch_shapes=[pltpu.VMEM((ncg, 8, cg), F32), pltpu.VMEM((ncg, rows, cg), F32),
                        pltpu.VMEM((ncg, bq, cg), F32), pltpu.VMEM((ncg, bq, cg), F32),
                        pltpu.VMEM((2 * bq, 2 * LANES), BF16)],
        compiler_params=_cparams(("parallel", "parallel")),
        name="diff_attn",
    )(q, k, vt, lam, subln.reshape(1, LANES))


def kernel(x, ffn_norm, ffn_w_in, ffn_w_out, mix_norm, rwkv_mu, rwkv_w_rkv, rwkv_w0, rwkv_w1, rwkv_w2, rwkv_a0, rwkv_a1, rwkv_a2, rwkv_g1, rwkv_g2, rwkv_k_k, rwkv_k_a, rwkv_r_k, rwkv_ln_g, rwkv_ln_b, rwkv_w_o, kv_norm, w_kv, k_norm, diff_w_q, diff_q_norm, diff_lambda, diff_subln, diff_w_o):
    batch, t, c = x.shape
    x = x.reshape(batch * t, c)

    x, = _block(x, batch, ffn_norm[0, 0], ffn_w_in[0, 0], ffn_w_out[0, 0])
    r, lw, k, v, kk, a, g = _rwkv_pre(x, batch, mix_norm[0], rwkv_mu[0], rwkv_w_rkv[0], rwkv_w0[0], rwkv_w1[0],
                                      rwkv_w2[0], rwkv_a0[0], rwkv_a1[0], rwkv_a2[0], rwkv_g1[0], rwkv_g2[0],
                                      rwkv_k_k[0], rwkv_k_a[0])
    y = _wkv_scan(r, lw, k, v, kk, a, rwkv_r_k[0], rwkv_ln_g[0], rwkv_ln_b[0])
    x, k_sh, v_sh = _block(x, batch, ffn_norm[0, 1], ffn_w_in[0, 1], ffn_w_out[0, 1],
                           pre=(y, g, rwkv_w_o[0]), post=("kv", kv_norm, w_kv, k_norm))

    x, q = _block(x, batch, ffn_norm[1, 0], ffn_w_in[1, 0], ffn_w_out[1, 0],
                  post=("q", mix_norm[1], diff_w_q[0], diff_q_norm[0]))
    lam_init = 0.8 - 0.6 * math.exp(-0.3 * 1)
    o = _diff_attn(q, k_sh, v_sh, diff_lambda[0], diff_subln[0], lam_init)
    x, = _block(x, batch, ffn_norm[1, 1], ffn_w_in[1, 1], ffn_w_out[1, 1], pre=(o, None, diff_w_o[0]))
    return x.reshape(batch, t, c)
```

```python
import functools
import math

import jax
import jax.numpy as jnp
from jax import lax
from jax.experimental import pallas as pl
from jax.experimental.pallas import tpu as pltpu

F32 = jnp.float32
BF16 = jnp.bfloat16

LANES = 128
SUBLANES = 8
MXU_WIDTH = 256
HEAD = 64
PAIR = 2 * HEAD
CHUNK = 64
ATT = 512
FFN_TF = MXU_WIDTH
NORM_EPS = 1e-6
SUBLN_EPS = 1e-5
RWKV_LN_EPS = 64e-5
VMEM_LIMIT = 56 * 1024 * 1024
NEG_BIG = -1e30
BF16_ROWS = 16
VT_ROWS = LANES + BF16_ROWS


def _bf16_parts(x, n):
    parts = []
    for _ in range(n):
        m, e = math.frexp(x)
        p = math.ldexp(round(m * 256.0) / 256.0, e)
        parts.append(p)
        x -= p
    return tuple(parts)


LOG2E = math.log2(math.e)
LOG2E_PARTS = _bf16_parts(LOG2E, 3)


def _cparams(sem):
    return pltpu.CompilerParams(dimension_semantics=sem, vmem_limit_bytes=VMEM_LIMIT)


def _const_spec(shape):
    nd = len(shape)
    return pl.BlockSpec(shape, lambda *_: (0,) * nd, pipeline_mode=pl.Buffered(1))


def _mm(a, b):
    return jnp.dot(a.astype(BF16), b.astype(BF16), preferred_element_type=F32)


def _mm_nt(a, b):
    return lax.dot_general(a.astype(BF16), b.astype(BF16), (((1,), (1,)), ((), ())),
                           preferred_element_type=F32)


def _mm_tn(a, b):
    return lax.dot_general(a.astype(BF16), b.astype(BF16), (((0,), (0,)), ((), ())),
                           preferred_element_type=F32)


def _rms(x, g, eps):
    return x * lax.rsqrt(jnp.mean(x * x, axis=-1, keepdims=True) + eps) * g


def _half_mask(shape):
    return lax.broadcasted_iota(jnp.int32, shape, len(shape) - 1) % PAIR < HEAD


def _pair_sum(x, first):
    s1 = jnp.sum(jnp.where(first, x, 0.0), axis=-1, keepdims=True)
    s2 = jnp.sum(jnp.where(first, 0.0, x), axis=-1, keepdims=True)
    return jnp.where(first, s1, s2)


def _ffn_value(x, g_ref, win_ref, wout_ref, h_sc):
    d_ff = wout_ref.shape[0]
    xn = _rms(x, g_ref[...], NORM_EPS).astype(BF16)
    for f0 in range(0, d_ff, FFN_TF):
        gate = jnp.dot(xn, win_ref[:, f0:f0 + FFN_TF], preferred_element_type=F32)
        up = jnp.dot(xn, win_ref[:, d_ff + f0:d_ff + f0 + FFN_TF], preferred_element_type=F32)
        h_sc[:, f0:f0 + FFN_TF] = (gate * jax.nn.sigmoid(gate) * up).astype(BF16)
    return x + 0.5 * jnp.dot(h_sc[...], wout_ref[...], preferred_element_type=F32)


def _keys_values(x, ng_ref, w_ref, kg_ref, k_out, v_out):
    tm, c = x.shape
    n_heads = c // LANES
    kv = _mm(_rms(x, ng_ref[...], NORM_EPS), w_ref[...])
    first = _half_mask((tm, LANES))
    kg = kg_ref[...]
    lane = lax.broadcasted_iota(jnp.int32, (tm, LANES), 1)
    pos = lax.broadcasted_iota(jnp.int32, (tm, LANES), 0)
    base = jnp.where(lane >= 2 * len(LOG2E_PARTS), 0.0,
                     jnp.where(lane % 2 == 0, (pos // HEAD * HEAD).astype(F32), (pos % HEAD).astype(F32)))
    ones = jnp.ones((VT_ROWS - LANES, tm), BF16)
    for h in range(n_heads):
        kh = kv[:, h * LANES:(h + 1) * LANES]
        ms = _pair_sum(kh * kh, first) * (1.0 / HEAD)
        kn = kh * lax.rsqrt(ms + NORM_EPS) * kg
        slope = 2.0 ** (-8.0 * (h + 1) / n_heads)
        k_out[0, h] = jnp.concatenate([kn.astype(BF16), (base * slope).astype(BF16)], axis=-1)
        vt = kv[:, c + h * LANES:c + (h + 1) * LANES].T.astype(BF16)
        v_out[0, h, 0] = jnp.concatenate([vt, ones], axis=0)


def _queries(x, ng_ref, w_ref, qg_ref, q_out):
    tm, c = x.shape
    q = _mm(_rms(x, ng_ref[...], NORM_EPS), w_ref[...])
    first = _half_mask((tm, LANES))
    qg = qg_ref[...] * (HEAD ** -0.5 * LOG2E)
    for h in range(c // LANES):
        qh = q[:, h * LANES:(h + 1) * LANES]
        ms = _pair_sum(qh * qh, first) * (1.0 / HEAD)
        q_out[0, h] = (qh * lax.rsqrt(ms + NORM_EPS) * qg).astype(BF16)


def _block_kernel(*refs, pre, post):
    refs = list(refs)
    x = refs.pop(0)[...]
    if pre is not None:
        y_ref = refs.pop(0)
        y = jnp.concatenate([y_ref[0, p] for p in range(y_ref.shape[1])], axis=-1)
        if pre == "gated":
            y = y * refs.pop(0)[...]
        x = x + jnp.dot(y.astype(BF16), refs.pop(0)[...], preferred_element_type=F32)
    g_ref, win_ref, wout_ref = refs[:3]
    refs = refs[3:]
    post_refs = [refs.pop(0) for _ in range(3)] if post is not None else []
    o_ref = refs.pop(0)
    h_sc = refs.pop()
    x = _ffn_value(x, g_ref, win_ref, wout_ref, h_sc)
    o_ref[...] = x
    if post == "kv":
        _keys_values(x, *post_refs, *refs)
    elif post == "q":
        _queries(x, *post_refs, *refs)


def _block(x, batch, g, w_in, w_out, *, pre=None, post=None):
    m, c = x.shape
    t = m // batch
    d_ff = w_out.shape[0]
    tm = min(ATT, t)
    bps = t // tm
    nh = c // LANES
    row_spec = pl.BlockSpec((tm, c), lambda i: (i, 0))
    head_idx = lambda i: (i // bps, 0, i % bps, 0)
    args, in_specs = [x], [row_spec]
    pre_kind = None
    if pre is not None:
        y, gate, w_o = pre
        pre_kind = "plain" if gate is None else "gated"
        args.append(y)
        in_specs.append(pl.BlockSpec((1, y.shape[1], tm, LANES), head_idx))
        if gate is not None:
            args.append(gate)
            in_specs.append(row_spec)
        args.append(w_o.astype(BF16))
        in_specs.append(_const_spec((c, c)))
    args += [g.reshape(1, c), w_in.astype(BF16), w_out.astype(BF16)]
    in_specs += [_const_spec((1, c)), _const_spec((c, 2 * d_ff)), _const_spec((d_ff, c))]
    out_specs, out_shape = [row_spec], [jax.ShapeDtypeStruct((m, c), F32)]
    post_kind = None
    if post is not None:
        post_kind, ng, w, head_norm = post
        args += [ng.reshape(1, c), w.astype(BF16), jnp.concatenate([head_norm, head_norm]).reshape(1, LANES)]
        in_specs += [_const_spec((1, c)), _const_spec(w.shape), _const_spec((1, LANES))]
        if post_kind == "kv":
            out_specs += [pl.BlockSpec((1, nh, tm, 2 * LANES), head_idx),
                          pl.BlockSpec((1, nh, 1, VT_ROWS, tm), lambda i: (i // bps, 0, i % bps, 0, 0))]
            out_shape += [jax.ShapeDtypeStruct((batch, nh, t, 2 * LANES), BF16),
                          jax.ShapeDtypeStruct((batch, nh, t // tm, VT_ROWS, tm), BF16)]
        else:
            out_specs.append(pl.BlockSpec((1, nh, tm, LANES), head_idx))
            out_shape.append(jax.ShapeDtypeStruct((batch, nh, t, LANES), BF16))
    return pl.pallas_call(
        functools.partial(_block_kernel, pre=pre_kind, post=post_kind),
        grid=(m // tm,),
        in_specs=in_specs,
        out_specs=out_specs,
        out_shape=out_shape,
        scratch_shapes=[pltpu.VMEM((tm, d_ff), BF16)],
        compiler_params=_cparams(("parallel",)),
        name="block_" + (pre_kind or "x") + "_" + (post_kind or "x"),
    )(*args)


def _rwkv_pre_kernel(x_ref, xp_ref, ng_ref, mu_ref, wrkv_ref, w0_ref, w1_ref, w2_ref, a0_ref, a1_ref,
                     a2_ref, g1_ref, g2_ref, kk_ref, ka_ref,
                     r_out, lw_out, k_out, v_out, kk_out, a_out, g_out, *, blocks_per_seq):
    tm, c = x_ref.shape
    ng = ng_ref[...]
    h = _rms(x_ref[...], ng, NORM_EPS)
    prev = _rms(xp_ref[SUBLANES - 1:SUBLANES, :], ng, NORM_EPS)
    prev = jnp.where(pl.program_id(0) % blocks_per_seq == 0, 0.0, prev)
    row = lax.broadcasted_iota(jnp.int32, (tm, c), 0)
    h_prev = jnp.where(row == 0, prev, pltpu.roll(h, 1, 0))
    dx = h_prev - h

    def mix(i):
        return h + dx * mu_ref[i:i + 1, :]

    r = _mm(mix(0), wrkv_ref[0])
    k = _mm(mix(1), wrkv_ref[1])
    v = _mm(mix(2), wrkv_ref[2])
    z = w0_ref[...] + _mm(jnp.tanh(_mm(mix(3), w1_ref[...])), w2_ref[...])
    lw = -math.exp(-0.5) * jax.nn.sigmoid(z)
    a = jax.nn.sigmoid(a0_ref[...] + _mm(_mm(mix(4), a1_ref[...]), a2_ref[...]))
    g_out[...] = _mm(jax.nn.sigmoid(_mm(mix(5), g1_ref[...])), g2_ref[...])
    kk = k * kk_ref[...]
    k = k * (1.0 + (a - 1.0) * ka_ref[...])
    first = _half_mask((tm, PAIR))
    for p in range(c // PAIR):
        sl = slice(p * PAIR, (p + 1) * PAIR)
        kkp = kk[:, sl]
        ss = _pair_sum(kkp * kkp, first)
        r_out[0, p] = r[:, sl]
        lw_out[0, p] = lw[:, sl]
        k_out[0, p] = k[:, sl]
        v_out[0, p] = v[:, sl]
        kk_out[0, p] = kkp * lax.rsqrt(jnp.maximum(ss, 1e-24))
        a_out[0, p] = a[:, sl]


def _rwkv_pre(x, batch, ng, mu, w_rkv, w0, w1, w2, a0, a1, a2, g1, g2, k_k, k_a, *, tm=512):
    m, c = x.shape
    t = m // batch
    tm = min(tm, t)
    bps = t // tm
    npair = c // PAIR
    row = lambda a: a.reshape(1, c)
    pair_spec = pl.BlockSpec((1, npair, tm, PAIR), lambda i: (i // bps, 0, i % bps, 0))
    pair_shape = jax.ShapeDtypeStruct((batch, npair, t, PAIR), F32)
    ws = [w_rkv.astype(BF16), row(w0), w1.astype(BF16), w2.astype(BF16), row(a0), a1.astype(BF16),
          a2.astype(BF16), g1.astype(BF16), g2.astype(BF16), row(k_k), row(k_a)]
    return pl.pallas_call(
        functools.partial(_rwkv_pre_kernel, blocks_per_seq=bps),
        grid=(m // tm,),
        in_specs=[pl.BlockSpec((tm, c), lambda i: (i, 0)),
                  pl.BlockSpec((SUBLANES, c), lambda i: (jnp.maximum(i * (tm // SUBLANES) - 1, 0), 0)),
                  _const_spec((1, c)), _const_spec(mu.shape)] + [_const_spec(w.shape) for w in ws],
        out_specs=[pair_spec] * 6 + [pl.BlockSpec((tm, c), lambda i: (i, 0))],
        out_shape=[pair_shape] * 6 + [jax.ShapeDtypeStruct((m, c), F32)],
        compiler_params=_cparams(("parallel",)),
        name="rwkv_pre",
    )(x, x, row(ng), mu, *ws)


def _stack(x, first):
    return jnp.concatenate([jnp.where(first, x, 0.0), jnp.where(first, 0.0, x)], axis=0)


def _wkv_kernel(r_ref, lw_ref, k_ref, v_ref, kk_ref, a_ref, rk_ref, lng_ref, lnb_ref, y_ref,
                s_sc, carry_sc, bv_sc, *, nblk):
    lb = r_ref.shape[2]
    nc = lb // CHUNK
    l2 = 2 * CHUNK
    i = pl.program_id(2)

    first = _half_mask((CHUNK, PAIR))
    ri = lax.broadcasted_iota(jnp.int32, (l2, l2), 0)
    ci = lax.broadcasted_iota(jnp.int32, (l2, l2), 1)
    same = (ri // CHUNK) == (ci // CHUNK)
    strict = same & (ri % CHUNK > ci % CHUNK)
    incl = same & (ri % CHUNK >= ci % CHUNK)
    eye = ri == ci
    ident = jnp.where(eye, 1.0, 0.0)
    rk = rk_ref[0]
    lng = lng_ref[0]
    lnb = lnb_ref[0]
    chs = range(nc)

    def prepare_stages():
        w = {}

        def prep():
            tri = jnp.where(lax.broadcasted_iota(jnp.int32, (CHUNK, CHUNK), 1)
                            <= lax.broadcasted_iota(jnp.int32, (CHUNK, CHUNK), 0), 1.0, 0.0).astype(BF16)
            lw_wide = jnp.concatenate([lw_ref[0, 0, ch * CHUNK:(ch + 1) * CHUNK, :] for ch in chs], axis=1)
            lw_hi = lw_wide.astype(BF16)
            lw_lo = (lw_wide - lw_hi.astype(F32)).astype(BF16)
            c_wide = (jnp.dot(tri, lw_hi, preferred_element_type=F32)
                      + jnp.dot(tri, lw_lo, preferred_element_type=F32))
            for ch in chs:
                sl = slice(ch * CHUNK, (ch + 1) * CHUNK)
                r = r_ref[0, 0, sl, :]
                k = k_ref[0, 0, sl, :]
                v = v_ref[0, 0, sl, :]
                kk = kk_ref[0, 0, sl, :]
                c = c_wide[:, ch * PAIR:(ch + 1) * PAIR]
                c_last = c[CHUNK - 1:CHUNK, :]
                e_neg = jnp.exp(-c)
                e_last = jnp.exp(c_last - c)
                b = kk * a_ref[0, 0, sl, :]
                rt = _stack(r * jnp.exp(c), first)
                at = _stack(-kk * jnp.exp(c - lw_ref[0, 0, sl, :]), first)
                w[ch] = dict(
                    rt=rt, at=at, vs=_stack(v, first), g_last=jnp.exp(c_last),
                    bonus=_pair_sum(r * k * rk, first) * v,
                    ar=jnp.concatenate([at, rt], axis=0),
                    bkt=jnp.concatenate([_stack(b * e_neg, first), _stack(k * e_neg, first)], axis=0),
                    bk_last=jnp.concatenate([_stack(b * e_last, first), _stack(k * e_last, first)], axis=0))

        def couplings():
            for ch in chs:
                d = w[ch]
                g = _mm_nt(d["ar"], d["bkt"])
                d["n"] = jnp.where(strict, g[:l2, :l2], 0.0)
                d["a_ak"] = jnp.where(strict, g[:l2, l2:], 0.0)
                d["a_rbk"] = jnp.concatenate([jnp.where(incl, g[l2:, :l2], 0.0),
                                              jnp.where(incl, g[l2:, l2:], 0.0)], axis=1)

        def start_inverse():
            for ch in chs:
                d = w[ch]
                d["aakv"] = _mm(d["a_ak"], d["vs"])
                d["tinv"] = ident + d["n"]
                d["pw"] = d["n"]

        def square():
            for ch in chs:
                w[ch]["pw"] = _mm(w[ch]["pw"], w[ch]["pw"])

        def extend():
            for ch in chs:
                w[ch]["tinv"] = w[ch]["tinv"] + _mm(w[ch]["tinv"], w[ch]["pw"])

        def solve():
            for ch in chs:
                d = w[ch]
                x = _mm(d["tinv"], jnp.concatenate([d["at"], d["aakv"]], axis=1))
                d["z"] = jnp.concatenate([x, jnp.concatenate([jnp.zeros_like(d["vs"]), d["vs"]], axis=1)],
                                         axis=0)

        def finish():
            for ch in chs:
                d = w[ch]
                m1 = _mm(d["a_rbk"], d["z"])
                m2 = _mm_tn(d["z"], d["bk_last"])
                carry_sc[ch, 0] = d["rt"] + m1[:, :PAIR]
                carry_sc[ch, 1] = m1[:, PAIR:]
                carry_sc[ch, 2] = jnp.where(eye, d["g_last"], 0.0) + m2[:PAIR]
                carry_sc[ch, 3] = m2[PAIR:]
                bv_sc[ch] = d["bonus"]

        rounds = int(math.log2(CHUNK)) - 1
        return [prep, couplings, start_inverse] + [square, extend] * rounds + [solve, finish]

    def advance(ch, s):
        ys = _mm_nt(carry_sc[ch, 0], s) + carry_sc[ch, 1]
        s = _mm(s, carry_sc[ch, 2]) + carry_sc[ch, 3]
        y = ys[:CHUNK] + ys[CHUNK:]
        mean = _pair_sum(y, first) * (1.0 / HEAD)
        d = y - mean
        var = _pair_sum(d * d, first) * (1.0 / HEAD)
        y_ref[0, 0, ch * CHUNK:(ch + 1) * CHUNK, :] = d * lax.rsqrt(var + RWKV_LN_EPS) * lng + lnb + bv_sc[ch]
        return s

    @pl.when(i == 0)
    def _():
        s_sc[...] = jnp.zeros_like(s_sc)
        for stage in prepare_stages():
            stage()

    @pl.when((i > 0) & (i < nblk))
    def _():
        s = s_sc[...]
        stages = prepare_stages()
        slots = len(stages) - 1
        for n_stage, stage in enumerate(stages):
            stage()
            for ch in chs:
                if ch * slots // nc == n_stage:
                    s = advance(ch, s)
        s_sc[...] = s

    @pl.when(i == nblk)
    def _():
        s = s_sc[...]
        for ch in chs:
            s = advance(ch, s)
        s_sc[...] = s


def _wkv_scan(r, lw, k, v, kk, a, r_k, ln_g, ln_b, *, lb=1024):
    batch, npair, t, _ = r.shape
    lb = min(lb, t)
    nblk = t // lb
    nc = lb // CHUNK
    in_spec = pl.BlockSpec((1, 1, lb, PAIR), lambda b, p, i: (b, p, jnp.minimum(i, nblk - 1), 0))
    out_spec = pl.BlockSpec((1, 1, lb, PAIR), lambda b, p, i: (b, p, jnp.maximum(i - 1, 0), 0))
    par_spec = pl.BlockSpec((1, 1, PAIR), lambda b, p, i: (p, 0, 0))
    par = lambda x: x.reshape(npair, 1, PAIR)
    return pl.pallas_call(
        functools.partial(_wkv_kernel, nblk=nblk),
        grid=(batch, npair, nblk + 1),
        in_specs=[in_spec] * 6 + [par_spec] * 3,
        out_specs=out_spec,
        out_shape=jax.ShapeDtypeStruct(r.shape, F32),
        scratch_shapes=[pltpu.VMEM((PAIR, PAIR), F32), pltpu.VMEM((nc, 4, PAIR, PAIR), F32),
                        pltpu.VMEM((nc, CHUNK, PAIR), F32)],
        compiler_params=_cparams(("parallel", "parallel", "arbitrary")),
        name="wkv_scan",
    )(r, lw, k, v, kk, a, par(r_k), par(ln_g), par(ln_b))


def _sublane_all(op, x):
    shift = SUBLANES // 2
    while shift:
        x = op(x, pltpu.roll(x, shift, 0))
        shift //= 2
    return x


def _diff_attn_kernel(q_ref, k_ref, vt_ref, lam_ref, sg_ref, o_ref, m_sc, acc_sc, sa_sc, sb_sc, qs_sc, *,
                      lam_init, cg, n_heads, unroll):
    bq = vt_ref.shape[4]
    dv = LANES
    rows = vt_ref.shape[3]
    ncg = 2 * bq // cg
    nq = q_ref.shape[2] // bq
    head = (pl.program_id(1) + 1).astype(F32)
    block_bias = jnp.exp2(jnp.full((SUBLANES, cg), -8.0 / n_heads, F32) * head) * (LOG2E * bq)
    groups = range(ncg)

    def stack_queries(i):
        q = q_ref[0, 0, pl.ds(pl.multiple_of(i * bq, bq), bq), :]
        first = _half_mask((bq, LANES))
        lane = lax.broadcasted_iota(jnp.int32, (bq, LANES), 1)
        feat = jnp.zeros((bq, LANES), F32)
        for n, part in enumerate(LOG2E_PARTS):
            feat = jnp.where(lane // 2 == n, part, feat)
        feat = feat.astype(BF16)
        zero = jnp.zeros_like(q)
        qs_sc[:bq] = jnp.concatenate([jnp.where(first, q, zero), feat], axis=1)
        qs_sc[bq:] = jnp.concatenate([jnp.where(first, zero, q), feat], axis=1)

    def scores(j, buf):
        kt = k_ref[0, 0, pl.ds(pl.multiple_of(j * bq, bq), bq), :]
        for g in groups:
            buf[g] = lax.dot_general(kt, qs_sc[g * cg:(g + 1) * cg], (((1,), (1,)), ((), ())),
                                     preferred_element_type=F32)

    def q_block(i, carry):
        m_sc[...] = jnp.full_like(m_sc, NEG_BIG)
        acc_sc[...] = jnp.zeros_like(acc_sc)

        def tile(j, buf, diag):
            q0 = [(g * cg) % bq for g in groups]
            nk = [min(bq, q0[g] + cg) if diag else bq for g in groups]
            offset = 0.0 if diag else block_bias * lax.convert_element_type(j - i, F32)
            p, alpha = [], []
            for g in groups:
                sg = buf[g, :nk[g], :]
                if diag:
                    key = lax.broadcasted_iota(jnp.int32, sg.shape, 0)
                    qry = lax.broadcasted_iota(jnp.int32, sg.shape, 1) + q0[g]
                    sg = jnp.where(key <= qry, sg, NEG_BIG)
                sg = sg.reshape(nk[g] // SUBLANES, SUBLANES, cg)
                m_prev = m_sc[g]
                m_new = jnp.maximum(m_prev, _sublane_all(jnp.maximum, jnp.max(sg, axis=0)) + offset)
                p.append(jnp.exp2(sg - (m_new - offset)[None]).reshape(nk[g], cg).astype(BF16))
                alpha.append(jnp.exp2(m_prev - m_new))
                m_sc[g] = m_new
            pv = [jnp.dot(vt_ref[0, 0, j, :, :nk[g]], p[g], preferred_element_type=F32) for g in groups]
            for g in groups:
                acc_sc[g] = (alpha[g][None] * acc_sc[g].reshape(rows // SUBLANES, SUBLANES, cg)
                             + pv[g].reshape(rows // SUBLANES, SUBLANES, cg)).reshape(rows, cg)

        bufs = (sa_sc, sb_sc)

        def run(first_block, count, last_is_diag):
            for n in range(count):
                last = n == count - 1
                if not (last and last_is_diag):
                    scores(first_block + n + 1, bufs[(n + 1) % 2])
                tile(first_block + n, bufs[n % 2], last and last_is_diag)

        def body(jj, carry):
            run(unroll * jj, unroll, False)
            return carry

        lax.fori_loop(0, i // unroll, body, 0)
        for rem in range(unroll):
            @pl.when(i % unroll == rem)
            def _():
                run(i - rem, rem + 1, True)

        stack_queries(jnp.minimum(i + 1, nq - 1))
        scores(0, sa_sc)

        o_t = []
        for g in groups:
            l = acc_sc[g, dv:dv + SUBLANES, :]
            o_t.append((acc_sc[g, :dv, :].reshape(dv // SUBLANES, SUBLANES, cg) / l[None]).reshape(dv, cg))
        half = ncg // 2
        o1 = jnp.concatenate(o_t[:half], axis=1)
        o2 = jnp.concatenate(o_t[half:], axis=1)
        lam = lam_ref[...]
        lam_full = (jnp.exp(jnp.sum(lam[0:1] * lam[1:2], axis=-1, keepdims=True))
                    - jnp.exp(jnp.sum(lam[2:3] * lam[3:4], axis=-1, keepdims=True)) + lam_init)
        o = (o1 - lam_full * o2).T
        o_ref[0, 0, pl.ds(pl.multiple_of(i * bq, bq), bq), :] = (_rms(o, sg_ref[...], SUBLN_EPS)
                                                                 * (1.0 - lam_init)).astype(BF16)
        return carry

    stack_queries(0)
    scores(0, sa_sc)
    lax.fori_loop(0, nq, q_block, 0)


def _diff_attn(q, k, vt, lam, subln, lam_init, *, unroll=8):
    assert unroll % 2 == 0
    batch, nh, t, _ = q.shape
    rows, bq = vt.shape[-2:]
    cg = min(MXU_WIDTH, bq)
    ncg = 2 * bq // cg
    return pl.pallas_call(
        functools.partial(_diff_attn_kernel, lam_init=lam_init, cg=cg, n_heads=nh, unroll=unroll),
        grid=(batch, nh),
        in_specs=[pl.BlockSpec((1, 1, t, LANES), lambda b, h: (b, h, 0, 0)),
                  pl.BlockSpec((1, 1, t, 2 * LANES), lambda b, h: (b, h, 0, 0)),
                  pl.BlockSpec((1, 1, t // bq, rows, bq), lambda b, h: (b, h, 0, 0, 0)),
                  _const_spec(lam.shape), _const_spec((1, LANES))],
        out_specs=pl.BlockSpec((1, 1, t, LANES), lambda b, h: (b, h, 0, 0)),
        out_shape=jax.ShapeDtypeStruct((batch, nh, t, LANES), BF16),
        scratch_shapes=[pltpu.VMEM((ncg, SUBLANES, cg), F32), pltpu.VMEM((ncg, rows, cg), F32),
                        pltpu.VMEM((ncg, bq, cg), F32), pltpu.VMEM((ncg, bq, cg), F32),
                        pltpu.VMEM((2 * bq, 2 * LANES), BF16)],
        compiler_params=_cparams(("parallel", "parallel")),
        name="diff_attn",
    )(q, k, vt, lam, subln.reshape(1, LANES))


def kernel(x, ffn_norm, ffn_w_in, ffn_w_out, mix_norm, rwkv_mu, rwkv_w_rkv, rwkv_w0, rwkv_w1, rwkv_w2, rwkv_a0, rwkv_a1, rwkv_a2, rwkv_g1, rwkv_g2, rwkv_k_k, rwkv_k_a, rwkv_r_k, rwkv_ln_g, rwkv_ln_b, rwkv_w_o, kv_norm, w_kv, k_norm, diff_w_q, diff_q_norm, diff_lambda, diff_subln, diff_w_o):
    batch, t, c = x.shape
    x = x.reshape(batch * t, c)

    x, = _block(x, batch, ffn_norm[0, 0], ffn_w_in[0, 0], ffn_w_out[0, 0])
    r, lw, k, v, kk, a, g = _rwkv_pre(x, batch, mix_norm[0], rwkv_mu[0], rwkv_w_rkv[0], rwkv_w0[0], rwkv_w1[0],
                                      rwkv_w2[0], rwkv_a0[0], rwkv_a1[0], rwkv_a2[0], rwkv_g1[0], rwkv_g2[0],
                                      rwkv_k_k[0], rwkv_k_a[0])
    y = _wkv_scan(r, lw, k, v, kk, a, rwkv_r_k[0], rwkv_ln_g[0], rwkv_ln_b[0])
    x, k_sh, v_sh = _block(x, batch, ffn_norm[0, 1], ffn_w_in[0, 1], ffn_w_out[0, 1],
                           pre=(y, g, rwkv_w_o[0]), post=("kv", kv_norm, w_kv, k_norm))

    x, q = _block(x, batch, ffn_norm[1, 0], ffn_w_in[1, 0], ffn_w_out[1, 0],
                  post=("q", mix_norm[1], diff_w_q[0], diff_q_norm[0]))
    lam_init = 0.8 - 0.6 * math.exp(-0.3 * 1)
    o = _diff_attn(q, k_sh, v_sh, diff_lambda[0], diff_subln[0], lam_init)
    x, = _block(x, batch, ffn_norm[1, 1], ffn_w_in[1, 1], ffn_w_out[1, 1], pre=(o, None, diff_w_o[0]))
    return x.reshape(batch, t, c)
```

```python
import functools
import math

import jax
import jax.numpy as jnp
from jax import lax
from jax.experimental import pallas as pl
from jax.experimental.pallas import tpu as pltpu

F32 = jnp.float32
BF16 = jnp.bfloat16

LANES = 128
SUBLANES = 8
MXU_WIDTH = 256
HEAD = 64
PAIR = 2 * HEAD
CHUNK = 64
ATT = 512
FFN_TF = MXU_WIDTH
NORM_EPS = 1e-6
SUBLN_EPS = 1e-5
RWKV_LN_EPS = 64e-5
VMEM_LIMIT = 56 * 1024 * 1024
NEG_BIG = -1e30
BF16_ROWS = 16
VT_ROWS = LANES + BF16_ROWS


def _bf16_parts(x, n):
    parts = []
    for _ in range(n):
        m, e = math.frexp(x)
        p = math.ldexp(round(m * 256.0) / 256.0, e)
        parts.append(p)
        x -= p
    return tuple(parts)


LOG2E = math.log2(math.e)
LOG2E_PARTS = _bf16_parts(LOG2E, 3)


def _cparams(sem):
    return pltpu.CompilerParams(dimension_semantics=sem, vmem_limit_bytes=VMEM_LIMIT)


def _const_spec(shape):
    nd = len(shape)
    return pl.BlockSpec(shape, lambda *_: (0,) * nd, pipeline_mode=pl.Buffered(1))


def _mm(a, b):
    return jnp.dot(a.astype(BF16), b.astype(BF16), preferred_element_type=F32)


def _mm_nt(a, b):
    return lax.dot_general(a.astype(BF16), b.astype(BF16), (((1,), (1,)), ((), ())),
                           preferred_element_type=F32)


def _mm_tn(a, b):
    return lax.dot_general(a.astype(BF16), b.astype(BF16), (((0,), (0,)), ((), ())),
                           preferred_element_type=F32)


def _rms(x, g, eps):
    return x * lax.rsqrt(jnp.mean(x * x, axis=-1, keepdims=True) + eps) * g


def _half_mask(shape):
    return lax.broadcasted_iota(jnp.int32, shape, len(shape) - 1) % PAIR < HEAD


def _pair_sum(x, first):
    s1 = jnp.sum(jnp.where(first, x, 0.0), axis=-1, keepdims=True)
    s2 = jnp.sum(jnp.where(first, 0.0, x), axis=-1, keepdims=True)
    return jnp.where(first, s1, s2)


def _ffn_value(x, g_ref, win_ref, wout_ref, h_sc):
    d_ff = wout_ref.shape[0]
    xn = _rms(x, g_ref[...], NORM_EPS).astype(BF16)
    for f0 in range(0, d_ff, FFN_TF):
        gate = jnp.dot(xn, win_ref[:, f0:f0 + FFN_TF], preferred_element_type=F32)
        up = jnp.dot(xn, win_ref[:, d_ff + f0:d_ff + f0 + FFN_TF], preferred_element_type=F32)
        h_sc[:, f0:f0 + FFN_TF] = (gate * jax.nn.sigmoid(gate) * up).astype(BF16)
    return x + 0.5 * jnp.dot(h_sc[...], wout_ref[...], preferred_element_type=F32)


def _keys_values(x, ng_ref, w_ref, kg_ref, k_out, v_out):
    tm, c = x.shape
    n_heads = c // LANES
    kv = _mm(_rms(x, ng_ref[...], NORM_EPS), w_ref[...])
    first = _half_mask((tm, LANES))
    kg = kg_ref[...]
    lane = lax.broadcasted_iota(jnp.int32, (tm, LANES), 1)
    pos = lax.broadcasted_iota(jnp.int32, (tm, LANES), 0)
    base = jnp.where(lane >= 2 * len(LOG2E_PARTS), 0.0,
                     jnp.where(lane % 2 == 0, (pos // HEAD * HEAD).astype(F32), (pos % HEAD).astype(F32)))
    ones = jnp.ones((VT_ROWS - LANES, tm), BF16)
    for h in range(n_heads):
        kh = kv[:, h * LANES:(h + 1) * LANES]
        ms = _pair_sum(kh * kh, first) * (1.0 / HEAD)
        kn = kh * lax.rsqrt(ms + NORM_EPS) * kg
        slope = 2.0 ** (-8.0 * (h + 1) / n_heads)
        k_out[0, h] = jnp.concatenate([kn.astype(BF16), (base * slope).astype(BF16)], axis=-1)
        vt = kv[:, c + h * LANES:c + (h + 1) * LANES].T.astype(BF16)
        v_out[0, h, 0] = jnp.concatenate([vt, ones], axis=0)


def _queries(x, ng_ref, w_ref, qg_ref, q_out):
    tm, c = x.shape
    q = _mm(_rms(x, ng_ref[...], NORM_EPS), w_ref[...])
    first = _half_mask((tm, LANES))
    qg = qg_ref[...] * (HEAD ** -0.5 * LOG2E)
    for h in range(c // LANES):
        qh = q[:, h * LANES:(h + 1) * LANES]
        ms = _pair_sum(qh * qh, first) * (1.0 / HEAD)
        q_out[0, h] = (qh * lax.rsqrt(ms + NORM_EPS) * qg).astype(BF16)


def _block_kernel(*refs, pre, post):
    refs = list(refs)
    x = refs.pop(0)[...]
    if pre is not None:
        y_ref = refs.pop(0)
        y = jnp.concatenate([y_ref[0, p] for p in range(y_ref.shape[1])], axis=-1)
        if pre == "gated":
            y = y * refs.pop(0)[...]
        x = x + jnp.dot(y.astype(BF16), refs.pop(0)[...], preferred_element_type=F32)
    g_ref, win_ref, wout_ref = refs[:3]
    refs = refs[3:]
    post_refs = [refs.pop(0) for _ in range(3)] if post is not None else []
    o_ref = refs.pop(0)
    h_sc = refs.pop()
    x = _ffn_value(x, g_ref, win_ref, wout_ref, h_sc)
    o_ref[...] = x
    if post == "kv":
        _keys_values(x, *post_refs, *refs)
    elif post == "q":
        _queries(x, *post_refs, *refs)


def _block(x, batch, g, w_in, w_out, *, pre=None, post=None):
    m, c = x.shape
    t = m // batch
    d_ff = w_out.shape[0]
    tm = min(ATT, t)
    bps = t // tm
    nh = c // LANES
    row_spec = pl.BlockSpec((tm, c), lambda i: (i, 0))
    head_idx = lambda i: (i // bps, 0, i % bps, 0)
    args, in_specs = [x], [row_spec]
    pre_kind = None
    if pre is not None:
        y, gate, w_o = pre
        pre_kind = "plain" if gate is None else "gated"
        args.append(y)
        in_specs.append(pl.BlockSpec((1, y.shape[1], tm, LANES), head_idx))
        if gate is not None:
            args.append(gate)
            in_specs.append(row_spec)
        args.append(w_o.astype(BF16))
        in_specs.append(_const_spec((c, c)))
    args += [g.reshape(1, c), w_in.astype(BF16), w_out.astype(BF16)]
    in_specs += [_const_spec((1, c)), _const_spec((c, 2 * d_ff)), _const_spec((d_ff, c))]
    out_specs, out_shape = [row_spec], [jax.ShapeDtypeStruct((m, c), F32)]
    post_kind = None
    if post is not None:
        post_kind, ng, w, head_norm = post
        args += [ng.reshape(1, c), w.astype(BF16), jnp.concatenate([head_norm, head_norm]).reshape(1, LANES)]
        in_specs += [_const_spec((1, c)), _const_spec(w.shape), _const_spec((1, LANES))]
        if post_kind == "kv":
            out_specs += [pl.BlockSpec((1, nh, tm, 2 * LANES), head_idx),
                          pl.BlockSpec((1, nh, 1, VT_ROWS, tm), lambda i: (i // bps, 0, i % bps, 0, 0))]
            out_shape += [jax.ShapeDtypeStruct((batch, nh, t, 2 * LANES), BF16),
                          jax.ShapeDtypeStruct((batch, nh, t // tm, VT_ROWS, tm), BF16)]
        else:
            out_specs.append(pl.BlockSpec((1, nh, tm, LANES), head_idx))
            out_shape.append(jax.ShapeDtypeStruct((batch, nh, t, LANES), BF16))
    return pl.pallas_call(
        functools.partial(_block_kernel, pre=pre_kind, post=post_kind),
        grid=(m // tm,),
        in_specs=in_specs,
        out_specs=out_specs,
        out_shape=out_shape,
        scratch_shapes=[pltpu.VMEM((tm, d_ff), BF16)],
        compiler_params=_cparams(("parallel",)),
        name="block_" + (pre_kind or "x") + "_" + (post_kind or "x"),
    )(*args)


def _rwkv_pre_kernel(x_ref, xp_ref, ng_ref, mu_ref, wrkv_ref, w0_ref, w1_ref, w2_ref, a0_ref, a1_ref,
                     a2_ref, g1_ref, g2_ref, kk_ref, ka_ref,
                     r_out, lw_out, k_out, v_out, kk_out, a_out, g_out, *, blocks_per_seq):
    tm, c = x_ref.shape
    ng = ng_ref[...]
    h = _rms(x_ref[...], ng, NORM_EPS)
    prev = _rms(xp_ref[SUBLANES - 1:SUBLANES, :], ng, NORM_EPS)
    prev = jnp.where(pl.program_id(0) % blocks_per_seq == 0, 0.0, prev)
    row = lax.broadcasted_iota(jnp.int32, (tm, c), 0)
    h_prev = jnp.where(row == 0, prev, pltpu.roll(h, 1, 0))
    dx = h_prev - h

    def mix(i):
        return h + dx * mu_ref[i:i + 1, :]

    r = _mm(mix(0), wrkv_ref[0])
    k = _mm(mix(1), wrkv_ref[1])
    v = _mm(mix(2), wrkv_ref[2])
    z = w0_ref[...] + _mm(jnp.tanh(_mm(mix(3), w1_ref[...])), w2_ref[...])
    lw = -math.exp(-0.5) * jax.nn.sigmoid(z)
    a = jax.nn.sigmoid(a0_ref[...] + _mm(_mm(mix(4), a1_ref[...]), a2_ref[...]))
    g_out[...] = _mm(jax.nn.sigmoid(_mm(mix(5), g1_ref[...])), g2_ref[...])
    kk = k * kk_ref[...]
    k = k * (1.0 + (a - 1.0) * ka_ref[...])
    first = _half_mask((tm, PAIR))
    for p in range(c // PAIR):
        sl = slice(p * PAIR, (p + 1) * PAIR)
        kkp = kk[:, sl]
        ss = _pair_sum(kkp * kkp, first)
        r_out[0, p] = r[:, sl]
        lw_out[0, p] = lw[:, sl]
        k_out[0, p] = k[:, sl]
        v_out[0, p] = v[:, sl]
        kk_out[0, p] = kkp * lax.rsqrt(jnp.maximum(ss, 1e-24))
        a_out[0, p] = a[:, sl]


def _rwkv_pre(x, batch, ng, mu, w_rkv, w0, w1, w2, a0, a1, a2, g1, g2, k_k, k_a, *, tm=512):
    m, c = x.shape
    t = m // batch
    tm = min(tm, t)
    bps = t // tm
    npair = c // PAIR
    row = lambda a: a.reshape(1, c)
    pair_spec = pl.BlockSpec((1, npair, tm, PAIR), lambda i: (i // bps, 0, i % bps, 0))
    pair_shape = jax.ShapeDtypeStruct((batch, npair, t, PAIR), F32)
    ws = [w_rkv.astype(BF16), row(w0), w1.astype(BF16), w2.astype(BF16), row(a0), a1.astype(BF16),
          a2.astype(BF16), g1.astype(BF16), g2.astype(BF16), row(k_k), row(k_a)]
    return pl.pallas_call(
        functools.partial(_rwkv_pre_kernel, blocks_per_seq=bps),
        grid=(m // tm,),
        in_specs=[pl.BlockSpec((tm, c), lambda i: (i, 0)),
                  pl.BlockSpec((SUBLANES, c), lambda i: (jnp.maximum(i * (tm // SUBLANES) - 1, 0), 0)),
                  _const_spec((1, c)), _const_spec(mu.shape)] + [_const_spec(w.shape) for w in ws],
        out_specs=[pair_spec] * 6 + [pl.BlockSpec((tm, c), lambda i: (i, 0))],
        out_shape=[pair_shape] * 6 + [jax.ShapeDtypeStruct((m, c), F32)],
        compiler_params=_cparams(("parallel",)),
        name="rwkv_pre",
    )(x, x, row(ng), mu, *ws)


def _stack(x, first):
    return jnp.concatenate([jnp.where(first, x, 0.0), jnp.where(first, 0.0, x)], axis=0)


def _wkv_kernel(r_ref, lw_ref, k_ref, v_ref, kk_ref, a_ref, rk_ref, lng_ref, lnb_ref, y_ref,
                s_sc, carry_sc, bv_sc, *, nblk):
    lb = r_ref.shape[2]
    nc = lb // CHUNK
    l2 = 2 * CHUNK
    i = pl.program_id(2)

    first = _half_mask((CHUNK, PAIR))
    ri = lax.broadcasted_iota(jnp.int32, (l2, l2), 0)
    ci = lax.broadcasted_iota(jnp.int32, (l2, l2), 1)
    same = (ri // CHUNK) == (ci // CHUNK)
    strict = same & (ri % CHUNK > ci % CHUNK)
    incl = same & (ri % CHUNK >= ci % CHUNK)
    eye = ri == ci
    ident = jnp.where(eye, 1.0, 0.0)
    rk = rk_ref[0]
    lng = lng_ref[0]
    lnb = lnb_ref[0]
    chs = range(nc)

    def prepare_stages():
        w = {}

        def prep():
            tri = jnp.where(lax.broadcasted_iota(jnp.int32, (CHUNK, CHUNK), 1)
                            <= lax.broadcasted_iota(jnp.int32, (CHUNK, CHUNK), 0), 1.0, 0.0).astype(BF16)
            lw_wide = jnp.concatenate([lw_ref[0, 0, ch * CHUNK:(ch + 1) * CHUNK, :] for ch in chs], axis=1)
            lw_hi = lw_wide.astype(BF16)
            lw_lo = (lw_wide - lw_hi.astype(F32)).astype(BF16)
            c_wide = (jnp.dot(tri, lw_hi, preferred_element_type=F32)
                      + jnp.dot(tri, lw_lo, preferred_element_type=F32))
            for ch in chs:
                sl = slice(ch * CHUNK, (ch + 1) * CHUNK)
                r = r_ref[0, 0, sl, :]
                k = k_ref[0, 0, sl, :]
                v = v_ref[0, 0, sl, :]
                kk = kk_ref[0, 0, sl, :]
                c = c_wide[:, ch * PAIR:(ch + 1) * PAIR]
                c_last = c[CHUNK - 1:CHUNK, :]
                e_neg = jnp.exp(-c)
                e_last = jnp.exp(c_last - c)
                b = kk * a_ref[0, 0, sl, :]
                rt = _stack(r * jnp.exp(c), first)
                at = _stack(-kk * jnp.exp(c - lw_ref[0, 0, sl, :]), first)
                w[ch] = dict(
                    rt=rt, at=at, vs=_stack(v, first), g_last=jnp.exp(c_last),
                    bonus=_pair_sum(r * k * rk, first) * v,
                    ar=jnp.concatenate([at, rt], axis=0),
                    bkt=jnp.concatenate([_stack(b * e_neg, first), _stack(k * e_neg, first)], axis=0),
                    bk_last=jnp.concatenate([_stack(b * e_last, first), _stack(k * e_last, first)], axis=0))

        def couplings():
            for ch in chs:
                d = w[ch]
                g = _mm_nt(d["ar"], d["bkt"])
                d["n"] = jnp.where(strict, g[:l2, :l2], 0.0)
                d["a_ak"] = jnp.where(strict, g[:l2, l2:], 0.0)
                d["a_rbk"] = jnp.concatenate([jnp.where(incl, g[l2:, :l2], 0.0),
                                              jnp.where(incl, g[l2:, l2:], 0.0)], axis=1)

        def start_inverse():
            for ch in chs:
                d = w[ch]
                d["aakv"] = _mm(d["a_ak"], d["vs"])
                d["tinv"] = ident + d["n"]
                d["pw"] = d["n"]

        def square():
            for ch in chs:
                w[ch]["pw"] = _mm(w[ch]["pw"], w[ch]["pw"])

        def extend():
            for ch in chs:
                w[ch]["tinv"] = w[ch]["tinv"] + _mm(w[ch]["tinv"], w[ch]["pw"])

        def solve():
            for ch in chs:
                d = w[ch]
                x = _mm(d["tinv"], jnp.concatenate([d["at"], d["aakv"]], axis=1))
                d["z"] = jnp.concatenate([x, jnp.concatenate([jnp.zeros_like(d["vs"]), d["vs"]], axis=1)],
                                         axis=0)

        def finish():
            for ch in chs:
                d = w[ch]
                m1 = _mm(d["a_rbk"], d["z"])
                m2 = _mm_tn(d["z"], d["bk_last"])
                carry_sc[ch, 0] = d["rt"] + m1[:, :PAIR]
                carry_sc[ch, 1] = m1[:, PAIR:]
                carry_sc[ch, 2] = jnp.where(eye, d["g_last"], 0.0) + m2[:PAIR]
                carry_sc[ch, 3] = m2[PAIR:]
                bv_sc[ch] = d["bonus"]

        rounds = int(math.log2(CHUNK)) - 1
        return [prep, couplings, start_inverse] + [square, extend] * rounds + [solve, finish]

    def advance(ch, s):
        ys = _mm_nt(carry_sc[ch, 0], s) + carry_sc[ch, 1]
        s = _mm(s, carry_sc[ch, 2]) + carry_sc[ch, 3]
        y = ys[:CHUNK] + ys[CHUNK:]
        mean = _pair_sum(y, first) * (1.0 / HEAD)
        d = y - mean
        var = _pair_sum(d * d, first) * (1.0 / HEAD)
        y_ref[0, 0, ch * CHUNK:(ch + 1) * CHUNK, :] = d * lax.rsqrt(var + RWKV_LN_EPS) * lng + lnb + bv_sc[ch]
        return s

    @pl.when(i == 0)
    def _():
        s_sc[...] = jnp.zeros_like(s_sc)
        for stage in prepare_stages():
            stage()

    @pl.when((i > 0) & (i < nblk))
    def _():
        s = s_sc[...]
        stages = prepare_stages()
        slots = len(stages) - 1
        for n_stage, stage in enumerate(stages):
            stage()
            for ch in chs:
                if ch * slots // nc == n_stage:
                    s = advance(ch, s)
        s_sc[...] = s

    @pl.when(i == nblk)
    def _():
        s = s_sc[...]
        for ch in chs:
            s = advance(ch, s)
        s_sc[...] = s


def _wkv_scan(r, lw, k, v, kk, a, r_k, ln_g, ln_b, *, lb=1024):
    batch, npair, t, _ = r.shape
    lb = min(lb, t)
    nblk = t // lb
    nc = lb // CHUNK
    in_spec = pl.BlockSpec((1, 1, lb, PAIR), lambda b, p, i: (b, p, jnp.minimum(i, nblk - 1), 0))
    out_spec = pl.BlockSpec((1, 1, lb, PAIR), lambda b, p, i: (b, p, jnp.maximum(i - 1, 0), 0))
    par_spec = pl.BlockSpec((1, 1, PAIR), lambda b, p, i: (p, 0, 0))
    par = lambda x: x.reshape(npair, 1, PAIR)
    return pl.pallas_call(
        functools.partial(_wkv_kernel, nblk=nblk),
        grid=(batch, npair, nblk + 1),
        in_specs=[in_spec] * 6 + [par_spec] * 3,
        out_specs=out_spec,
        out_shape=jax.ShapeDtypeStruct(r.shape, F32),
        scratch_shapes=[pltpu.VMEM((PAIR, PAIR), F32), pltpu.VMEM((nc, 4, PAIR, PAIR), F32),
                        pltpu.VMEM((nc, CHUNK, PAIR), F32)],
        compiler_params=_cparams(("parallel", "parallel", "arbitrary")),
        name="wkv_scan",
    )(r, lw, k, v, kk, a, par(r_k), par(ln_g), par(ln_b))


def _sublane_all(op, x):
    shift = SUBLANES // 2
    while shift:
        x = op(x, pltpu.roll(x, shift, 0))
        shift //= 2
    return x


def _loop_attn_kernel(q_ref, k_ref, vt_ref, lam_ref, sg_ref, o_ref, m_sc, acc_sc, sa_sc, sb_sc, qs_sc, *,
                      lam_init, cg, n_heads, unroll):
    bq = vt_ref.shape[4]
    dv = LANES
    rows = vt_ref.shape[3]
    ncg = 2 * bq // cg
    nq = q_ref.shape[2] // bq
    head = (pl.program_id(1) + 1).astype(F32)
    block_bias = jnp.exp2(jnp.full((SUBLANES, cg), -8.0 / n_heads, F32) * head) * (LOG2E * bq)
    groups = range(ncg)

    def stack_queries(i):
        q = q_ref[0, 0, pl.ds(pl.multiple_of(i * bq, bq), bq), :]
        first = _half_mask((bq, LANES))
        lane = lax.broadcasted_iota(jnp.int32, (bq, LANES), 1)
        feat = jnp.zeros((bq, LANES), F32)
        for n, part in enumerate(LOG2E_PARTS):
            feat = jnp.where(lane // 2 == n, part, feat)
        feat = feat.astype(BF16)
        zero = jnp.zeros_like(q)
        qs_sc[:bq] = jnp.concatenate([jnp.where(first, q, zero), feat], axis=1)
        qs_sc[bq:] = jnp.concatenate([jnp.where(first, zero, q), feat], axis=1)

    def scores(j, buf):
        kt = k_ref[0, 0, pl.ds(pl.multiple_of(j * bq, bq), bq), :]
        for g in groups:
            buf[g] = lax.dot_general(kt, qs_sc[g * cg:(g + 1) * cg], (((1,), (1,)), ((), ())),
                                     preferred_element_type=F32)

    def q_block(i, carry):
        m_sc[...] = jnp.full_like(m_sc, NEG_BIG)
        acc_sc[...] = jnp.zeros_like(acc_sc)

        def tile(j, buf, diag):
            q0 = [(g * cg) % bq for g in groups]
            nk = [min(bq, q0[g] + cg) if diag else bq for g in groups]
            offset = 0.0 if diag else block_bias * lax.convert_element_type(j - i, F32)
            p, alpha = [], []
            for g in groups:
                sg = buf[g, :nk[g], :]
                if diag:
                    key = lax.broadcasted_iota(jnp.int32, sg.shape, 0)
                    qry = lax.broadcasted_iota(jnp.int32, sg.shape, 1) + q0[g]
                    sg = jnp.where(key <= qry, sg, NEG_BIG)
                sg = sg.reshape(nk[g] // SUBLANES, SUBLANES, cg)
                m_prev = m_sc[g]
                m_new = jnp.maximum(m_prev, _sublane_all(jnp.maximum, jnp.max(sg, axis=0)) + offset)
                p.append(jnp.exp2(sg - (m_new - offset)[None]).reshape(nk[g], cg).astype(BF16))
                alpha.append(jnp.exp2(m_prev - m_new))
                m_sc[g] = m_new
            pv = [jnp.dot(vt_ref[0, 0, j, :, :nk[g]], p[g], preferred_element_type=F32) for g in groups]
            for g in groups:
                acc_sc[g] = (alpha[g][None] * acc_sc[g].reshape(rows // SUBLANES, SUBLANES, cg)
                             + pv[g].reshape(rows // SUBLANES, SUBLANES, cg)).reshape(rows, cg)

        bufs = (sa_sc, sb_sc)

        def run(first_block, count, last_is_diag):
            for n in range(count):
                last = n == count - 1
                if not (last and last_is_diag):
                    scores(first_block + n + 1, bufs[(n + 1) % 2])
                tile(first_block + n, bufs[n % 2], last and last_is_diag)

        def body(jj, carry):
            run(unroll * jj, unroll, False)
            return carry

        lax.fori_loop(0, i // unroll, body, 0)
        for rem in range(unroll):
            @pl.when(i % unroll == rem)
            def _():
                run(i - rem, rem + 1, True)

        stack_queries(jnp.minimum(i + 1, nq - 1))
        scores(0, sa_sc)

        o_t = []
        for g in groups:
            l = acc_sc[g, dv:dv + SUBLANES, :]
            o_t.append((acc_sc[g, :dv, :].reshape(dv // SUBLANES, SUBLANES, cg) / l[None]).reshape(dv, cg))
        half = ncg // 2
        o1 = jnp.concatenate(o_t[:half], axis=1)
        o2 = jnp.concatenate(o_t[half:], axis=1)
        lam = lam_ref[...]
        lam_full = (jnp.exp(jnp.sum(lam[0:1] * lam[1:2], axis=-1, keepdims=True))
                    - jnp.exp(jnp.sum(lam[2:3] * lam[3:4], axis=-1, keepdims=True)) + lam_init)
        o = (o1 - lam_full * o2).T
        o_ref[0, 0, pl.ds(pl.multiple_of(i * bq, bq), bq), :] = (_rms(o, sg_ref[...], SUBLN_EPS)
                                                                 * (1.0 - lam_init)).astype(BF16)
        return carry

    stack_queries(0)
    scores(0, sa_sc)
    lax.fori_loop(0, nq, q_block, 0)


def _loop_attn(q, k, vt, lam, subln, lam_init, *, unroll=8):
    assert unroll % 2 == 0
    batch, nh, t, _ = q.shape
    rows, bq = vt.shape[-2:]
    cg = min(MXU_WIDTH, bq)
    ncg = 2 * bq // cg
    return pl.pallas_call(
        functools.partial(_loop_attn_kernel, lam_init=lam_init, cg=cg, n_heads=nh, unroll=unroll),
        grid=(batch, nh),
        in_specs=[pl.BlockSpec((1, 1, t, LANES), lambda b, h: (b, h, 0, 0)),
                  pl.BlockSpec((1, 1, t, 2 * LANES), lambda b, h: (b, h, 0, 0)),
                  pl.BlockSpec((1, 1, t // bq, rows, bq), lambda b, h: (b, h, 0, 0, 0)),
                  _const_spec(lam.shape), _const_spec((1, LANES))],
        out_specs=pl.BlockSpec((1, 1, t, LANES), lambda b, h: (b, h, 0, 0)),
        out_shape=jax.ShapeDtypeStruct((batch, nh, t, LANES), BF16),
        scratch_shapes=[pltpu.VMEM((ncg, SUBLANES, cg), F32), pltpu.VMEM((ncg, rows, cg), F32),
                        pltpu.VMEM((ncg, bq, cg), F32), pltpu.VMEM((ncg, bq, cg), F32),
                        pltpu.VMEM((2 * bq, 2 * LANES), BF16)],
        compiler_params=_cparams(("parallel", "parallel")),
        name="diff_attn",
    )(q, k, vt, lam, subln.reshape(1, LANES))


def _diff_attn_kernel(q_ref, k_ref, vt_ref, lam_ref, sg_ref, o_ref, m_sc, acc_sc, sa_sc, sb_sc, qs_sc, *,
                      lam_init, cg, n_heads):
    bq = vt_ref.shape[4]
    dv = LANES
    rows = vt_ref.shape[3]
    ncg = 2 * bq // cg
    nq = q_ref.shape[2] // bq
    n_full = nq - 1
    head = (pl.program_id(1) + 1).astype(F32)
    block_bias = jnp.exp2(jnp.full((SUBLANES, cg), -8.0 / n_heads, F32) * head) * (LOG2E * bq)
    groups = range(ncg)
    q0 = [(g * cg) % bq for g in groups]
    bufs = (sa_sc, sb_sc)

    def stack_queries(i, slot):
        q = q_ref[0, 0, pl.ds(pl.multiple_of(i * bq, bq), bq), :]
        first = _half_mask((bq, LANES))
        lane = lax.broadcasted_iota(jnp.int32, (bq, LANES), 1)
        feat = jnp.zeros((bq, LANES), F32)
        for n, part in enumerate(LOG2E_PARTS):
            feat = jnp.where(lane // 2 == n, part, feat)
        feat = feat.astype(BF16)
        zero = jnp.zeros_like(q)
        qs_sc[slot, :bq] = jnp.concatenate([jnp.where(first, q, zero), feat], axis=1)
        qs_sc[slot, bq:] = jnp.concatenate([jnp.where(first, zero, q), feat], axis=1)

    def scores(slot, j, buf, g_off=0):
        kt = k_ref[0, 0, pl.ds(pl.multiple_of(j * bq, bq), bq), :]
        for g in groups:
            buf[g_off + g] = lax.dot_general(kt, qs_sc[slot, g * cg:(g + 1) * cg], (((1,), (1,)), ((), ())),
                                             preferred_element_type=F32)

    def tile(slot, i, j, buf, diag, g_off=0):
        nk = [min(bq, q0[g] + cg) if diag else bq for g in groups]
        offset = 0.0 if diag else block_bias * lax.convert_element_type(j - i, F32)
        p, alpha = [], []
        for g in groups:
            sg = buf[g_off + g, :nk[g], :]
            if diag:
                key = lax.broadcasted_iota(jnp.int32, sg.shape, 0)
                qry = lax.broadcasted_iota(jnp.int32, sg.shape, 1) + q0[g]
                sg = jnp.where(key <= qry, sg, NEG_BIG)
            sg = sg.reshape(nk[g] // SUBLANES, SUBLANES, cg)
            m_prev = m_sc[slot, g]
            m_new = jnp.maximum(m_prev, _sublane_all(jnp.maximum, jnp.max(sg, axis=0)) + offset)
            p.append(jnp.exp2(sg - (m_new - offset)[None]).reshape(nk[g], cg).astype(BF16))
            alpha.append(jnp.exp2(m_prev - m_new))
            m_sc[slot, g] = m_new
        pv = [jnp.dot(vt_ref[0, 0, j, :, :nk[g]], p[g], preferred_element_type=F32) for g in groups]
        for g in groups:
            acc_sc[slot, g] = (alpha[g][None] * acc_sc[slot, g].reshape(rows // SUBLANES, SUBLANES, cg)
                               + pv[g].reshape(rows // SUBLANES, SUBLANES, cg)).reshape(rows, cg)

    def finish(slot, i):
        o_t = []
        for g in groups:
            l = acc_sc[slot, g, dv:dv + SUBLANES, :]
            o_t.append((acc_sc[slot, g, :dv, :].reshape(dv // SUBLANES, SUBLANES, cg) / l[None]).reshape(dv, cg))
        half = ncg // 2
        o1 = jnp.concatenate(o_t[:half], axis=1)
        o2 = jnp.concatenate(o_t[half:], axis=1)
        lam = lam_ref[...]
        lam_full = (jnp.exp(jnp.sum(lam[0:1] * lam[1:2], axis=-1, keepdims=True))
                    - jnp.exp(jnp.sum(lam[2:3] * lam[3:4], axis=-1, keepdims=True)) + lam_init)
        o = (o1 - lam_full * o2).T
        o_ref[0, 0, pl.ds(pl.multiple_of(i * bq, bq), bq), :] = (_rms(o, sg_ref[...], SUBLN_EPS)
                                                                 * (1.0 - lam_init)).astype(BF16)

    def full_tile_ids(p, n):
        slot = (n >= p).astype(jnp.int32)
        return slot, jnp.where(slot == 0, p, nq - 1 - p), n - slot * p

    def pair(p, carry):
        ia, ib = p, nq - 1 - p
        m_sc[...] = jnp.full_like(m_sc, NEG_BIG)
        acc_sc[...] = jnp.zeros_like(acc_sc)
        for n in range(n_full):
            if n + 1 < n_full:
                slot_n, _, key_n = full_tile_ids(p, n + 1)
                scores(slot_n, key_n, bufs[(n + 1) % 2])
            else:
                scores(0, ia, bufs[(n + 1) % 2])
                scores(1, ib, bufs[(n + 1) % 2], ncg)
            slot, i, key = full_tile_ids(p, n)
            tile(slot, i, key, bufs[n % 2], False)
        nxt = jnp.minimum(p + 1, nq // 2 - 1)
        stack_queries(nxt, 0)
        stack_queries(nq - 1 - nxt, 1)
        slot_n, _, key_n = full_tile_ids(nxt, 0)
        scores(slot_n, key_n, bufs[0])
        tile(0, ia, ia, bufs[n_full % 2], True)
        tile(1, ib, ib, bufs[n_full % 2], True, ncg)
        finish(0, ia)
        finish(1, ib)
        return carry

    stack_queries(0, 0)
    stack_queries(nq - 1, 1)
    slot_0, _, key_0 = full_tile_ids(jnp.int32(0), 0)
    scores(slot_0, key_0, bufs[0])
    lax.fori_loop(0, nq // 2, pair, 0)


def _diff_attn(q, k, vt, lam, subln, lam_init):
    batch, nh, t, _ = q.shape
    rows, bq = vt.shape[-2:]
    cg = min(MXU_WIDTH, bq)
    ncg = 2 * bq // cg
    assert (t // bq) % 2 == 0
    return pl.pallas_call(
        functools.partial(_diff_attn_kernel, lam_init=lam_init, cg=cg, n_heads=nh),
        grid=(batch, nh),
        in_specs=[pl.BlockSpec((1, 1, t, LANES), lambda b, h: (b, h, 0, 0)),
                  pl.BlockSpec((1, 1, t, 2 * LANES), lambda b, h: (b, h, 0, 0)),
                  pl.BlockSpec((1, 1, t // bq, rows, bq), lambda b, h: (b, h, 0, 0, 0)),
                  _const_spec(lam.shape), _const_spec((1, LANES))],
        out_specs=pl.BlockSpec((1, 1, t, LANES), lambda b, h: (b, h, 0, 0)),
        out_shape=jax.ShapeDtypeStruct((batch, nh, t, LANES), BF16),
        scratch_shapes=[pltpu.VMEM((2, ncg, SUBLANES, cg), F32), pltpu.VMEM((2, ncg, rows, cg), F32),
                        pltpu.VMEM((2 * ncg, bq, cg), F32), pltpu.VMEM((2 * ncg, bq, cg), F32),
                        pltpu.VMEM((2, 2 * bq, 2 * LANES), BF16)],
        compiler_params=_cparams(("parallel", "parallel")),
        name="diff_attn",
    )(q, k, vt, lam, subln.reshape(1, LANES))


def kernel(x, ffn_norm, ffn_w_in, ffn_w_out, mix_norm, rwkv_mu, rwkv_w_rkv, rwkv_w0, rwkv_w1, rwkv_w2, rwkv_a0, rwkv_a1, rwkv_a2, rwkv_g1, rwkv_g2, rwkv_k_k, rwkv_k_a, rwkv_r_k, rwkv_ln_g, rwkv_ln_b, rwkv_w_o, kv_norm, w_kv, k_norm, diff_w_q, diff_q_norm, diff_lambda, diff_subln, diff_w_o):
    batch, t, c = x.shape
    x = x.reshape(batch * t, c)

    x, = _block(x, batch, ffn_norm[0, 0], ffn_w_in[0, 0], ffn_w_out[0, 0])
    r, lw, k, v, kk, a, g = _rwkv_pre(x, batch, mix_norm[0], rwkv_mu[0], rwkv_w_rkv[0], rwkv_w0[0], rwkv_w1[0],
                                      rwkv_w2[0], rwkv_a0[0], rwkv_a1[0], rwkv_a2[0], rwkv_g1[0], rwkv_g2[0],
                                      rwkv_k_k[0], rwkv_k_a[0])
    y = _wkv_scan(r, lw, k, v, kk, a, rwkv_r_k[0], rwkv_ln_g[0], rwkv_ln_b[0])
    x, k_sh, v_sh = _block(x, batch, ffn_norm[0, 1], ffn_w_in[0, 1], ffn_w_out[0, 1],
                           pre=(y, g, rwkv_w_o[0]), post=("kv", kv_norm, w_kv, k_norm))

    x, q = _block(x, batch, ffn_norm[1, 0], ffn_w_in[1, 0], ffn_w_out[1, 0],
                  post=("q", mix_norm[1], diff_w_q[0], diff_q_norm[0]))
    lam_init = 0.8 - 0.6 * math.exp(-0.3 * 1)
    o = _diff_attn(q, k_sh, v_sh, diff_lambda[0], diff_subln[0], lam_init)
    x, = _block(x, batch, ffn_norm[1, 1], ffn_w_in[1, 1], ffn_w_out[1, 1], pre=(o, None, diff_w_o[0]))
    return x.reshape(batch, t, c)
```

```python
import functools
import math

import jax
import jax.numpy as jnp
from jax import lax
from jax.experimental import pallas as pl
from jax.experimental.pallas import tpu as pltpu

F32 = jnp.float32
BF16 = jnp.bfloat16

LANES = 128
SUBLANES = 8
MXU_WIDTH = 256
HEAD = 64
PAIR = 2 * HEAD
CHUNK = 64
ATT = 512
FFN_TF = MXU_WIDTH
NORM_EPS = 1e-6
SUBLN_EPS = 1e-5
RWKV_LN_EPS = 64e-5
VMEM_LIMIT = 56 * 1024 * 1024
NEG_BIG = -1e30
BF16_ROWS = 16
VT_ROWS = LANES + BF16_ROWS


def _bf16_parts(x, n):
    parts = []
    for _ in range(n):
        m, e = math.frexp(x)
        p = math.ldexp(round(m * 256.0) / 256.0, e)
        parts.append(p)
        x -= p
    return tuple(parts)


LOG2E = math.log2(math.e)
LOG2E_PARTS = _bf16_parts(LOG2E, 3)


def _cparams(sem):
    return pltpu.CompilerParams(dimension_semantics=sem, vmem_limit_bytes=VMEM_LIMIT)


def _const_spec(shape, lead=()):
    nd = len(shape)
    return pl.BlockSpec((None,) * len(lead) + tuple(shape), lambda *_: tuple(lead) + (0,) * nd,
                        pipeline_mode=pl.Buffered(1))


def _mm(a, b):
    return jnp.dot(a.astype(BF16), b.astype(BF16), preferred_element_type=F32)


def _mm_nt(a, b):
    return lax.dot_general(a.astype(BF16), b.astype(BF16), (((1,), (1,)), ((), ())),
                           preferred_element_type=F32)


def _mm_tn(a, b):
    return lax.dot_general(a.astype(BF16), b.astype(BF16), (((0,), (0,)), ((), ())),
                           preferred_element_type=F32)


def _rms(x, g, eps):
    return x * lax.rsqrt(jnp.mean(x * x, axis=-1, keepdims=True) + eps) * g


def _half_mask(shape):
    return lax.broadcasted_iota(jnp.int32, shape, len(shape) - 1) % PAIR < HEAD


def _pair_sum(x, first):
    s1 = jnp.sum(jnp.where(first, x, 0.0), axis=-1, keepdims=True)
    s2 = jnp.sum(jnp.where(first, 0.0, x), axis=-1, keepdims=True)
    return jnp.where(first, s1, s2)


def _ffn_value(x, g_ref, win_ref, wout_ref, h_sc):
    d_ff = wout_ref.shape[0]
    xn = _rms(x, g_ref[...], NORM_EPS).astype(BF16)
    for f0 in range(0, d_ff, FFN_TF):
        gate = jnp.dot(xn, win_ref[:, f0:f0 + FFN_TF], preferred_element_type=F32)
        up = jnp.dot(xn, win_ref[:, d_ff + f0:d_ff + f0 + FFN_TF], preferred_element_type=F32)
        h_sc[:, f0:f0 + FFN_TF] = (gate * jax.nn.sigmoid(gate) * up).astype(BF16)
    return x + 0.5 * jnp.dot(h_sc[...], wout_ref[...], preferred_element_type=F32)


def _keys_values(x, ng_ref, w_ref, kg_ref, k_out, v_out):
    tm, c = x.shape
    n_heads = c // LANES
    kv = _mm(_rms(x, ng_ref[...], NORM_EPS), w_ref[...])
    first = _half_mask((tm, LANES))
    kg = kg_ref[...]
    lane = lax.broadcasted_iota(jnp.int32, (tm, LANES), 1)
    pos = lax.broadcasted_iota(jnp.int32, (tm, LANES), 0)
    base = jnp.where(lane >= 2 * len(LOG2E_PARTS), 0.0,
                     jnp.where(lane % 2 == 0, (pos // HEAD * HEAD).astype(F32), (pos % HEAD).astype(F32)))
    ones = jnp.ones((VT_ROWS - LANES, tm), BF16)
    for h in range(n_heads):
        kh = kv[:, h * LANES:(h + 1) * LANES]
        ms = _pair_sum(kh * kh, first) * (1.0 / HEAD)
        kn = kh * lax.rsqrt(ms + NORM_EPS) * kg
        slope = 2.0 ** (-8.0 * (h + 1) / n_heads)
        k_out[0, h] = jnp.concatenate([kn.astype(BF16), (base * slope).astype(BF16)], axis=-1)
        vt = kv[:, c + h * LANES:c + (h + 1) * LANES].T.astype(BF16)
        v_out[0, h, 0] = jnp.concatenate([vt, ones], axis=0)


def _queries(x, ng_ref, w_ref, qg_ref, q_out):
    tm, c = x.shape
    q = _mm(_rms(x, ng_ref[...], NORM_EPS), w_ref[...])
    first = _half_mask((tm, LANES))
    qg = qg_ref[...] * (HEAD ** -0.5 * LOG2E)
    for h in range(c // LANES):
        qh = q[:, h * LANES:(h + 1) * LANES]
        ms = _pair_sum(qh * qh, first) * (1.0 / HEAD)
        q_out[0, h] = (qh * lax.rsqrt(ms + NORM_EPS) * qg).astype(BF16)


def _block_kernel(*refs, pre, post):
    refs = list(refs)
    x = refs.pop(0)[...]
    if pre is not None:
        y_ref = refs.pop(0)
        y = jnp.concatenate([y_ref[0, p] for p in range(y_ref.shape[1])], axis=-1)
        if pre == "gated":
            y = y * refs.pop(0)[...]
        x = x + jnp.dot(y.astype(BF16), refs.pop(0)[...], preferred_element_type=F32)
    g_ref, win_ref, wout_ref = refs[:3]
    refs = refs[3:]
    post_refs = [refs.pop(0) for _ in range(3)] if post is not None else []
    o_ref = refs.pop(0)
    h_sc = refs.pop()
    x = _ffn_value(x, g_ref, win_ref, wout_ref, h_sc)
    o_ref[...] = x
    if post == "kv":
        _keys_values(x, *post_refs, *refs)
    elif post == "q":
        _queries(x, *post_refs, *refs)


def _block(x, batch, g, w_in, w_out, which, *, pre=None, post=None):
    m, c = x.shape
    t = m // batch
    d_ff = w_out.shape[-2]
    tm = min(ATT, t)
    bps = t // tm
    nh = c // LANES
    row_spec = pl.BlockSpec((tm, c), lambda i: (i, 0))
    head_idx = lambda i: (i // bps, 0, i % bps, 0)
    args, in_specs = [x], [row_spec]
    pre_kind = None
    if pre is not None:
        y, gate, w_o = pre
        pre_kind = "plain" if gate is None else "gated"
        args.append(y)
        in_specs.append(pl.BlockSpec((1, y.shape[1], tm, LANES), head_idx))
        if gate is not None:
            args.append(gate)
            in_specs.append(row_spec)
        args.append(w_o.astype(BF16))
        in_specs.append(_const_spec((c, c)))
    args += [g.reshape(1, c), w_in, w_out]
    in_specs += [_const_spec((1, c)), _const_spec((c, 2 * d_ff), which), _const_spec((d_ff, c), which)]
    out_specs, out_shape = [row_spec], [jax.ShapeDtypeStruct((m, c), F32)]
    post_kind = None
    if post is not None:
        post_kind, ng, w, head_norm = post
        args += [ng.reshape(1, c), w.astype(BF16), jnp.concatenate([head_norm, head_norm]).reshape(1, LANES)]
        in_specs += [_const_spec((1, c)), _const_spec(w.shape), _const_spec((1, LANES))]
        if post_kind == "kv":
            out_specs += [pl.BlockSpec((1, nh, tm, 2 * LANES), head_idx),
                          pl.BlockSpec((1, nh, 1, VT_ROWS, tm), lambda i: (i // bps, 0, i % bps, 0, 0))]
            out_shape += [jax.ShapeDtypeStruct((batch, nh, t, 2 * LANES), BF16),
                          jax.ShapeDtypeStruct((batch, nh, t // tm, VT_ROWS, tm), BF16)]
        else:
            out_specs.append(pl.BlockSpec((1, nh, tm, LANES), head_idx))
            out_shape.append(jax.ShapeDtypeStruct((batch, nh, t, LANES), BF16))
    return pl.pallas_call(
        functools.partial(_block_kernel, pre=pre_kind, post=post_kind),
        grid=(m // tm,),
        in_specs=in_specs,
        out_specs=out_specs,
        out_shape=out_shape,
        scratch_shapes=[pltpu.VMEM((tm, d_ff), BF16)],
        compiler_params=_cparams(("parallel",)),
        name="block_" + (pre_kind or "x") + "_" + (post_kind or "x"),
    )(*args)


def _rwkv_pre_kernel(x_ref, xp_ref, ng_ref, mu_ref, wrkv_ref, w0_ref, w1_ref, w2_ref, a0_ref, a1_ref,
                     a2_ref, g1_ref, g2_ref, kk_ref, ka_ref,
                     r_out, lw_out, k_out, v_out, kk_out, a_out, g_out, *, blocks_per_seq):
    tm, c = x_ref.shape
    ng = ng_ref[...]
    h = _rms(x_ref[...], ng, NORM_EPS)
    prev = _rms(xp_ref[SUBLANES - 1:SUBLANES, :], ng, NORM_EPS)
    prev = jnp.where(pl.program_id(0) % blocks_per_seq == 0, 0.0, prev)
    row = lax.broadcasted_iota(jnp.int32, (tm, c), 0)
    h_prev = jnp.where(row == 0, prev, pltpu.roll(h, 1, 0))
    dx = h_prev - h

    def mix(i):
        return h + dx * mu_ref[i:i + 1, :]

    r = _mm(mix(0), wrkv_ref[0])
    k = _mm(mix(1), wrkv_ref[1])
    v = _mm(mix(2), wrkv_ref[2])
    z = w0_ref[...] + _mm(jnp.tanh(_mm(mix(3), w1_ref[...])), w2_ref[...])
    lw = -math.exp(-0.5) * jax.nn.sigmoid(z)
    a = jax.nn.sigmoid(a0_ref[...] + _mm(_mm(mix(4), a1_ref[...]), a2_ref[...]))
    g_out[...] = _mm(jax.nn.sigmoid(_mm(mix(5), g1_ref[...])), g2_ref[...])
    kk = k * kk_ref[...]
    k = k * (1.0 + (a - 1.0) * ka_ref[...])
    first = _half_mask((tm, PAIR))
    for p in range(c // PAIR):
        sl = slice(p * PAIR, (p + 1) * PAIR)
        kkp = kk[:, sl]
        ss = _pair_sum(kkp * kkp, first)
        r_out[0, p] = r[:, sl]
        lw_out[0, p] = lw[:, sl]
        k_out[0, p] = k[:, sl]
        v_out[0, p] = v[:, sl]
        kk_out[0, p] = kkp * lax.rsqrt(jnp.maximum(ss, 1e-24))
        a_out[0, p] = a[:, sl]


def _rwkv_pre(x, batch, ng, mu, w_rkv, w0, w1, w2, a0, a1, a2, g1, g2, k_k, k_a, *, tm=512):
    m, c = x.shape
    t = m // batch
    tm = min(tm, t)
    bps = t // tm
    npair = c // PAIR
    row = lambda a: a.reshape(1, c)
    pair_spec = pl.BlockSpec((1, npair, tm, PAIR), lambda i: (i // bps, 0, i % bps, 0))
    pair_shape = jax.ShapeDtypeStruct((batch, npair, t, PAIR), F32)
    ws = [w_rkv.astype(BF16), row(w0), w1.astype(BF16), w2.astype(BF16), row(a0), a1.astype(BF16),
          a2.astype(BF16), g1.astype(BF16), g2.astype(BF16), row(k_k), row(k_a)]
    return pl.pallas_call(
        functools.partial(_rwkv_pre_kernel, blocks_per_seq=bps),
        grid=(m // tm,),
        in_specs=[pl.BlockSpec((tm, c), lambda i: (i, 0)),
                  pl.BlockSpec((SUBLANES, c), lambda i: (jnp.maximum(i * (tm // SUBLANES) - 1, 0), 0)),
                  _const_spec((1, c)), _const_spec(mu.shape)] + [_const_spec(w.shape) for w in ws],
        out_specs=[pair_spec] * 6 + [pl.BlockSpec((tm, c), lambda i: (i, 0))],
        out_shape=[pair_shape] * 6 + [jax.ShapeDtypeStruct((m, c), F32)],
        compiler_params=_cparams(("parallel",)),
        name="rwkv_pre",
    )(x, x, row(ng), mu, *ws)


def _stack(x, first):
    return jnp.concatenate([jnp.where(first, x, 0.0), jnp.where(first, 0.0, x)], axis=0)


def _wkv_kernel(r_ref, lw_ref, k_ref, v_ref, kk_ref, a_ref, rk_ref, lng_ref, lnb_ref, y_ref,
                s_sc, carry_sc, bv_sc, *, nblk):
    lb = r_ref.shape[2]
    nc = lb // CHUNK
    l2 = 2 * CHUNK
    i = pl.program_id(2)

    first = _half_mask((CHUNK, PAIR))
    ri = lax.broadcasted_iota(jnp.int32, (l2, l2), 0)
    ci = lax.broadcasted_iota(jnp.int32, (l2, l2), 1)
    same = (ri // CHUNK) == (ci // CHUNK)
    strict = same & (ri % CHUNK > ci % CHUNK)
    incl = same & (ri % CHUNK >= ci % CHUNK)
    eye = ri == ci
    ident = jnp.where(eye, 1.0, 0.0)
    rk = rk_ref[0]
    lng = lng_ref[0]
    lnb = lnb_ref[0]
    chs = range(nc)

    def prepare_stages():
        w = {}

        def prep():
            tri = jnp.where(lax.broadcasted_iota(jnp.int32, (CHUNK, CHUNK), 1)
                            <= lax.broadcasted_iota(jnp.int32, (CHUNK, CHUNK), 0), 1.0, 0.0).astype(BF16)
            lw_wide = jnp.concatenate([lw_ref[0, 0, ch * CHUNK:(ch + 1) * CHUNK, :] for ch in chs], axis=1)
            lw_hi = lw_wide.astype(BF16)
            lw_lo = (lw_wide - lw_hi.astype(F32)).astype(BF16)
            c_wide = (jnp.dot(tri, lw_hi, preferred_element_type=F32)
                      + jnp.dot(tri, lw_lo, preferred_element_type=F32))
            for ch in chs:
                sl = slice(ch * CHUNK, (ch + 1) * CHUNK)
                r = r_ref[0, 0, sl, :]
                k = k_ref[0, 0, sl, :]
                v = v_ref[0, 0, sl, :]
                kk = kk_ref[0, 0, sl, :]
                c = c_wide[:, ch * PAIR:(ch + 1) * PAIR]
                c_last = c[CHUNK - 1:CHUNK, :]
                e_neg = jnp.exp(-c)
                e_last = jnp.exp(c_last - c)
                b = kk * a_ref[0, 0, sl, :]
                rt = _stack(r * jnp.exp(c), first)
                at = _stack(-kk * jnp.exp(c - lw_ref[0, 0, sl, :]), first)
                w[ch] = dict(
                    rt=rt, at=at, vs=_stack(v, first), g_last=jnp.exp(c_last),
                    bonus=_pair_sum(r * k * rk, first) * v,
                    ar=jnp.concatenate([at, rt], axis=0),
                    bkt=jnp.concatenate([_stack(b * e_neg, first), _stack(k * e_neg, first)], axis=0),
                    bk_last=jnp.concatenate([_stack(b * e_last, first), _stack(k * e_last, first)], axis=0))

        def couplings():
            for ch in chs:
                d = w[ch]
                g = _mm_nt(d["ar"], d["bkt"])
                d["n"] = jnp.where(strict, g[:l2, :l2], 0.0)
                d["a_ak"] = jnp.where(strict, g[:l2, l2:], 0.0)
                d["a_rbk"] = jnp.concatenate([jnp.where(incl, g[l2:, :l2], 0.0),
                                              jnp.where(incl, g[l2:, l2:], 0.0)], axis=1)

        def start_inverse():
            for ch in chs:
                d = w[ch]
                d["aakv"] = _mm(d["a_ak"], d["vs"])
                d["tinv"] = ident + d["n"]
                d["pw"] = d["n"]

        def square():
            for ch in chs:
                w[ch]["pw"] = _mm(w[ch]["pw"], w[ch]["pw"])

        def extend():
            for ch in chs:
                w[ch]["tinv"] = w[ch]["tinv"] + _mm(w[ch]["tinv"], w[ch]["pw"])

        def solve():
            for ch in chs:
                d = w[ch]
                x = _mm(d["tinv"], jnp.concatenate([d["at"], d["aakv"]], axis=1))
                d["z"] = jnp.concatenate([x, jnp.concatenate([jnp.zeros_like(d["vs"]), d["vs"]], axis=1)],
                                         axis=0)

        def finish():
            for ch in chs:
                d = w[ch]
                m1 = _mm(d["a_rbk"], d["z"])
                m2 = _mm_tn(d["z"], d["bk_last"])
                carry_sc[ch, 0] = d["rt"] + m1[:, :PAIR]
                carry_sc[ch, 1] = m1[:, PAIR:]
                carry_sc[ch, 2] = jnp.where(eye, d["g_last"], 0.0) + m2[:PAIR]
                carry_sc[ch, 3] = m2[PAIR:]
                bv_sc[ch] = d["bonus"]

        rounds = int(math.log2(CHUNK)) - 1
        return [prep, couplings, start_inverse] + [square, extend] * rounds + [solve, finish]

    def advance(ch, s):
        ys = _mm_nt(carry_sc[ch, 0], s) + carry_sc[ch, 1]
        s = _mm(s, carry_sc[ch, 2]) + carry_sc[ch, 3]
        y = ys[:CHUNK] + ys[CHUNK:]
        mean = _pair_sum(y, first) * (1.0 / HEAD)
        d = y - mean
        var = _pair_sum(d * d, first) * (1.0 / HEAD)
        y_ref[0, 0, ch * CHUNK:(ch + 1) * CHUNK, :] = d * lax.rsqrt(var + RWKV_LN_EPS) * lng + lnb + bv_sc[ch]
        return s

    @pl.when(i == 0)
    def _():
        s_sc[...] = jnp.zeros_like(s_sc)
        for stage in prepare_stages():
            stage()

    @pl.when((i > 0) & (i < nblk))
    def _():
        s = s_sc[...]
        stages = prepare_stages()
        slots = len(stages) - 1
        for n_stage, stage in enumerate(stages):
            stage()
            for ch in chs:
                if ch * slots // nc == n_stage:
                    s = advance(ch, s)
        s_sc[...] = s

    @pl.when(i == nblk)
    def _():
        s = s_sc[...]
        for ch in chs:
            s = advance(ch, s)
        s_sc[...] = s


def _wkv_scan(r, lw, k, v, kk, a, r_k, ln_g, ln_b, *, lb=1024):
    batch, npair, t, _ = r.shape
    lb = min(lb, t)
    nblk = t // lb
    nc = lb // CHUNK
    in_spec = pl.BlockSpec((1, 1, lb, PAIR), lambda b, p, i: (b, p, jnp.minimum(i, nblk - 1), 0))
    out_spec = pl.BlockSpec((1, 1, lb, PAIR), lambda b, p, i: (b, p, jnp.maximum(i - 1, 0), 0))
    par_spec = pl.BlockSpec((1, 1, PAIR), lambda b, p, i: (p, 0, 0))
    par = lambda x: x.reshape(npair, 1, PAIR)
    return pl.pallas_call(
        functools.partial(_wkv_kernel, nblk=nblk),
        grid=(batch, npair, nblk + 1),
        in_specs=[in_spec] * 6 + [par_spec] * 3,
        out_specs=out_spec,
        out_shape=jax.ShapeDtypeStruct(r.shape, F32),
        scratch_shapes=[pltpu.VMEM((PAIR, PAIR), F32), pltpu.VMEM((nc, 4, PAIR, PAIR), F32),
                        pltpu.VMEM((nc, CHUNK, PAIR), F32)],
        compiler_params=_cparams(("parallel", "parallel", "arbitrary")),
        name="wkv_scan",
    )(r, lw, k, v, kk, a, par(r_k), par(ln_g), par(ln_b))


def _sublane_all(op, x):
    shift = SUBLANES // 2
    while shift:
        x = op(x, pltpu.roll(x, shift, 0))
        shift //= 2
    return x


def _diff_attn_kernel(q_ref, k_ref, vt_ref, lam_ref, sg_ref, o_ref, m_sc, acc_sc, sa_sc, sb_sc, qs_sc, *,
                      lam_init, cg, n_heads, unroll):
    bq = vt_ref.shape[4]
    dv = LANES
    rows = vt_ref.shape[3]
    ncg = 2 * bq // cg
    nq = q_ref.shape[2] // bq
    head = (pl.program_id(1) + 1).astype(F32)
    block_bias = jnp.exp2(jnp.full((SUBLANES, cg), -8.0 / n_heads, F32) * head) * (LOG2E * bq)
    groups = range(ncg)

    def stack_queries(i):
        q = q_ref[0, 0, pl.ds(pl.multiple_of(i * bq, bq), bq), :]
        first = _half_mask((bq, LANES))
        lane = lax.broadcasted_iota(jnp.int32, (bq, LANES), 1)
        feat = jnp.zeros((bq, LANES), F32)
        for n, part in enumerate(LOG2E_PARTS):
            feat = jnp.where(lane // 2 == n, part, feat)
        feat = feat.astype(BF16)
        zero = jnp.zeros_like(q)
        qs_sc[:bq] = jnp.concatenate([jnp.where(first, q, zero), feat], axis=1)
        qs_sc[bq:] = jnp.concatenate([jnp.where(first, zero, q), feat], axis=1)

    def scores(j, buf):
        kt = k_ref[0, 0, pl.ds(pl.multiple_of(j * bq, bq), bq), :]
        for g in groups:
            buf[g] = lax.dot_general(kt, qs_sc[g * cg:(g + 1) * cg], (((1,), (1,)), ((), ())),
                                     preferred_element_type=F32)

    def q_block(i, carry):
        m_sc[...] = jnp.full_like(m_sc, NEG_BIG)
        acc_sc[...] = jnp.zeros_like(acc_sc)

        def tile(j, buf, diag):
            q0 = [(g * cg) % bq for g in groups]
            nk = [min(bq, q0[g] + cg) if diag else bq for g in groups]
            offset = 0.0 if diag else block_bias * lax.convert_element_type(j - i, F32)
            p, alpha = [], []
            for g in groups:
                sg = buf[g, :nk[g], :]
                if diag:
                    key = lax.broadcasted_iota(jnp.int32, sg.shape, 0)
                    qry = lax.broadcasted_iota(jnp.int32, sg.shape, 1) + q0[g]
                    sg = jnp.where(key <= qry, sg, NEG_BIG)
                sg = sg.reshape(nk[g] // SUBLANES, SUBLANES, cg)
                m_prev = m_sc[g]
                m_new = jnp.maximum(m_prev, _sublane_all(jnp.maximum, jnp.max(sg, axis=0)) + offset)
                p.append(jnp.exp2(sg - (m_new - offset)[None]).reshape(nk[g], cg).astype(BF16))
                alpha.append(jnp.exp2(m_prev - m_new))
                m_sc[g] = m_new
            pv = [jnp.dot(vt_ref[0, 0, j, :, :nk[g]], p[g], preferred_element_type=F32) for g in groups]
            for g in groups:
                acc_sc[g] = (alpha[g][None] * acc_sc[g].reshape(rows // SUBLANES, SUBLANES, cg)
                             + pv[g].reshape(rows // SUBLANES, SUBLANES, cg)).reshape(rows, cg)

        bufs = (sa_sc, sb_sc)

        def run(first_block, count, last_is_diag):
            for n in range(count):
                last = n == count - 1
                if not (last and last_is_diag):
                    scores(first_block + n + 1, bufs[(n + 1) % 2])
                tile(first_block + n, bufs[n % 2], last and last_is_diag)

        def body(jj, carry):
            run(unroll * jj, unroll, False)
            return carry

        lax.fori_loop(0, i // unroll, body, 0)
        for rem in range(unroll):
            @pl.when(i % unroll == rem)
            def _():
                run(i - rem, rem + 1, True)

        stack_queries(jnp.minimum(i + 1, nq - 1))
        scores(0, sa_sc)

        o_t = []
        for g in groups:
            l = acc_sc[g, dv:dv + SUBLANES, :]
            o_t.append((acc_sc[g, :dv, :].reshape(dv // SUBLANES, SUBLANES, cg) / l[None]).reshape(dv, cg))
        half = ncg // 2
        o1 = jnp.concatenate(o_t[:half], axis=1)
        o2 = jnp.concatenate(o_t[half:], axis=1)
        lam = lam_ref[...]
        lam_full = (jnp.exp(jnp.sum(lam[0:1] * lam[1:2], axis=-1, keepdims=True))
                    - jnp.exp(jnp.sum(lam[2:3] * lam[3:4], axis=-1, keepdims=True)) + lam_init)
        o = (o1 - lam_full * o2).T
        o_ref[0, 0, pl.ds(pl.multiple_of(i * bq, bq), bq), :] = (_rms(o, sg_ref[...], SUBLN_EPS)
                                                                 * (1.0 - lam_init)).astype(BF16)
        return carry

    stack_queries(0)
    scores(0, sa_sc)
    lax.fori_loop(0, nq, q_block, 0)


def _diff_attn(q, k, vt, lam, subln, lam_init, *, unroll=8):
    assert unroll % 2 == 0
    batch, nh, t, _ = q.shape
    rows, bq = vt.shape[-2:]
    cg = min(MXU_WIDTH, bq)
    ncg = 2 * bq // cg
    return pl.pallas_call(
        functools.partial(_diff_attn_kernel, lam_init=lam_init, cg=cg, n_heads=nh, unroll=unroll),
        grid=(batch, nh),
        in_specs=[pl.BlockSpec((1, 1, t, LANES), lambda b, h: (b, h, 0, 0)),
                  pl.BlockSpec((1, 1, t, 2 * LANES), lambda b, h: (b, h, 0, 0)),
                  pl.BlockSpec((1, 1, t // bq, rows, bq), lambda b, h: (b, h, 0, 0, 0)),
                  _const_spec(lam.shape), _const_spec((1, LANES))],
        out_specs=pl.BlockSpec((1, 1, t, LANES), lambda b, h: (b, h, 0, 0)),
        out_shape=jax.ShapeDtypeStruct((batch, nh, t, LANES), BF16),
        scratch_shapes=[pltpu.VMEM((ncg, SUBLANES, cg), F32), pltpu.VMEM((ncg, rows, cg), F32),
                        pltpu.VMEM((ncg, bq, cg), F32), pltpu.VMEM((ncg, bq, cg), F32),
                        pltpu.VMEM((2 * bq, 2 * LANES), BF16)],
        compiler_params=_cparams(("parallel", "parallel")),
        name="diff_attn",
    )(q, k, vt, lam, subln.reshape(1, LANES))


def kernel(x, ffn_norm, ffn_w_in, ffn_w_out, mix_norm, rwkv_mu, rwkv_w_rkv, rwkv_w0, rwkv_w1, rwkv_w2, rwkv_a0, rwkv_a1, rwkv_a2, rwkv_g1, rwkv_g2, rwkv_k_k, rwkv_k_a, rwkv_r_k, rwkv_ln_g, rwkv_ln_b, rwkv_w_o, kv_norm, w_kv, k_norm, diff_w_q, diff_q_norm, diff_lambda, diff_subln, diff_w_o):
    batch, t, c = x.shape
    x = x.reshape(batch * t, c)
    w_in, w_out = ffn_w_in.astype(BF16), ffn_w_out.astype(BF16)

    x, = _block(x, batch, ffn_norm[0, 0], w_in, w_out, (0, 0))
    r, lw, k, v, kk, a, g = _rwkv_pre(x, batch, mix_norm[0], rwkv_mu[0], rwkv_w_rkv[0], rwkv_w0[0], rwkv_w1[0],
                                      rwkv_w2[0], rwkv_a0[0], rwkv_a1[0], rwkv_a2[0], rwkv_g1[0], rwkv_g2[0],
                                      rwkv_k_k[0], rwkv_k_a[0])
    y = _wkv_scan(r, lw, k, v, kk, a, rwkv_r_k[0], rwkv_ln_g[0], rwkv_ln_b[0])
    x, k_sh, v_sh = _block(x, batch, ffn_norm[0, 1], w_in, w_out, (0, 1),
                           pre=(y, g, rwkv_w_o[0]), post=("kv", kv_norm, w_kv, k_norm))

    x, q = _block(x, batch, ffn_norm[1, 0], w_in, w_out, (1, 0),
                  post=("q", mix_norm[1], diff_w_q[0], diff_q_norm[0]))
    lam_init = 0.8 - 0.6 * math.exp(-0.3 * 1)
    o = _diff_attn(q, k_sh, v_sh, diff_lambda[0], diff_subln[0], lam_init)
    x, = _block(x, batch, ffn_norm[1, 1], w_in, w_out, (1, 1), pre=(o, None, diff_w_o[0]))
    return x.reshape(batch, t, c)
```

```python
import functools
import math

import jax
import jax.numpy as jnp
from jax import lax
from jax.experimental import pallas as pl
from jax.experimental.pallas import tpu as pltpu

F32 = jnp.float32
BF16 = jnp.bfloat16

LANES = 128
SUBLANES = 8
MXU_WIDTH = 256
HEAD = 64
PAIR = 2 * HEAD
CHUNK = 64
ATT = 512
FFN_TF = MXU_WIDTH
NORM_EPS = 1e-6
SUBLN_EPS = 1e-5
RWKV_LN_EPS = 64e-5
VMEM_LIMIT = 56 * 1024 * 1024
NEG_BIG = -1e30
BF16_ROWS = 16
VT_ROWS = LANES + BF16_ROWS


def _bf16_parts(x, n):
    parts = []
    for _ in range(n):
        m, e = math.frexp(x)
        p = math.ldexp(round(m * 256.0) / 256.0, e)
        parts.append(p)
        x -= p
    return tuple(parts)


LOG2E = math.log2(math.e)
LOG2E_PARTS = _bf16_parts(LOG2E, 3)


def _cparams(sem):
    return pltpu.CompilerParams(dimension_semantics=sem, vmem_limit_bytes=VMEM_LIMIT)


def _const_spec(shape, lead=()):
    nd = len(shape)
    return pl.BlockSpec((None,) * len(lead) + tuple(shape), lambda *_: tuple(lead) + (0,) * nd,
                        pipeline_mode=pl.Buffered(1))


def _mm(a, b):
    return jnp.dot(a.astype(BF16), b.astype(BF16), preferred_element_type=F32)


def _mm_nt(a, b):
    return lax.dot_general(a.astype(BF16), b.astype(BF16), (((1,), (1,)), ((), ())),
                           preferred_element_type=F32)


def _mm_tn(a, b):
    return lax.dot_general(a.astype(BF16), b.astype(BF16), (((0,), (0,)), ((), ())),
                           preferred_element_type=F32)


def _rms(x, g, eps):
    return x * lax.rsqrt(jnp.mean(x * x, axis=-1, keepdims=True) + eps) * g


def _half_mask(shape):
    return lax.broadcasted_iota(jnp.int32, shape, len(shape) - 1) % PAIR < HEAD


def _pair_sum(x, first):
    s1 = jnp.sum(jnp.where(first, x, 0.0), axis=-1, keepdims=True)
    s2 = jnp.sum(jnp.where(first, 0.0, x), axis=-1, keepdims=True)
    return jnp.where(first, s1, s2)


def _ffn_value(x, g_ref, win_ref, wout_ref, h_sc):
    d_ff = wout_ref.shape[0]
    xn = _rms(x, g_ref[...], NORM_EPS).astype(BF16)
    for f0 in range(0, d_ff, FFN_TF):
        gate = jnp.dot(xn, win_ref[:, f0:f0 + FFN_TF], preferred_element_type=F32)
        up = jnp.dot(xn, win_ref[:, d_ff + f0:d_ff + f0 + FFN_TF], preferred_element_type=F32)
        h_sc[:, f0:f0 + FFN_TF] = (gate * jax.nn.sigmoid(gate) * up).astype(BF16)
    return x + 0.5 * jnp.dot(h_sc[...], wout_ref[...], preferred_element_type=F32)


def _keys_values(x, ng_ref, w_ref, kg_ref, k_out, v_out):
    tm, c = x.shape
    n_heads = c // LANES
    kv = _mm(_rms(x, ng_ref[...], NORM_EPS), w_ref[...])
    first = _half_mask((tm, LANES))
    kg = kg_ref[...]
    lane = lax.broadcasted_iota(jnp.int32, (tm, LANES), 1)
    pos = lax.broadcasted_iota(jnp.int32, (tm, LANES), 0)
    base = jnp.where(lane >= 2 * len(LOG2E_PARTS), 0.0,
                     jnp.where(lane % 2 == 0, (pos // HEAD * HEAD).astype(F32), (pos % HEAD).astype(F32)))
    ones = jnp.ones((VT_ROWS - LANES, tm), BF16)
    for h in range(n_heads):
        kh = kv[:, h * LANES:(h + 1) * LANES]
        ms = _pair_sum(kh * kh, first) * (1.0 / HEAD)
        kn = kh * lax.rsqrt(ms + NORM_EPS) * kg
        slope = 2.0 ** (-8.0 * (h + 1) / n_heads)
        k_out[0, h] = jnp.concatenate([kn.astype(BF16), (base * slope).astype(BF16)], axis=-1)
        vt = kv[:, c + h * LANES:c + (h + 1) * LANES].T.astype(BF16)
        v_out[0, h, 0] = jnp.concatenate([vt, ones], axis=0)


def _queries(x, ng_ref, w_ref, qg_ref, q_out):
    tm, c = x.shape
    q = _mm(_rms(x, ng_ref[...], NORM_EPS), w_ref[...])
    first = _half_mask((tm, LANES))
    qg = qg_ref[...] * (HEAD ** -0.5 * LOG2E)
    for h in range(c // LANES):
        qh = q[:, h * LANES:(h + 1) * LANES]
        ms = _pair_sum(qh * qh, first) * (1.0 / HEAD)
        q_out[0, h] = (qh * lax.rsqrt(ms + NORM_EPS) * qg).astype(BF16)


def _block_kernel(*refs, pre, post):
    refs = list(refs)
    x = refs.pop(0)[...]
    if pre is not None:
        y_ref = refs.pop(0)
        y = jnp.concatenate([y_ref[0, p] for p in range(y_ref.shape[1])], axis=-1)
        if pre == "gated":
            y = y * refs.pop(0)[...]
        x = x + jnp.dot(y.astype(BF16), refs.pop(0)[...], preferred_element_type=F32)
    g_ref, win_ref, wout_ref = refs[:3]
    refs = refs[3:]
    post_refs = [refs.pop(0) for _ in range(3)] if post is not None else []
    o_ref = refs.pop(0)
    h_sc = refs.pop()
    x = _ffn_value(x, g_ref, win_ref, wout_ref, h_sc)
    o_ref[...] = x
    if post == "kv":
        _keys_values(x, *post_refs, *refs)
    elif post == "q":
        _queries(x, *post_refs, *refs)


def _block(x, batch, g, w_in, w_out, which, *, pre=None, post=None):
    m, c = x.shape
    t = m // batch
    d_ff = w_out.shape[-2]
    tm = min(ATT, t)
    bps = t // tm
    nh = c // LANES
    row_spec = pl.BlockSpec((tm, c), lambda i: (i, 0))
    head_idx = lambda i: (i // bps, 0, i % bps, 0)
    args, in_specs = [x], [row_spec]
    pre_kind = None
    if pre is not None:
        y, gate, w_o = pre
        pre_kind = "plain" if gate is None else "gated"
        args.append(y)
        in_specs.append(pl.BlockSpec((1, y.shape[1], tm, LANES), head_idx))
        if gate is not None:
            args.append(gate)
            in_specs.append(row_spec)
        args.append(w_o.astype(BF16))
        in_specs.append(_const_spec((c, c)))
    args += [g.reshape(1, c), w_in, w_out]
    in_specs += [_const_spec((1, c)), _const_spec((c, 2 * d_ff), which), _const_spec((d_ff, c), which)]
    out_specs, out_shape = [row_spec], [jax.ShapeDtypeStruct((m, c), F32)]
    post_kind = None
    if post is not None:
        post_kind, ng, w, head_norm = post
        args += [ng.reshape(1, c), w.astype(BF16), jnp.concatenate([head_norm, head_norm]).reshape(1, LANES)]
        in_specs += [_const_spec((1, c)), _const_spec(w.shape), _const_spec((1, LANES))]
        if post_kind == "kv":
            out_specs += [pl.BlockSpec((1, nh, tm, 2 * LANES), head_idx),
                          pl.BlockSpec((1, nh, 1, VT_ROWS, tm), lambda i: (i // bps, 0, i % bps, 0, 0))]
            out_shape += [jax.ShapeDtypeStruct((batch, nh, t, 2 * LANES), BF16),
                          jax.ShapeDtypeStruct((batch, nh, t // tm, VT_ROWS, tm), BF16)]
        else:
            out_specs.append(pl.BlockSpec((1, nh, tm, LANES), head_idx))
            out_shape.append(jax.ShapeDtypeStruct((batch, nh, t, LANES), BF16))
    return pl.pallas_call(
        functools.partial(_block_kernel, pre=pre_kind, post=post_kind),
        grid=(m // tm,),
        in_specs=in_specs,
        out_specs=out_specs,
        out_shape=out_shape,
        scratch_shapes=[pltpu.VMEM((tm, d_ff), BF16)],
        compiler_params=_cparams(("parallel",)),
        name="block_" + (pre_kind or "x") + "_" + (post_kind or "x"),
    )(*args)


def _rwkv_pre_kernel(x_ref, xp_ref, ng_ref, mu_ref, wrkv_ref, w0_ref, w1_ref, w2_ref, a0_ref, a1_ref,
                     a2_ref, g1_ref, g2_ref, kk_ref, ka_ref,
                     r_out, lw_out, k_out, v_out, kk_out, a_out, g_out, *, blocks_per_seq):
    tm, c = x_ref.shape
    ng = ng_ref[...]
    h = _rms(x_ref[...], ng, NORM_EPS)
    prev = _rms(xp_ref[SUBLANES - 1:SUBLANES, :], ng, NORM_EPS)
    prev = jnp.where(pl.program_id(0) % blocks_per_seq == 0, 0.0, prev)
    row = lax.broadcasted_iota(jnp.int32, (tm, c), 0)
    h_prev = jnp.where(row == 0, prev, pltpu.roll(h, 1, 0))
    dx = h_prev - h

    def mix(i):
        return h + dx * mu_ref[i:i + 1, :]

    r = _mm(mix(0), wrkv_ref[0])
    k = _mm(mix(1), wrkv_ref[1])
    v = _mm(mix(2), wrkv_ref[2])
    z = w0_ref[...] + _mm(jnp.tanh(_mm(mix(3), w1_ref[...])), w2_ref[...])
    lw = -math.exp(-0.5) * jax.nn.sigmoid(z)
    a = jax.nn.sigmoid(a0_ref[...] + _mm(_mm(mix(4), a1_ref[...]), a2_ref[...]))
    g_out[...] = _mm(jax.nn.sigmoid(_mm(mix(5), g1_ref[...])), g2_ref[...])
    kk = k * kk_ref[...]
    k = k * (1.0 + (a - 1.0) * ka_ref[...])
    first = _half_mask((tm, PAIR))
    for p in range(c // PAIR):
        sl = slice(p * PAIR, (p + 1) * PAIR)
        kkp = kk[:, sl]
        ss = _pair_sum(kkp * kkp, first)
        r_out[0, p] = r[:, sl]
        lw_out[0, p] = lw[:, sl]
        k_out[0, p] = k[:, sl]
        v_out[0, p] = v[:, sl]
        kk_out[0, p] = kkp * lax.rsqrt(jnp.maximum(ss, 1e-24))
        a_out[0, p] = a[:, sl]


def _rwkv_pre(x, batch, ng, mu, w_rkv, w0, w1, w2, a0, a1, a2, g1, g2, k_k, k_a, *, tm=512):
    m, c = x.shape
    t = m // batch
    tm = min(tm, t)
    bps = t // tm
    npair = c // PAIR
    row = lambda a: a.reshape(1, c)
    pair_spec = pl.BlockSpec((1, npair, tm, PAIR), lambda i: (i // bps, 0, i % bps, 0))
    pair_shape = jax.ShapeDtypeStruct((batch, npair, t, PAIR), F32)
    ws = [w_rkv.astype(BF16), row(w0), w1.astype(BF16), w2.astype(BF16), row(a0), a1.astype(BF16),
          a2.astype(BF16), g1.astype(BF16), g2.astype(BF16), row(k_k), row(k_a)]
    return pl.pallas_call(
        functools.partial(_rwkv_pre_kernel, blocks_per_seq=bps),
        grid=(m // tm,),
        in_specs=[pl.BlockSpec((tm, c), lambda i: (i, 0)),
                  pl.BlockSpec((SUBLANES, c), lambda i: (jnp.maximum(i * (tm // SUBLANES) - 1, 0), 0)),
                  _const_spec((1, c)), _const_spec(mu.shape)] + [_const_spec(w.shape) for w in ws],
        out_specs=[pair_spec] * 6 + [pl.BlockSpec((tm, c), lambda i: (i, 0))],
        out_shape=[pair_shape] * 6 + [jax.ShapeDtypeStruct((m, c), F32)],
        compiler_params=_cparams(("parallel",)),
        name="rwkv_pre",
    )(x, x, row(ng), mu, *ws)


def _stack(x, first):
    return jnp.concatenate([jnp.where(first, x, 0.0), jnp.where(first, 0.0, x)], axis=0)


def _wkv_kernel(r_ref, lw_ref, k_ref, v_ref, kk_ref, a_ref, rk_ref, lng_ref, lnb_ref, y_ref,
                s_sc, carry_sc, bv_sc, *, nblk):
    lb = r_ref.shape[2]
    nc = lb // CHUNK
    l2 = 2 * CHUNK
    i = pl.program_id(2)

    first = _half_mask((CHUNK, PAIR))
    ri = lax.broadcasted_iota(jnp.int32, (l2, l2), 0)
    ci = lax.broadcasted_iota(jnp.int32, (l2, l2), 1)
    same = (ri // CHUNK) == (ci // CHUNK)
    strict = same & (ri % CHUNK > ci % CHUNK)
    incl = same & (ri % CHUNK >= ci % CHUNK)
    eye = ri == ci
    ident = jnp.where(eye, 1.0, 0.0)
    rk = rk_ref[0]
    lng = lng_ref[0]
    lnb = lnb_ref[0]
    chs = range(nc)

    def prepare_stages():
        w = {}

        def prep():
            tri = jnp.where(lax.broadcasted_iota(jnp.int32, (CHUNK, CHUNK), 1)
                            <= lax.broadcasted_iota(jnp.int32, (CHUNK, CHUNK), 0), 1.0, 0.0).astype(BF16)
            lw_wide = jnp.concatenate([lw_ref[0, 0, ch * CHUNK:(ch + 1) * CHUNK, :] for ch in chs], axis=1)
            lw_hi = lw_wide.astype(BF16)
            lw_lo = (lw_wide - lw_hi.astype(F32)).astype(BF16)
            c_wide = (jnp.dot(tri, lw_hi, preferred_element_type=F32)
                      + jnp.dot(tri, lw_lo, preferred_element_type=F32))
            for ch in chs:
                sl = slice(ch * CHUNK, (ch + 1) * CHUNK)
                r = r_ref[0, 0, sl, :]
                k = k_ref[0, 0, sl, :]
                v = v_ref[0, 0, sl, :]
                kk = kk_ref[0, 0, sl, :]
                c = c_wide[:, ch * PAIR:(ch + 1) * PAIR]
                c_last = c[CHUNK - 1:CHUNK, :]
                e_neg = jnp.exp(-c)
                e_last = jnp.exp(c_last - c)
                b = kk * a_ref[0, 0, sl, :]
                rt = _stack(r * jnp.exp(c), first)
                at = _stack(-kk * jnp.exp(c - lw_ref[0, 0, sl, :]), first)
                w[ch] = dict(
                    rt=rt, at=at, vs=_stack(v, first), g_last=jnp.exp(c_last),
                    bonus=_pair_sum(r * k * rk, first) * v,
                    ar=jnp.concatenate([at, rt], axis=0),
                    bkt=jnp.concatenate([_stack(b * e_neg, first), _stack(k * e_neg, first)], axis=0),
                    bk_last=jnp.concatenate([_stack(b * e_last, first), _stack(k * e_last, first)], axis=0))

        def couplings():
            for ch in chs:
                d = w[ch]
                g = _mm_nt(d["ar"], d["bkt"])
                d["n"] = jnp.where(strict, g[:l2, :l2], 0.0)
                d["a_ak"] = jnp.where(strict, g[:l2, l2:], 0.0)
                d["a_rbk"] = jnp.concatenate([jnp.where(incl, g[l2:, :l2], 0.0),
                                              jnp.where(incl, g[l2:, l2:], 0.0)], axis=1)

        def start_inverse():
            for ch in chs:
                d = w[ch]
                d["aakv"] = _mm(d["a_ak"], d["vs"])
                d["tinv"] = ident + d["n"]
                d["pw"] = d["n"]

        def square():
            for ch in chs:
                w[ch]["pw"] = _mm(w[ch]["pw"], w[ch]["pw"])

        def extend():
            for ch in chs:
                w[ch]["tinv"] = w[ch]["tinv"] + _mm(w[ch]["tinv"], w[ch]["pw"])

        def solve():
            for ch in chs:
                d = w[ch]
                x = _mm(d["tinv"], jnp.concatenate([d["at"], d["aakv"]], axis=1))
                d["z"] = jnp.concatenate([x, jnp.concatenate([jnp.zeros_like(d["vs"]), d["vs"]], axis=1)],
                                         axis=0)

        def finish():
            for ch in chs:
                d = w[ch]
                m1 = _mm(d["a_rbk"], d["z"])
                m2 = _mm_tn(d["z"], d["bk_last"])
                carry_sc[ch, 0] = d["rt"] + m1[:, :PAIR]
                carry_sc[ch, 1] = m1[:, PAIR:]
                carry_sc[ch, 2] = jnp.where(eye, d["g_last"], 0.0) + m2[:PAIR]
                carry_sc[ch, 3] = m2[PAIR:]
                bv_sc[ch] = d["bonus"]

        rounds = int(math.log2(CHUNK)) - 1
        return [prep, couplings, start_inverse] + [square, extend] * rounds + [solve, finish]

    def advance(ch, s):
        ys = _mm_nt(carry_sc[ch, 0], s) + carry_sc[ch, 1]
        s = _mm(s, carry_sc[ch, 2]) + carry_sc[ch, 3]
        y = ys[:CHUNK] + ys[CHUNK:]
        mean = _pair_sum(y, first) * (1.0 / HEAD)
        d = y - mean
        var = _pair_sum(d * d, first) * (1.0 / HEAD)
        y_ref[0, 0, ch * CHUNK:(ch + 1) * CHUNK, :] = d * lax.rsqrt(var + RWKV_LN_EPS) * lng + lnb + bv_sc[ch]
        return s

    @pl.when(i == 0)
    def _():
        s_sc[...] = jnp.zeros_like(s_sc)
        for stage in prepare_stages():
            stage()

    @pl.when((i > 0) & (i < nblk))
    def _():
        s = s_sc[...]
        stages = prepare_stages()
        slots = len(stages) - 1
        for n_stage, stage in enumerate(stages):
            stage()
            for ch in chs:
                if ch * slots // nc == n_stage:
                    s = advance(ch, s)
        s_sc[...] = s

    @pl.when(i == nblk)
    def _():
        s = s_sc[...]
        for ch in chs:
            s = advance(ch, s)
        s_sc[...] = s


def _wkv_scan(r, lw, k, v, kk, a, r_k, ln_g, ln_b, *, lb=1024):
    batch, npair, t, _ = r.shape
    lb = min(lb, t)
    nblk = t // lb
    nc = lb // CHUNK
    in_spec = pl.BlockSpec((1, 1, lb, PAIR), lambda b, p, i: (b, p, jnp.minimum(i, nblk - 1), 0))
    out_spec = pl.BlockSpec((1, 1, lb, PAIR), lambda b, p, i: (b, p, jnp.maximum(i - 1, 0), 0))
    par_spec = pl.BlockSpec((1, 1, PAIR), lambda b, p, i: (p, 0, 0))
    par = lambda x: x.reshape(npair, 1, PAIR)
    return pl.pallas_call(
        functools.partial(_wkv_kernel, nblk=nblk),
        grid=(batch, npair, nblk + 1),
        in_specs=[in_spec] * 6 + [par_spec] * 3,
        out_specs=out_spec,
        out_shape=jax.ShapeDtypeStruct(r.shape, F32),
        scratch_shapes=[pltpu.VMEM((PAIR, PAIR), F32), pltpu.VMEM((nc, 4, PAIR, PAIR), F32),
                        pltpu.VMEM((nc, CHUNK, PAIR), F32)],
        compiler_params=_cparams(("parallel", "parallel", "arbitrary")),
        name="wkv_scan",
    )(r, lw, k, v, kk, a, par(r_k), par(ln_g), par(ln_b))


def _sublane_all(op, x):
    shift = SUBLANES // 2
    while shift:
        x = op(x, pltpu.roll(x, shift, 0))
        shift //= 2
    return x


def _loop_attn_kernel(q_ref, k_ref, vt_ref, lam_ref, sg_ref, o_ref, m_sc, acc_sc, sa_sc, sb_sc, qs_sc, *,
                      lam_init, cg, n_heads, unroll):
    bq = vt_ref.shape[4]
    dv = LANES
    rows = vt_ref.shape[3]
    ncg = 2 * bq // cg
    nq = q_ref.shape[2] // bq
    head = (pl.program_id(1) + 1).astype(F32)
    block_bias = jnp.exp2(jnp.full((SUBLANES, cg), -8.0 / n_heads, F32) * head) * (LOG2E * bq)
    groups = range(ncg)

    def stack_queries(i):
        q = q_ref[0, 0, pl.ds(pl.multiple_of(i * bq, bq), bq), :]
        first = _half_mask((bq, LANES))
        lane = lax.broadcasted_iota(jnp.int32, (bq, LANES), 1)
        feat = jnp.zeros((bq, LANES), F32)
        for n, part in enumerate(LOG2E_PARTS):
            feat = jnp.where(lane // 2 == n, part, feat)
        feat = feat.astype(BF16)
        zero = jnp.zeros_like(q)
        qs_sc[:bq] = jnp.concatenate([jnp.where(first, q, zero), feat], axis=1)
        qs_sc[bq:] = jnp.concatenate([jnp.where(first, zero, q), feat], axis=1)

    def scores(j, buf):
        kt = k_ref[0, 0, pl.ds(pl.multiple_of(j * bq, bq), bq), :]
        for g in groups:
            buf[g] = lax.dot_general(kt, qs_sc[g * cg:(g + 1) * cg], (((1,), (1,)), ((), ())),
                                     preferred_element_type=F32)

    def q_block(i, carry):
        m_sc[...] = jnp.full_like(m_sc, NEG_BIG)
        acc_sc[...] = jnp.zeros_like(acc_sc)

        def tile(j, buf, diag):
            q0 = [(g * cg) % bq for g in groups]
            nk = [min(bq, q0[g] + cg) if diag else bq for g in groups]
            offset = 0.0 if diag else block_bias * lax.convert_element_type(j - i, F32)
            p, alpha = [], []
            for g in groups:
                sg = buf[g, :nk[g], :]
                if diag:
                    key = lax.broadcasted_iota(jnp.int32, sg.shape, 0)
                    qry = lax.broadcasted_iota(jnp.int32, sg.shape, 1) + q0[g]
                    sg = jnp.where(key <= qry, sg, NEG_BIG)
                sg = sg.reshape(nk[g] // SUBLANES, SUBLANES, cg)
                m_prev = m_sc[g]
                m_new = jnp.maximum(m_prev, _sublane_all(jnp.maximum, jnp.max(sg, axis=0)) + offset)
                p.append(jnp.exp2(sg - (m_new - offset)[None]).reshape(nk[g], cg).astype(BF16))
                alpha.append(jnp.exp2(m_prev - m_new))
                m_sc[g] = m_new
            pv = [jnp.dot(vt_ref[0, 0, j, :, :nk[g]], p[g], preferred_element_type=F32) for g in groups]
            for g in groups:
                acc_sc[g] = (alpha[g][None] * acc_sc[g].reshape(rows // SUBLANES, SUBLANES, cg)
                             + pv[g].reshape(rows // SUBLANES, SUBLANES, cg)).reshape(rows, cg)

        bufs = (sa_sc, sb_sc)

        def run(first_block, count, last_is_diag):
            for n in range(count):
                last = n == count - 1
                if not (last and last_is_diag):
                    scores(first_block + n + 1, bufs[(n + 1) % 2])
                tile(first_block + n, bufs[n % 2], last and last_is_diag)

        def body(jj, carry):
            run(unroll * jj, unroll, False)
            return carry

        lax.fori_loop(0, i // unroll, body, 0)
        for rem in range(unroll):
            @pl.when(i % unroll == rem)
            def _():
                run(i - rem, rem + 1, True)

        stack_queries(jnp.minimum(i + 1, nq - 1))
        scores(0, sa_sc)

        o_t = []
        for g in groups:
            l = acc_sc[g, dv:dv + SUBLANES, :]
            o_t.append((acc_sc[g, :dv, :].reshape(dv // SUBLANES, SUBLANES, cg) / l[None]).reshape(dv, cg))
        half = ncg // 2
        o1 = jnp.concatenate(o_t[:half], axis=1)
        o2 = jnp.concatenate(o_t[half:], axis=1)
        lam = lam_ref[...]
        lam_full = (jnp.exp(jnp.sum(lam[0:1] * lam[1:2], axis=-1, keepdims=True))
                    - jnp.exp(jnp.sum(lam[2:3] * lam[3:4], axis=-1, keepdims=True)) + lam_init)
        o = (o1 - lam_full * o2).T
        o_ref[0, 0, pl.ds(pl.multiple_of(i * bq, bq), bq), :] = (_rms(o, sg_ref[...], SUBLN_EPS)
                                                                 * (1.0 - lam_init)).astype(BF16)
        return carry

    stack_queries(0)
    scores(0, sa_sc)
    lax.fori_loop(0, nq, q_block, 0)


def _loop_attn(q, k, vt, lam, subln, lam_init, *, unroll=8):
    assert unroll % 2 == 0
    batch, nh, t, _ = q.shape
    rows, bq = vt.shape[-2:]
    cg = min(MXU_WIDTH, bq)
    ncg = 2 * bq // cg
    return pl.pallas_call(
        functools.partial(_loop_attn_kernel, lam_init=lam_init, cg=cg, n_heads=nh, unroll=unroll),
        grid=(batch, nh),
        in_specs=[pl.BlockSpec((1, 1, t, LANES), lambda b, h: (b, h, 0, 0)),
                  pl.BlockSpec((1, 1, t, 2 * LANES), lambda b, h: (b, h, 0, 0)),
                  pl.BlockSpec((1, 1, t // bq, rows, bq), lambda b, h: (b, h, 0, 0, 0)),
                  _const_spec(lam.shape), _const_spec((1, LANES))],
        out_specs=pl.BlockSpec((1, 1, t, LANES), lambda b, h: (b, h, 0, 0)),
        out_shape=jax.ShapeDtypeStruct((batch, nh, t, LANES), BF16),
        scratch_shapes=[pltpu.VMEM((ncg, SUBLANES, cg), F32), pltpu.VMEM((ncg, rows, cg), F32),
                        pltpu.VMEM((ncg, bq, cg), F32), pltpu.VMEM((ncg, bq, cg), F32),
                        pltpu.VMEM((2 * bq, 2 * LANES), BF16)],
        compiler_params=_cparams(("parallel", "parallel")),
        name="diff_attn",
    )(q, k, vt, lam, subln.reshape(1, LANES))


def _diff_attn_kernel(q_ref, k_ref, vt_ref, lam_ref, sg_ref, o_ref, m_sc, acc_sc, sa_sc, sb_sc, qs_sc, *,
                      lam_init, cg, n_heads, unroll):
    bq = vt_ref.shape[4]
    dv = LANES
    rows = vt_ref.shape[3]
    ncg = 2 * bq // cg
    n_pairs = q_ref.shape[2] // (2 * bq)
    head = (pl.program_id(1) + 1).astype(F32)
    block_bias = jnp.exp2(jnp.full((SUBLANES, cg), -8.0 / n_heads, F32) * head) * (LOG2E * bq)
    first_blk = list(range(ncg))
    second_blk = list(range(ncg, 2 * ncg))
    both = first_blk + second_blk
    q0 = [(g * cg) % bq for g in both]
    bufs = (sa_sc, sb_sc)

    def stack_queries(i):
        first = _half_mask((bq, LANES))
        lane = lax.broadcasted_iota(jnp.int32, (bq, LANES), 1)
        feat = jnp.zeros((bq, LANES), F32)
        for n, part in enumerate(LOG2E_PARTS):
            feat = jnp.where(lane // 2 == n, part, feat)
        feat = feat.astype(BF16)
        for blk in range(2):
            q = q_ref[0, 0, pl.ds(pl.multiple_of((2 * i + blk) * bq, bq), bq), :]
            zero = jnp.zeros_like(q)
            qs_sc[2 * blk * bq:(2 * blk + 1) * bq] = jnp.concatenate([jnp.where(first, q, zero), feat], axis=1)
            qs_sc[(2 * blk + 1) * bq:(2 * blk + 2) * bq] = jnp.concatenate([jnp.where(first, zero, q), feat], axis=1)

    def scores(j, buf, gs):
        kt = k_ref[0, 0, pl.ds(pl.multiple_of(j * bq, bq), bq), :]
        for g in gs:
            buf[g] = lax.dot_general(kt, qs_sc[g * cg:(g + 1) * cg], (((1,), (1,)), ((), ())),
                                     preferred_element_type=F32)

    def pair(i, carry):
        m_sc[...] = jnp.full_like(m_sc, NEG_BIG)
        acc_sc[...] = jnp.zeros_like(acc_sc)

        def tile(j, buf, full, diag):
            p, alpha, nk = {}, {}, {}
            for g in list(full) + list(diag):
                on_diag = g in diag
                nk[g] = min(bq, q0[g] + cg) if on_diag else bq
                sg = buf[g, :nk[g], :]
                if on_diag:
                    key = lax.broadcasted_iota(jnp.int32, sg.shape, 0)
                    qry = lax.broadcasted_iota(jnp.int32, sg.shape, 1) + q0[g]
                    sg = jnp.where(key <= qry, sg, NEG_BIG)
                    offset = 0.0
                else:
                    offset = block_bias * lax.convert_element_type(j - (2 * i + g // ncg), F32)
                sg = sg.reshape(nk[g] // SUBLANES, SUBLANES, cg)
                m_prev = m_sc[g]
                m_new = jnp.maximum(m_prev, _sublane_all(jnp.maximum, jnp.max(sg, axis=0)) + offset)
                p[g] = jnp.exp2(sg - (m_new - offset)[None]).reshape(nk[g], cg).astype(BF16)
                alpha[g] = jnp.exp2(m_prev - m_new)
                m_sc[g] = m_new
            pv = {g: jnp.dot(vt_ref[0, 0, j, :, :nk[g]], p[g], preferred_element_type=F32) for g in p}
            for g in p:
                acc_sc[g] = (alpha[g][None] * acc_sc[g].reshape(rows // SUBLANES, SUBLANES, cg)
                             + pv[g].reshape(rows // SUBLANES, SUBLANES, cg)).reshape(rows, cg)

        def run(first_block, count, with_tail):
            for n in range(count):
                scores(first_block + n + 1, bufs[(n + 1) % 2], both)
                tile(first_block + n, bufs[n % 2], both, [])
            if with_tail:
                scores(2 * i + 1, bufs[(count + 1) % 2], second_blk)
                tile(2 * i, bufs[count % 2], second_blk, first_blk)
                tile(2 * i + 1, bufs[(count + 1) % 2], [], second_blk)

        def body(jj, carry):
            run(unroll * jj, unroll, False)
            return carry

        n_full = 2 * i
        lax.fori_loop(0, n_full // unroll, body, 0)
        for rem in range(0, unroll, 2):
            @pl.when(n_full % unroll == rem)
            def _():
                run(n_full - rem, rem, True)

        stack_queries(jnp.minimum(i + 1, n_pairs - 1))
        scores(0, sa_sc, both)

        lam = lam_ref[...]
        lam_full = (jnp.exp(jnp.sum(lam[0:1] * lam[1:2], axis=-1, keepdims=True))
                    - jnp.exp(jnp.sum(lam[2:3] * lam[3:4], axis=-1, keepdims=True)) + lam_init)
        for blk, gs in enumerate((first_blk, second_blk)):
            o_t = []
            for g in gs:
                l = acc_sc[g, dv:dv + SUBLANES, :]
                o_t.append((acc_sc[g, :dv, :].reshape(dv // SUBLANES, SUBLANES, cg) / l[None]).reshape(dv, cg))
            half = ncg // 2
            o1 = jnp.concatenate(o_t[:half], axis=1)
            o2 = jnp.concatenate(o_t[half:], axis=1)
            o = (o1 - lam_full * o2).T
            o_ref[0, 0, pl.ds(pl.multiple_of((2 * i + blk) * bq, bq), bq), :] = (
                _rms(o, sg_ref[...], SUBLN_EPS) * (1.0 - lam_init)).astype(BF16)
        return carry

    stack_queries(0)
    scores(0, sa_sc, both)
    lax.fori_loop(0, n_pairs, pair, 0)


def _diff_attn(q, k, vt, lam, subln, lam_init, *, unroll=8):
    assert unroll % 2 == 0
    batch, nh, t, _ = q.shape
    rows, bq = vt.shape[-2:]
    assert (t // bq) % 2 == 0
    cg = min(MXU_WIDTH, bq)
    ncg = 4 * bq // cg
    return pl.pallas_call(
        functools.partial(_diff_attn_kernel, lam_init=lam_init, cg=cg, n_heads=nh, unroll=unroll),
        grid=(batch, nh),
        in_specs=[pl.BlockSpec((1, 1, t, LANES), lambda b, h: (b, h, 0, 0)),
                  pl.BlockSpec((1, 1, t, 2 * LANES), lambda b, h: (b, h, 0, 0)),
                  pl.BlockSpec((1, 1, t // bq, rows, bq), lambda b, h: (b, h, 0, 0, 0)),
                  _const_spec(lam.shape), _const_spec((1, LANES))],
        out_specs=pl.BlockSpec((1, 1, t, LANES), lambda b, h: (b, h, 0, 0)),
        out_shape=jax.ShapeDtypeStruct((batch, nh, t, LANES), BF16),
        scratch_shapes=[pltpu.VMEM((ncg, SUBLANES, cg), F32), pltpu.VMEM((ncg, rows, cg), F32),
                        pltpu.VMEM((ncg, bq, cg), F32), pltpu.VMEM((ncg, bq, cg), F32),
                        pltpu.VMEM((4 * bq, 2 * LANES), BF16)],
        compiler_params=_cparams(("parallel", "parallel")),
        name="diff_attn",
    )(q, k, vt, lam, subln.reshape(1, LANES))


def kernel(x, ffn_norm, ffn_w_in, ffn_w_out, mix_norm, rwkv_mu, rwkv_w_rkv, rwkv_w0, rwkv_w1, rwkv_w2, rwkv_a0, rwkv_a1, rwkv_a2, rwkv_g1, rwkv_g2, rwkv_k_k, rwkv_k_a, rwkv_r_k, rwkv_ln_g, rwkv_ln_b, rwkv_w_o, kv_norm, w_kv, k_norm, diff_w_q, diff_q_norm, diff_lambda, diff_subln, diff_w_o):
    batch, t, c = x.shape
    x = x.reshape(batch * t, c)
    w_in, w_out = ffn_w_in.astype(BF16), ffn_w_out.astype(BF16)

    x, = _block(x, batch, ffn_norm[0, 0], w_in, w_out, (0, 0))
    r, lw, k, v, kk, a, g = _rwkv_pre(x, batch, mix_norm[0], rwkv_mu[0], rwkv_w_rkv[0], rwkv_w0[0], rwkv_w1[0],
                                      rwkv_w2[0], rwkv_a0[0], rwkv_a1[0], rwkv_a2[0], rwkv_g1[0], rwkv_g2[0],
                                      rwkv_k_k[0], rwkv_k_a[0])
    y = _wkv_scan(r, lw, k, v, kk, a, rwkv_r_k[0], rwkv_ln_g[0], rwkv_ln_b[0])
    x, k_sh, v_sh = _block(x, batch, ffn_norm[0, 1], w_in, w_out, (0, 1),
                           pre=(y, g, rwkv_w_o[0]), post=("kv", kv_norm, w_kv, k_norm))

    x, q = _block(x, batch, ffn_norm[1, 0], w_in, w_out, (1, 0),
                  post=("q", mix_norm[1], diff_w_q[0], diff_q_norm[0]))
    lam_init = 0.8 - 0.6 * math.exp(-0.3 * 1)
    o = _diff_attn(q, k_sh, v_sh, diff_lambda[0], diff_subln[0], lam_init)
    x, = _block(x, batch, ffn_norm[1, 1], w_in, w_out, (1, 1), pre=(o, None, diff_w_o[0]))
    return x.reshape(batch, t, c)
```

```python
import functools
import math

import jax
import jax.numpy as jnp
from jax import lax
from jax.experimental import pallas as pl
from jax.experimental.pallas import tpu as pltpu

F32 = jnp.float32
BF16 = jnp.bfloat16

LANES = 128
SUBLANES = 8
MXU_WIDTH = 256
HEAD = 64
PAIR = 2 * HEAD
CHUNK = 64
ATT = 512
FFN_TF = MXU_WIDTH
NORM_EPS = 1e-6
SUBLN_EPS = 1e-5
RWKV_LN_EPS = 64e-5
VMEM_LIMIT = 56 * 1024 * 1024
NEG_BIG = -1e30
BF16_ROWS = 16
VT_ROWS = LANES + BF16_ROWS


def _bf16_parts(x, n):
    parts = []
    for _ in range(n):
        m, e = math.frexp(x)
        p = math.ldexp(round(m * 256.0) / 256.0, e)
        parts.append(p)
        x -= p
    return tuple(parts)


LOG2E = math.log2(math.e)
LOG2E_PARTS = _bf16_parts(LOG2E, 3)


def _cparams(sem):
    return pltpu.CompilerParams(dimension_semantics=sem, vmem_limit_bytes=VMEM_LIMIT)


def _const_spec(shape, lead=()):
    nd = len(shape)
    return pl.BlockSpec((None,) * len(lead) + tuple(shape), lambda *_: tuple(lead) + (0,) * nd,
                        pipeline_mode=pl.Buffered(1))


def _mm(a, b):
    return jnp.dot(a.astype(BF16), b.astype(BF16), preferred_element_type=F32)


def _mm_nt(a, b):
    return lax.dot_general(a.astype(BF16), b.astype(BF16), (((1,), (1,)), ((), ())),
                           preferred_element_type=F32)


def _mm_tn(a, b):
    return lax.dot_general(a.astype(BF16), b.astype(BF16), (((0,), (0,)), ((), ())),
                           preferred_element_type=F32)


def _rms(x, g, eps):
    return x * lax.rsqrt(jnp.mean(x * x, axis=-1, keepdims=True) + eps) * g


def _half_mask(shape):
    return lax.broadcasted_iota(jnp.int32, shape, len(shape) - 1) % PAIR < HEAD


def _pair_sum(x, first):
    s1 = jnp.sum(jnp.where(first, x, 0.0), axis=-1, keepdims=True)
    s2 = jnp.sum(jnp.where(first, 0.0, x), axis=-1, keepdims=True)
    return jnp.where(first, s1, s2)


def _ffn_value(x, g_ref, win_ref, wout_ref, h_sc):
    d_ff = wout_ref.shape[0]
    xn = _rms(x, g_ref[...], NORM_EPS).astype(BF16)
    for f0 in range(0, d_ff, FFN_TF):
        gate = jnp.dot(xn, win_ref[:, f0:f0 + FFN_TF], preferred_element_type=F32)
        up = jnp.dot(xn, win_ref[:, d_ff + f0:d_ff + f0 + FFN_TF], preferred_element_type=F32)
        h_sc[:, f0:f0 + FFN_TF] = (gate * jax.nn.sigmoid(gate) * up).astype(BF16)
    return x + 0.5 * jnp.dot(h_sc[...], wout_ref[...], preferred_element_type=F32)


def _keys_values(x, ng_ref, w_ref, kg_ref, k_out, v_out):
    tm, c = x.shape
    n_heads = c // LANES
    kv = _mm(_rms(x, ng_ref[...], NORM_EPS), w_ref[...])
    first = _half_mask((tm, LANES))
    kg = kg_ref[...]
    lane = lax.broadcasted_iota(jnp.int32, (tm, LANES), 1)
    pos = lax.broadcasted_iota(jnp.int32, (tm, LANES), 0)
    base = jnp.where(lane >= 2 * len(LOG2E_PARTS), 0.0,
                     jnp.where(lane % 2 == 0, (pos // HEAD * HEAD).astype(F32), (pos % HEAD).astype(F32)))
    ones = jnp.ones((VT_ROWS - LANES, tm), BF16)
    for h in range(n_heads):
        kh = kv[:, h * LANES:(h + 1) * LANES]
        ms = _pair_sum(kh * kh, first) * (1.0 / HEAD)
        kn = kh * lax.rsqrt(ms + NORM_EPS) * kg
        slope = 2.0 ** (-8.0 * (h + 1) / n_heads)
        k_out[0, h] = jnp.concatenate([kn.astype(BF16), (base * slope).astype(BF16)], axis=-1)
        vt = kv[:, c + h * LANES:c + (h + 1) * LANES].T.astype(BF16)
        v_out[0, h, 0] = jnp.concatenate([vt, ones], axis=0)


def _queries(x, ng_ref, w_ref, qg_ref, q_out):
    tm, c = x.shape
    q = _mm(_rms(x, ng_ref[...], NORM_EPS), w_ref[...])
    first = _half_mask((tm, LANES))
    qg = qg_ref[...] * (HEAD ** -0.5 * LOG2E)
    for h in range(c // LANES):
        qh = q[:, h * LANES:(h + 1) * LANES]
        ms = _pair_sum(qh * qh, first) * (1.0 / HEAD)
        q_out[0, h] = (qh * lax.rsqrt(ms + NORM_EPS) * qg).astype(BF16)


def _block_kernel(*refs, pre, post):
    refs = list(refs)
    x = refs.pop(0)[...]
    if pre is not None:
        y_ref = refs.pop(0)
        y = jnp.concatenate([y_ref[0, p] for p in range(y_ref.shape[1])], axis=-1)
        if pre == "gated":
            y = y * refs.pop(0)[...]
        x = x + jnp.dot(y.astype(BF16), refs.pop(0)[...], preferred_element_type=F32)
    g_ref, win_ref, wout_ref = refs[:3]
    refs = refs[3:]
    post_refs = [refs.pop(0) for _ in range(3)] if post is not None else []
    o_ref = refs.pop(0)
    h_sc = refs.pop()
    x = _ffn_value(x, g_ref, win_ref, wout_ref, h_sc)
    o_ref[...] = x
    if post == "kv":
        _keys_values(x, *post_refs, *refs)
    elif post == "q":
        _queries(x, *post_refs, *refs)


def _block(x, batch, g, w_in, w_out, which, *, pre=None, post=None):
    m, c = x.shape
    t = m // batch
    d_ff = w_out.shape[-2]
    tm = min(ATT if post is not None and post[0] == "kv" else 2 * ATT, t)
    bps = t // tm
    nh = c // LANES
    row_spec = pl.BlockSpec((tm, c), lambda i: (i, 0))
    head_idx = lambda i: (i // bps, 0, i % bps, 0)
    args, in_specs = [x], [row_spec]
    pre_kind = None
    if pre is not None:
        y, gate, w_o = pre
        pre_kind = "plain" if gate is None else "gated"
        args.append(y)
        in_specs.append(pl.BlockSpec((1, y.shape[1], tm, LANES), head_idx))
        if gate is not None:
            args.append(gate)
            in_specs.append(row_spec)
        args.append(w_o.astype(BF16))
        in_specs.append(_const_spec((c, c)))
    args += [g.reshape(1, c), w_in, w_out]
    in_specs += [_const_spec((1, c)), _const_spec((c, 2 * d_ff), which), _const_spec((d_ff, c), which)]
    out_specs, out_shape = [row_spec], [jax.ShapeDtypeStruct((m, c), F32)]
    post_kind = None
    if post is not None:
        post_kind, ng, w, head_norm = post
        args += [ng.reshape(1, c), w.astype(BF16), jnp.concatenate([head_norm, head_norm]).reshape(1, LANES)]
        in_specs += [_const_spec((1, c)), _const_spec(w.shape), _const_spec((1, LANES))]
        if post_kind == "kv":
            out_specs += [pl.BlockSpec((1, nh, tm, 2 * LANES), head_idx),
                          pl.BlockSpec((1, nh, 1, VT_ROWS, tm), lambda i: (i // bps, 0, i % bps, 0, 0))]
            out_shape += [jax.ShapeDtypeStruct((batch, nh, t, 2 * LANES), BF16),
                          jax.ShapeDtypeStruct((batch, nh, t // tm, VT_ROWS, tm), BF16)]
        else:
            out_specs.append(pl.BlockSpec((1, nh, tm, LANES), head_idx))
            out_shape.append(jax.ShapeDtypeStruct((batch, nh, t, LANES), BF16))
    return pl.pallas_call(
        functools.partial(_block_kernel, pre=pre_kind, post=post_kind),
        grid=(m // tm,),
        in_specs=in_specs,
        out_specs=out_specs,
        out_shape=out_shape,
        scratch_shapes=[pltpu.VMEM((tm, d_ff), BF16)],
        compiler_params=_cparams(("parallel",)),
        name="block_" + (pre_kind or "x") + "_" + (post_kind or "x"),
    )(*args)


def _rwkv_pre_kernel(x_ref, xp_ref, ng_ref, mu_ref, wrkv_ref, w0_ref, w1_ref, w2_ref, a0_ref, a1_ref,
                     a2_ref, g1_ref, g2_ref, kk_ref, ka_ref,
                     r_out, lw_out, k_out, v_out, kk_out, a_out, g_out, *, blocks_per_seq):
    tm, c = x_ref.shape
    ng = ng_ref[...]
    h = _rms(x_ref[...], ng, NORM_EPS)
    prev = _rms(xp_ref[SUBLANES - 1:SUBLANES, :], ng, NORM_EPS)
    prev = jnp.where(pl.program_id(0) % blocks_per_seq == 0, 0.0, prev)
    row = lax.broadcasted_iota(jnp.int32, (tm, c), 0)
    h_prev = jnp.where(row == 0, prev, pltpu.roll(h, 1, 0))
    dx = h_prev - h

    def mix(i):
        return h + dx * mu_ref[i:i + 1, :]

    r = _mm(mix(0), wrkv_ref[0])
    k = _mm(mix(1), wrkv_ref[1])
    v = _mm(mix(2), wrkv_ref[2])
    z = w0_ref[...] + _mm(jnp.tanh(_mm(mix(3), w1_ref[...])), w2_ref[...])
    lw = -math.exp(-0.5) * jax.nn.sigmoid(z)
    a = jax.nn.sigmoid(a0_ref[...] + _mm(_mm(mix(4), a1_ref[...]), a2_ref[...]))
    g_out[...] = _mm(jax.nn.sigmoid(_mm(mix(5), g1_ref[...])), g2_ref[...])
    kk = k * kk_ref[...]
    k = k * (1.0 + (a - 1.0) * ka_ref[...])
    first = _half_mask((tm, PAIR))
    for p in range(c // PAIR):
        sl = slice(p * PAIR, (p + 1) * PAIR)
        kkp = kk[:, sl]
        ss = _pair_sum(kkp * kkp, first)
        r_out[0, p] = r[:, sl]
        lw_out[0, p] = lw[:, sl]
        k_out[0, p] = k[:, sl]
        v_out[0, p] = v[:, sl]
        kk_out[0, p] = kkp * lax.rsqrt(jnp.maximum(ss, 1e-24))
        a_out[0, p] = a[:, sl]


def _rwkv_pre(x, batch, ng, mu, w_rkv, w0, w1, w2, a0, a1, a2, g1, g2, k_k, k_a, *, tm=512):
    m, c = x.shape
    t = m // batch
    tm = min(tm, t)
    bps = t // tm
    npair = c // PAIR
    row = lambda a: a.reshape(1, c)
    pair_spec = pl.BlockSpec((1, npair, tm, PAIR), lambda i: (i // bps, 0, i % bps, 0))
    pair_shape = jax.ShapeDtypeStruct((batch, npair, t, PAIR), F32)
    ws = [w_rkv.astype(BF16), row(w0), w1.astype(BF16), w2.astype(BF16), row(a0), a1.astype(BF16),
          a2.astype(BF16), g1.astype(BF16), g2.astype(BF16), row(k_k), row(k_a)]
    return pl.pallas_call(
        functools.partial(_rwkv_pre_kernel, blocks_per_seq=bps),
        grid=(m // tm,),
        in_specs=[pl.BlockSpec((tm, c), lambda i: (i, 0)),
                  pl.BlockSpec((SUBLANES, c), lambda i: (jnp.maximum(i * (tm // SUBLANES) - 1, 0), 0)),
                  _const_spec((1, c)), _const_spec(mu.shape)] + [_const_spec(w.shape) for w in ws],
        out_specs=[pair_spec] * 6 + [pl.BlockSpec((tm, c), lambda i: (i, 0))],
        out_shape=[pair_shape] * 6 + [jax.ShapeDtypeStruct((m, c), F32)],
        compiler_params=_cparams(("parallel",)),
        name="rwkv_pre",
    )(x, x, row(ng), mu, *ws)


def _stack(x, first):
    return jnp.concatenate([jnp.where(first, x, 0.0), jnp.where(first, 0.0, x)], axis=0)


def _wkv_kernel(r_ref, lw_ref, k_ref, v_ref, kk_ref, a_ref, rk_ref, lng_ref, lnb_ref, y_ref,
                s_sc, carry_sc, bv_sc, *, nblk):
    lb = r_ref.shape[2]
    nc = lb // CHUNK
    l2 = 2 * CHUNK
    i = pl.program_id(2)

    first = _half_mask((CHUNK, PAIR))
    ri = lax.broadcasted_iota(jnp.int32, (l2, l2), 0)
    ci = lax.broadcasted_iota(jnp.int32, (l2, l2), 1)
    same = (ri // CHUNK) == (ci // CHUNK)
    strict = same & (ri % CHUNK > ci % CHUNK)
    incl = same & (ri % CHUNK >= ci % CHUNK)
    eye = ri == ci
    ident = jnp.where(eye, 1.0, 0.0)
    rk = rk_ref[0]
    lng = lng_ref[0]
    lnb = lnb_ref[0]
    chs = range(nc)

    def prepare_stages():
        w = {}

        def prep():
            tri = jnp.where(lax.broadcasted_iota(jnp.int32, (CHUNK, CHUNK), 1)
                            <= lax.broadcasted_iota(jnp.int32, (CHUNK, CHUNK), 0), 1.0, 0.0).astype(BF16)
            lw_wide = jnp.concatenate([lw_ref[0, 0, ch * CHUNK:(ch + 1) * CHUNK, :] for ch in chs], axis=1)
            lw_hi = lw_wide.astype(BF16)
            lw_lo = (lw_wide - lw_hi.astype(F32)).astype(BF16)
            c_wide = (jnp.dot(tri, lw_hi, preferred_element_type=F32)
                      + jnp.dot(tri, lw_lo, preferred_element_type=F32))
            for ch in chs:
                sl = slice(ch * CHUNK, (ch + 1) * CHUNK)
                r = r_ref[0, 0, sl, :]
                k = k_ref[0, 0, sl, :]
                v = v_ref[0, 0, sl, :]
                kk = kk_ref[0, 0, sl, :]
                c = c_wide[:, ch * PAIR:(ch + 1) * PAIR]
                c_last = c[CHUNK - 1:CHUNK, :]
                e_neg = jnp.exp(-c)
                e_last = jnp.exp(c_last - c)
                b = kk * a_ref[0, 0, sl, :]
                rt = _stack(r * jnp.exp(c), first)
                at = _stack(-kk * jnp.exp(c - lw_ref[0, 0, sl, :]), first)
                w[ch] = dict(
                    rt=rt, at=at, vs=_stack(v, first), g_last=jnp.exp(c_last),
                    bonus=_pair_sum(r * k * rk, first) * v,
                    ar=jnp.concatenate([at, rt], axis=0),
                    bkt=jnp.concatenate([_stack(b * e_neg, first), _stack(k * e_neg, first)], axis=0),
                    bk_last=jnp.concatenate([_stack(b * e_last, first), _stack(k * e_last, first)], axis=0))

        def couplings():
            for ch in chs:
                d = w[ch]
                g = _mm_nt(d["ar"], d["bkt"])
                d["n"] = jnp.where(strict, g[:l2, :l2], 0.0)
                d["a_ak"] = jnp.where(strict, g[:l2, l2:], 0.0)
                d["a_rbk"] = jnp.concatenate([jnp.where(incl, g[l2:, :l2], 0.0),
                                              jnp.where(incl, g[l2:, l2:], 0.0)], axis=1)

        def start_inverse():
            for ch in chs:
                d = w[ch]
                d["aakv"] = _mm(d["a_ak"], d["vs"])
                d["tinv"] = ident + d["n"]
                d["pw"] = d["n"]

        def square():
            for ch in chs:
                w[ch]["pw"] = _mm(w[ch]["pw"], w[ch]["pw"])

        def extend():
            for ch in chs:
                w[ch]["tinv"] = w[ch]["tinv"] + _mm(w[ch]["tinv"], w[ch]["pw"])

        def solve():
            for ch in chs:
                d = w[ch]
                x = _mm(d["tinv"], jnp.concatenate([d["at"], d["aakv"]], axis=1))
                d["z"] = jnp.concatenate([x, jnp.concatenate([jnp.zeros_like(d["vs"]), d["vs"]], axis=1)],
                                         axis=0)

        def finish():
            for ch in chs:
                d = w[ch]
                m1 = _mm(d["a_rbk"], d["z"])
                m2 = _mm_tn(d["z"], d["bk_last"])
                carry_sc[ch, 0] = d["rt"] + m1[:, :PAIR]
                carry_sc[ch, 1] = m1[:, PAIR:]
                carry_sc[ch, 2] = jnp.where(eye, d["g_last"], 0.0) + m2[:PAIR]
                carry_sc[ch, 3] = m2[PAIR:]
                bv_sc[ch] = d["bonus"]

        rounds = int(math.log2(CHUNK)) - 1
        return [prep, couplings, start_inverse] + [square, extend] * rounds + [solve, finish]

    def advance(ch, s):
        ys = _mm_nt(carry_sc[ch, 0], s) + carry_sc[ch, 1]
        s = _mm(s, carry_sc[ch, 2]) + carry_sc[ch, 3]
        y = ys[:CHUNK] + ys[CHUNK:]
        mean = _pair_sum(y, first) * (1.0 / HEAD)
        d = y - mean
        var = _pair_sum(d * d, first) * (1.0 / HEAD)
        y_ref[0, 0, ch * CHUNK:(ch + 1) * CHUNK, :] = d * lax.rsqrt(var + RWKV_LN_EPS) * lng + lnb + bv_sc[ch]
        return s

    @pl.when(i == 0)
    def _():
        s_sc[...] = jnp.zeros_like(s_sc)
        for stage in prepare_stages():
            stage()

    @pl.when((i > 0) & (i < nblk))
    def _():
        s = s_sc[...]
        stages = prepare_stages()
        slots = len(stages) - 1
        for n_stage, stage in enumerate(stages):
            stage()
            for ch in chs:
                if ch * slots // nc == n_stage:
                    s = advance(ch, s)
        s_sc[...] = s

    @pl.when(i == nblk)
    def _():
        s = s_sc[...]
        for ch in chs:
            s = advance(ch, s)
        s_sc[...] = s


def _wkv_scan(r, lw, k, v, kk, a, r_k, ln_g, ln_b, *, lb=1024):
    batch, npair, t, _ = r.shape
    lb = min(lb, t)
    nblk = t // lb
    nc = lb // CHUNK
    in_spec = pl.BlockSpec((1, 1, lb, PAIR), lambda b, p, i: (b, p, jnp.minimum(i, nblk - 1), 0))
    out_spec = pl.BlockSpec((1, 1, lb, PAIR), lambda b, p, i: (b, p, jnp.maximum(i - 1, 0), 0))
    par_spec = pl.BlockSpec((1, 1, PAIR), lambda b, p, i: (p, 0, 0))
    par = lambda x: x.reshape(npair, 1, PAIR)
    return pl.pallas_call(
        functools.partial(_wkv_kernel, nblk=nblk),
        grid=(batch, npair, nblk + 1),
        in_specs=[in_spec] * 6 + [par_spec] * 3,
        out_specs=out_spec,
        out_shape=jax.ShapeDtypeStruct(r.shape, F32),
        scratch_shapes=[pltpu.VMEM((PAIR, PAIR), F32), pltpu.VMEM((nc, 4, PAIR, PAIR), F32),
                        pltpu.VMEM((nc, CHUNK, PAIR), F32)],
        compiler_params=_cparams(("parallel", "parallel", "arbitrary")),
        name="wkv_scan",
    )(r, lw, k, v, kk, a, par(r_k), par(ln_g), par(ln_b))


def _sublane_all(op, x):
    shift = SUBLANES // 2
    while shift:
        x = op(x, pltpu.roll(x, shift, 0))
        shift //= 2
    return x


def _diff_attn_kernel(q_ref, k_ref, vt_ref, lam_ref, sg_ref, o_ref, m_sc, acc_sc, sa_sc, sb_sc, qs_sc, *,
                      lam_init, cg, n_heads, unroll):
    bq = vt_ref.shape[4]
    dv = LANES
    rows = vt_ref.shape[3]
    ncg = 2 * bq // cg
    n_pairs = q_ref.shape[2] // (2 * bq)
    head = (pl.program_id(1) + 1).astype(F32)
    block_bias = jnp.exp2(jnp.full((SUBLANES, cg), -8.0 / n_heads, F32) * head) * (LOG2E * bq)
    first_blk = list(range(ncg))
    second_blk = list(range(ncg, 2 * ncg))
    both = first_blk + second_blk
    q0 = [(g * cg) % bq for g in both]
    bufs = (sa_sc, sb_sc)

    def stack_queries(i):
        first = _half_mask((bq, LANES))
        lane = lax.broadcasted_iota(jnp.int32, (bq, LANES), 1)
        feat = jnp.zeros((bq, LANES), F32)
        for n, part in enumerate(LOG2E_PARTS):
            feat = jnp.where(lane // 2 == n, part, feat)
        feat = feat.astype(BF16)
        for blk in range(2):
            q = q_ref[0, 0, pl.ds(pl.multiple_of((2 * i + blk) * bq, bq), bq), :]
            zero = jnp.zeros_like(q)
            qs_sc[2 * blk * bq:(2 * blk + 1) * bq] = jnp.concatenate([jnp.where(first, q, zero), feat], axis=1)
            qs_sc[(2 * blk + 1) * bq:(2 * blk + 2) * bq] = jnp.concatenate([jnp.where(first, zero, q), feat], axis=1)

    def scores(j, buf, gs):
        kt = k_ref[0, 0, pl.ds(pl.multiple_of(j * bq, bq), bq), :]
        for g in gs:
            buf[g] = lax.dot_general(kt, qs_sc[g * cg:(g + 1) * cg], (((1,), (1,)), ((), ())),
                                     preferred_element_type=F32)

    def pair(i, carry):
        m_sc[...] = jnp.full_like(m_sc, NEG_BIG)
        acc_sc[...] = jnp.zeros_like(acc_sc)

        def tile(j, buf, full, diag):
            p, alpha, nk = {}, {}, {}
            for g in list(full) + list(diag):
                on_diag = g in diag
                nk[g] = min(bq, q0[g] + cg) if on_diag else bq
                sg = buf[g, :nk[g], :]
                if on_diag:
                    key = lax.broadcasted_iota(jnp.int32, sg.shape, 0)
                    qry = lax.broadcasted_iota(jnp.int32, sg.shape, 1) + q0[g]
                    sg = jnp.where(key <= qry, sg, NEG_BIG)
                    offset = 0.0
                else:
                    offset = block_bias * lax.convert_element_type(j - (2 * i + g // ncg), F32)
                sg = sg.reshape(nk[g] // SUBLANES, SUBLANES, cg)
                m_prev = m_sc[g]
                m_new = jnp.maximum(m_prev, _sublane_all(jnp.maximum, jnp.max(sg, axis=0)) + offset)
                p[g] = jnp.exp2(sg - (m_new - offset)[None]).reshape(nk[g], cg).astype(BF16)
                alpha[g] = jnp.exp2(m_prev - m_new)
                m_sc[g] = m_new
            pv = {g: jnp.dot(vt_ref[0, 0, j, :, :nk[g]], p[g], preferred_element_type=F32) for g in p}
            for g in p:
                acc_sc[g] = (alpha[g][None] * acc_sc[g].reshape(rows // SUBLANES, SUBLANES, cg)
                             + pv[g].reshape(rows // SUBLANES, SUBLANES, cg)).reshape(rows, cg)

        def run(first_block, count, with_tail):
            for n in range(count):
                scores(first_block + n + 1, bufs[(n + 1) % 2], both)
                tile(first_block + n, bufs[n % 2], both, [])
            if with_tail:
                scores(2 * i + 1, bufs[(count + 1) % 2], second_blk)
                tile(2 * i, bufs[count % 2], second_blk, first_blk)
                tile(2 * i + 1, bufs[(count + 1) % 2], [], second_blk)

        def body(jj, carry):
            run(unroll * jj, unroll, False)
            return carry

        n_full = 2 * i
        lax.fori_loop(0, n_full // unroll, body, 0)
        for rem in range(0, unroll, 2):
            @pl.when(n_full % unroll == rem)
            def _():
                run(n_full - rem, rem, True)

        stack_queries(jnp.minimum(i + 1, n_pairs - 1))
        scores(0, sa_sc, both)

        lam = lam_ref[...]
        lam_full = (jnp.exp(jnp.sum(lam[0:1] * lam[1:2], axis=-1, keepdims=True))
                    - jnp.exp(jnp.sum(lam[2:3] * lam[3:4], axis=-1, keepdims=True)) + lam_init)
        for blk, gs in enumerate((first_blk, second_blk)):
            o_t = []
            for g in gs:
                l = acc_sc[g, dv:dv + SUBLANES, :]
                o_t.append((acc_sc[g, :dv, :].reshape(dv // SUBLANES, SUBLANES, cg) / l[None]).reshape(dv, cg))
            half = ncg // 2
            o1 = jnp.concatenate(o_t[:half], axis=1)
            o2 = jnp.concatenate(o_t[half:], axis=1)
            o = (o1 - lam_full * o2).T
            o_ref[0, 0, pl.ds(pl.multiple_of((2 * i + blk) * bq, bq), bq), :] = (
                _rms(o, sg_ref[...], SUBLN_EPS) * (1.0 - lam_init)).astype(BF16)
        return carry

    stack_queries(0)
    scores(0, sa_sc, both)
    lax.fori_loop(0, n_pairs, pair, 0)


def _diff_attn(q, k, vt, lam, subln, lam_init, *, unroll=8):
    assert unroll % 2 == 0
    batch, nh, t, _ = q.shape
    rows, bq = vt.shape[-2:]
    assert (t // bq) % 2 == 0
    cg = min(MXU_WIDTH, bq)
    ncg = 4 * bq // cg
    return pl.pallas_call(
        functools.partial(_diff_attn_kernel, lam_init=lam_init, cg=cg, n_heads=nh, unroll=unroll),
        grid=(batch, nh),
        in_specs=[pl.BlockSpec((1, 1, t, LANES), lambda b, h: (b, h, 0, 0)),
                  pl.BlockSpec((1, 1, t, 2 * LANES), lambda b, h: (b, h, 0, 0)),
                  pl.BlockSpec((1, 1, t // bq, rows, bq), lambda b, h: (b, h, 0, 0, 0)),
                  _const_spec(lam.shape), _const_spec((1, LANES))],
        out_specs=pl.BlockSpec((1, 1, t, LANES), lambda b, h: (b, h, 0, 0)),
        out_shape=jax.ShapeDtypeStruct((batch, nh, t, LANES), BF16),
        scratch_shapes=[pltpu.VMEM((ncg, SUBLANES, cg), F32), pltpu.VMEM((ncg, rows, cg), F32),
                        pltpu.VMEM((ncg, bq, cg), F32), pltpu.VMEM((ncg, bq, cg), F32),
                        pltpu.VMEM((4 * bq, 2 * LANES), BF16)],
        compiler_params=_cparams(("parallel", "parallel")),
        name="diff_attn",
    )(q, k, vt, lam, subln.reshape(1, LANES))


def kernel(x, ffn_norm, ffn_w_in, ffn_w_out, mix_norm, rwkv_mu, rwkv_w_rkv, rwkv_w0, rwkv_w1, rwkv_w2, rwkv_a0, rwkv_a1, rwkv_a2, rwkv_g1, rwkv_g2, rwkv_k_k, rwkv_k_a, rwkv_r_k, rwkv_ln_g, rwkv_ln_b, rwkv_w_o, kv_norm, w_kv, k_norm, diff_w_q, diff_q_norm, diff_lambda, diff_subln, diff_w_o):
    batch, t, c = x.shape
    x = x.reshape(batch * t, c)
    w_in, w_out = ffn_w_in.astype(BF16), ffn_w_out.astype(BF16)

    x, = _block(x, batch, ffn_norm[0, 0], w_in, w_out, (0, 0))
    r, lw, k, v, kk, a, g = _rwkv_pre(x, batch, mix_norm[0], rwkv_mu[0], rwkv_w_rkv[0], rwkv_w0[0], rwkv_w1[0],
                                      rwkv_w2[0], rwkv_a0[0], rwkv_a1[0], rwkv_a2[0], rwkv_g1[0], rwkv_g2[0],
                                      rwkv_k_k[0], rwkv_k_a[0])
    y = _wkv_scan(r, lw, k, v, kk, a, rwkv_r_k[0], rwkv_ln_g[0], rwkv_ln_b[0])
    x, k_sh, v_sh = _block(x, batch, ffn_norm[0, 1], w_in, w_out, (0, 1),
                           pre=(y, g, rwkv_w_o[0]), post=("kv", kv_norm, w_kv, k_norm))

    x, q = _block(x, batch, ffn_norm[1, 0], w_in, w_out, (1, 0),
                  post=("q", mix_norm[1], diff_w_q[0], diff_q_norm[0]))
    lam_init = 0.8 - 0.6 * math.exp(-0.3 * 1)
    o = _diff_attn(q, k_sh, v_sh, diff_lambda[0], diff_subln[0], lam_init)
    x, = _block(x, batch, ffn_norm[1, 1], w_in, w_out, (1, 1), pre=(o, None, diff_w_o[0]))
    return x.reshape(batch, t, c)
```

```python
import functools
import math

import jax
import jax.numpy as jnp
from jax import lax
from jax.experimental import pallas as pl
from jax.experimental.pallas import tpu as pltpu

F32 = jnp.float32
BF16 = jnp.bfloat16

LANES = 128
SUBLANES = 8
MXU_WIDTH = 256
HEAD = 64
PAIR = 2 * HEAD
CHUNK = 64
ATT = 512
FFN_TF = MXU_WIDTH
NORM_EPS = 1e-6
SUBLN_EPS = 1e-5
RWKV_LN_EPS = 64e-5
VMEM_LIMIT = 56 * 1024 * 1024
NEG_BIG = -1e30
BF16_ROWS = 16
VT_ROWS = LANES + BF16_ROWS


def _bf16_parts(x, n):
    parts = []
    for _ in range(n):
        m, e = math.frexp(x)
        p = math.ldexp(round(m * 256.0) / 256.0, e)
        parts.append(p)
        x -= p
    return tuple(parts)


LOG2E = math.log2(math.e)
LOG2E_PARTS = _bf16_parts(LOG2E, 3)


def _cparams(sem):
    return pltpu.CompilerParams(dimension_semantics=sem, vmem_limit_bytes=VMEM_LIMIT)


def _const_spec(shape, lead=()):
    nd = len(shape)
    return pl.BlockSpec((None,) * len(lead) + tuple(shape), lambda *_: tuple(lead) + (0,) * nd,
                        pipeline_mode=pl.Buffered(1))


def _mm(a, b):
    return jnp.dot(a.astype(BF16), b.astype(BF16), preferred_element_type=F32)


def _mm_nt(a, b):
    return lax.dot_general(a.astype(BF16), b.astype(BF16), (((1,), (1,)), ((), ())),
                           preferred_element_type=F32)


def _mm_tn(a, b):
    return lax.dot_general(a.astype(BF16), b.astype(BF16), (((0,), (0,)), ((), ())),
                           preferred_element_type=F32)


def _rms(x, g, eps):
    return x * lax.rsqrt(jnp.mean(x * x, axis=-1, keepdims=True) + eps) * g


def _half_mask(shape):
    return lax.broadcasted_iota(jnp.int32, shape, len(shape) - 1) % PAIR < HEAD


def _pair_sum(x, first):
    s1 = jnp.sum(jnp.where(first, x, 0.0), axis=-1, keepdims=True)
    s2 = jnp.sum(jnp.where(first, 0.0, x), axis=-1, keepdims=True)
    return jnp.where(first, s1, s2)


def _ffn_value(x, g_ref, win_ref, wout_ref, h_sc):
    d_ff = wout_ref.shape[0]
    xn = _rms(x, g_ref[...], NORM_EPS).astype(BF16)
    for f0 in range(0, d_ff, FFN_TF):
        gate = jnp.dot(xn, win_ref[:, f0:f0 + FFN_TF], preferred_element_type=F32)
        up = jnp.dot(xn, win_ref[:, d_ff + f0:d_ff + f0 + FFN_TF], preferred_element_type=F32)
        h_sc[:, f0:f0 + FFN_TF] = (gate * jax.nn.sigmoid(gate) * up).astype(BF16)
    return x + 0.5 * jnp.dot(h_sc[...], wout_ref[...], preferred_element_type=F32)


def _keys_values(x, ng_ref, w_ref, kg_ref, k_out, v_out):
    tm, c = x.shape
    n_heads = c // LANES
    kv = _mm(_rms(x, ng_ref[...], NORM_EPS), w_ref[...])
    first = _half_mask((tm, LANES))
    kg = kg_ref[...]
    lane = lax.broadcasted_iota(jnp.int32, (tm, LANES), 1)
    pos = lax.broadcasted_iota(jnp.int32, (tm, LANES), 0)
    base = jnp.where(lane >= 2 * len(LOG2E_PARTS), 0.0,
                     jnp.where(lane % 2 == 0, (pos // HEAD * HEAD).astype(F32), (pos % HEAD).astype(F32)))
    ones = jnp.ones((VT_ROWS - LANES, tm), BF16)
    for h in range(n_heads):
        kh = kv[:, h * LANES:(h + 1) * LANES]
        ms = _pair_sum(kh * kh, first) * (1.0 / HEAD)
        kn = kh * lax.rsqrt(ms + NORM_EPS) * kg
        slope = 2.0 ** (-8.0 * (h + 1) / n_heads)
        k_out[0, h] = jnp.concatenate([kn.astype(BF16), (base * slope).astype(BF16)], axis=-1)
        vt = kv[:, c + h * LANES:c + (h + 1) * LANES].T.astype(BF16)
        v_out[0, h, 0] = jnp.concatenate([vt, ones], axis=0)


def _queries(x, ng_ref, w_ref, qg_ref, q_out):
    tm, c = x.shape
    q = _mm(_rms(x, ng_ref[...], NORM_EPS), w_ref[...])
    first = _half_mask((tm, LANES))
    qg = qg_ref[...] * (HEAD ** -0.5 * LOG2E)
    for h in range(c // LANES):
        qh = q[:, h * LANES:(h + 1) * LANES]
        ms = _pair_sum(qh * qh, first) * (1.0 / HEAD)
        q_out[0, h] = (qh * lax.rsqrt(ms + NORM_EPS) * qg).astype(BF16)


def _block_kernel(*refs, pre, post):
    refs = list(refs)
    x = refs.pop(0)[...]
    if pre is not None:
        y_ref = refs.pop(0)
        y = jnp.concatenate([y_ref[0, p] for p in range(y_ref.shape[1])], axis=-1)
        if pre == "gated":
            y = y * refs.pop(0)[...]
        x = x + jnp.dot(y.astype(BF16), refs.pop(0)[...], preferred_element_type=F32)
    g_ref, win_ref, wout_ref = refs[:3]
    refs = refs[3:]
    post_refs = [refs.pop(0) for _ in range(3)] if post is not None else []
    o_ref = refs.pop(0)
    h_sc = refs.pop()
    x = _ffn_value(x, g_ref, win_ref, wout_ref, h_sc)
    o_ref[...] = x
    if post == "kv":
        _keys_values(x, *post_refs, *refs)
    elif post == "q":
        _queries(x, *post_refs, *refs)


def _block(x, batch, g, w_in, w_out, which, *, pre=None, post=None):
    m, c = x.shape
    t = m // batch
    d_ff = w_out.shape[-2]
    tm = min(ATT if post is not None and post[0] == "kv" else 2 * ATT, t)
    bps = t // tm
    nh = c // LANES
    row_spec = pl.BlockSpec((tm, c), lambda i: (i, 0))
    head_idx = lambda i: (i // bps, 0, i % bps, 0)
    args, in_specs = [x], [row_spec]
    pre_kind = None
    if pre is not None:
        y, gate, w_o = pre
        pre_kind = "plain" if gate is None else "gated"
        args.append(y)
        in_specs.append(pl.BlockSpec((1, y.shape[1], tm, LANES), head_idx))
        if gate is not None:
            args.append(gate)
            in_specs.append(row_spec)
        args.append(w_o.astype(BF16))
        in_specs.append(_const_spec((c, c)))
    args += [g.reshape(1, c), w_in, w_out]
    in_specs += [_const_spec((1, c)), _const_spec((c, 2 * d_ff), which), _const_spec((d_ff, c), which)]
    out_specs, out_shape = [row_spec], [jax.ShapeDtypeStruct((m, c), F32)]
    post_kind = None
    if post is not None:
        post_kind, ng, w, head_norm = post
        args += [ng.reshape(1, c), w.astype(BF16), jnp.concatenate([head_norm, head_norm]).reshape(1, LANES)]
        in_specs += [_const_spec((1, c)), _const_spec(w.shape), _const_spec((1, LANES))]
        if post_kind == "kv":
            out_specs += [pl.BlockSpec((1, nh, tm, 2 * LANES), head_idx),
                          pl.BlockSpec((1, nh, 1, VT_ROWS, tm), lambda i: (i // bps, 0, i % bps, 0, 0))]
            out_shape += [jax.ShapeDtypeStruct((batch, nh, t, 2 * LANES), BF16),
                          jax.ShapeDtypeStruct((batch, nh, t // tm, VT_ROWS, tm), BF16)]
        else:
            out_specs.append(pl.BlockSpec((1, nh, tm, LANES), head_idx))
            out_shape.append(jax.ShapeDtypeStruct((batch, nh, t, LANES), BF16))
    return pl.pallas_call(
        functools.partial(_block_kernel, pre=pre_kind, post=post_kind),
        grid=(m // tm,),
        in_specs=in_specs,
        out_specs=out_specs,
        out_shape=out_shape,
        scratch_shapes=[pltpu.VMEM((tm, d_ff), BF16)],
        compiler_params=_cparams(("parallel",)),
        name="block_" + (pre_kind or "x") + "_" + (post_kind or "x"),
    )(*args)


def _rwkv_pre_kernel(x_ref, xp_ref, ng_ref, mu_ref, wrkv_ref, w0_ref, w1_ref, w2_ref, a0_ref, a1_ref,
                     a2_ref, g1_ref, g2_ref, kk_ref, ka_ref,
                     r_out, lw_out, k_out, v_out, kk_out, a_out, g_out, *, blocks_per_seq):
    tm, c = x_ref.shape
    ng = ng_ref[...]
    h = _rms(x_ref[...], ng, NORM_EPS)
    prev = _rms(xp_ref[SUBLANES - 1:SUBLANES, :], ng, NORM_EPS)
    prev = jnp.where(pl.program_id(0) % blocks_per_seq == 0, 0.0, prev)
    row = lax.broadcasted_iota(jnp.int32, (tm, c), 0)
    h_prev = jnp.where(row == 0, prev, pltpu.roll(h, 1, 0))
    dx = h_prev - h

    def mix(i):
        return h + dx * mu_ref[i:i + 1, :]

    r = _mm(mix(0), wrkv_ref[0])
    k = _mm(mix(1), wrkv_ref[1])
    v = _mm(mix(2), wrkv_ref[2])
    z = w0_ref[...] + _mm(jnp.tanh(_mm(mix(3), w1_ref[...])), w2_ref[...])
    lw = -math.exp(-0.5) * jax.nn.sigmoid(z)
    a = jax.nn.sigmoid(a0_ref[...] + _mm(_mm(mix(4), a1_ref[...]), a2_ref[...]))
    g_out[...] = _mm(jax.nn.sigmoid(_mm(mix(5), g1_ref[...])), g2_ref[...])
    kk = k * kk_ref[...]
    k = k * (1.0 + (a - 1.0) * ka_ref[...])
    first = _half_mask((tm, PAIR))
    for p in range(c // PAIR):
        sl = slice(p * PAIR, (p + 1) * PAIR)
        kkp = kk[:, sl]
        ss = _pair_sum(kkp * kkp, first)
        r_out[0, p] = r[:, sl]
        lw_out[0, p] = lw[:, sl]
        k_out[0, p] = k[:, sl]
        v_out[0, p] = v[:, sl]
        kk_out[0, p] = kkp * lax.rsqrt(jnp.maximum(ss, 1e-24))
        a_out[0, p] = a[:, sl]


def _rwkv_pre(x, batch, ng, mu, w_rkv, w0, w1, w2, a0, a1, a2, g1, g2, k_k, k_a, *, tm=512):
    m, c = x.shape
    t = m // batch
    tm = min(tm, t)
    bps = t // tm
    npair = c // PAIR
    row = lambda a: a.reshape(1, c)
    pair_spec = pl.BlockSpec((1, npair, tm, PAIR), lambda i: (i // bps, 0, i % bps, 0))
    pair_shape = jax.ShapeDtypeStruct((batch, npair, t, PAIR), F32)
    ws = [w_rkv.astype(BF16), row(w0), w1.astype(BF16), w2.astype(BF16), row(a0), a1.astype(BF16),
          a2.astype(BF16), g1.astype(BF16), g2.astype(BF16), row(k_k), row(k_a)]
    return pl.pallas_call(
        functools.partial(_rwkv_pre_kernel, blocks_per_seq=bps),
        grid=(m // tm,),
        in_specs=[pl.BlockSpec((tm, c), lambda i: (i, 0)),
                  pl.BlockSpec((SUBLANES, c), lambda i: (jnp.maximum(i * (tm // SUBLANES) - 1, 0), 0)),
                  _const_spec((1, c)), _const_spec(mu.shape)] + [_const_spec(w.shape) for w in ws],
        out_specs=[pair_spec] * 6 + [pl.BlockSpec((tm, c), lambda i: (i, 0))],
        out_shape=[pair_shape] * 6 + [jax.ShapeDtypeStruct((m, c), F32)],
        compiler_params=_cparams(("parallel",)),
        name="rwkv_pre",
    )(x, x, row(ng), mu, *ws)


def _stack(x, first):
    return jnp.concatenate([jnp.where(first, x, 0.0), jnp.where(first, 0.0, x)], axis=0)


def _wkv_kernel(r_ref, lw_ref, k_ref, v_ref, kk_ref, a_ref, rk_ref, lng_ref, lnb_ref, y_ref,
                s_sc, carry_sc, bv_sc, *, nblk, blocks_per_seq):
    lb = r_ref.shape[2]
    nc = lb // CHUNK
    l2 = 2 * CHUNK
    i = pl.program_id(0)

    first = _half_mask((CHUNK, PAIR))
    ri = lax.broadcasted_iota(jnp.int32, (l2, l2), 0)
    ci = lax.broadcasted_iota(jnp.int32, (l2, l2), 1)
    same = (ri // CHUNK) == (ci // CHUNK)
    strict = same & (ri % CHUNK > ci % CHUNK)
    incl = same & (ri % CHUNK >= ci % CHUNK)
    eye = ri == ci
    ident = jnp.where(eye, 1.0, 0.0)
    rk = rk_ref[0]
    lng = lng_ref[0]
    lnb = lnb_ref[0]
    chs = range(nc)

    def prepare_stages():
        w = {}

        def prep():
            tri = jnp.where(lax.broadcasted_iota(jnp.int32, (CHUNK, CHUNK), 1)
                            <= lax.broadcasted_iota(jnp.int32, (CHUNK, CHUNK), 0), 1.0, 0.0).astype(BF16)
            lw_wide = jnp.concatenate([lw_ref[0, 0, ch * CHUNK:(ch + 1) * CHUNK, :] for ch in chs], axis=1)
            lw_hi = lw_wide.astype(BF16)
            lw_lo = (lw_wide - lw_hi.astype(F32)).astype(BF16)
            c_wide = (jnp.dot(tri, lw_hi, preferred_element_type=F32)
                      + jnp.dot(tri, lw_lo, preferred_element_type=F32))
            for ch in chs:
                sl = slice(ch * CHUNK, (ch + 1) * CHUNK)
                r = r_ref[0, 0, sl, :]
                k = k_ref[0, 0, sl, :]
                v = v_ref[0, 0, sl, :]
                kk = kk_ref[0, 0, sl, :]
                c = c_wide[:, ch * PAIR:(ch + 1) * PAIR]
                c_last = c[CHUNK - 1:CHUNK, :]
                e_neg = jnp.exp(-c)
                e_last = jnp.exp(c_last - c)
                b = kk * a_ref[0, 0, sl, :]
                rt = _stack(r * jnp.exp(c), first)
                at = _stack(-kk * jnp.exp(c - lw_ref[0, 0, sl, :]), first)
                w[ch] = dict(
                    rt=rt, at=at, vs=_stack(v, first), g_last=jnp.exp(c_last),
                    bonus=_pair_sum(r * k * rk, first) * v,
                    ar=jnp.concatenate([at, rt], axis=0),
                    bkt=jnp.concatenate([_stack(b * e_neg, first), _stack(k * e_neg, first)], axis=0),
                    bk_last=jnp.concatenate([_stack(b * e_last, first), _stack(k * e_last, first)], axis=0))

        def couplings():
            for ch in chs:
                d = w[ch]
                g = _mm_nt(d["ar"], d["bkt"])
                d["n"] = jnp.where(strict, g[:l2, :l2], 0.0)
                d["a_ak"] = jnp.where(strict, g[:l2, l2:], 0.0)
                d["a_rbk"] = jnp.concatenate([jnp.where(incl, g[l2:, :l2], 0.0),
                                              jnp.where(incl, g[l2:, l2:], 0.0)], axis=1)

        def start_inverse():
            for ch in chs:
                d = w[ch]
                d["aakv"] = _mm(d["a_ak"], d["vs"])
                d["tinv"] = ident + d["n"]
                d["pw"] = d["n"]

        def square():
            for ch in chs:
                w[ch]["pw"] = _mm(w[ch]["pw"], w[ch]["pw"])

        def extend():
            for ch in chs:
                w[ch]["tinv"] = w[ch]["tinv"] + _mm(w[ch]["tinv"], w[ch]["pw"])

        def solve():
            for ch in chs:
                d = w[ch]
                x = _mm(d["tinv"], jnp.concatenate([d["at"], d["aakv"]], axis=1))
                d["z"] = jnp.concatenate([x, jnp.concatenate([jnp.zeros_like(d["vs"]), d["vs"]], axis=1)],
                                         axis=0)

        def finish():
            for ch in chs:
                d = w[ch]
                m1 = _mm(d["a_rbk"], d["z"])
                m2 = _mm_tn(d["z"], d["bk_last"])
                carry_sc[ch, 0] = d["rt"] + m1[:, :PAIR]
                carry_sc[ch, 1] = m1[:, PAIR:]
                carry_sc[ch, 2] = jnp.where(eye, d["g_last"], 0.0) + m2[:PAIR]
                carry_sc[ch, 3] = m2[PAIR:]
                bv_sc[ch] = d["bonus"]

        rounds = int(math.log2(CHUNK)) - 1
        return [prep, couplings, start_inverse] + [square, extend] * rounds + [solve, finish]

    def advance(ch, s):
        ys = _mm_nt(carry_sc[ch, 0], s) + carry_sc[ch, 1]
        s = _mm(s, carry_sc[ch, 2]) + carry_sc[ch, 3]
        y = ys[:CHUNK] + ys[CHUNK:]
        mean = _pair_sum(y, first) * (1.0 / HEAD)
        d = y - mean
        var = _pair_sum(d * d, first) * (1.0 / HEAD)
        y_ref[0, 0, ch * CHUNK:(ch + 1) * CHUNK, :] = d * lax.rsqrt(var + RWKV_LN_EPS) * lng + lnb + bv_sc[ch]
        return s

    def carried_state():
        return jnp.where((i - 1) % blocks_per_seq == 0, 0.0, s_sc[...])

    @pl.when(i == 0)
    def _():
        for stage in prepare_stages():
            stage()

    @pl.when((i > 0) & (i < nblk))
    def _():
        s = carried_state()
        stages = prepare_stages()
        slots = len(stages) - 1
        for n_stage, stage in enumerate(stages):
            stage()
            for ch in chs:
                if ch * slots // nc == n_stage:
                    s = advance(ch, s)
        s_sc[...] = s

    @pl.when(i == nblk)
    def _():
        s = carried_state()
        for ch in chs:
            s = advance(ch, s)
        s_sc[...] = s


def _wkv_scan(r, lw, k, v, kk, a, r_k, ln_g, ln_b, *, lb=1024):
    batch, npair, t, _ = r.shape
    lb = min(lb, t)
    bps = t // lb
    nblk = batch * npair * bps
    nc = lb // CHUNK

    def block_of(n):
        seq = n // bps
        return seq // npair, seq % npair, n % bps, 0

    prepared = lambda i: jnp.minimum(i, nblk - 1)
    advanced = lambda i: jnp.maximum(i - 1, 0)
    in_spec = pl.BlockSpec((1, 1, lb, PAIR), lambda i: block_of(prepared(i)))
    out_spec = pl.BlockSpec((1, 1, lb, PAIR), lambda i: block_of(advanced(i)))
    prep_par = pl.BlockSpec((1, 1, PAIR), lambda i: (block_of(prepared(i))[1], 0, 0))
    adv_par = pl.BlockSpec((1, 1, PAIR), lambda i: (block_of(advanced(i))[1], 0, 0))
    par = lambda x: x.reshape(npair, 1, PAIR)
    return pl.pallas_call(
        functools.partial(_wkv_kernel, nblk=nblk, blocks_per_seq=bps),
        grid=(nblk + 1,),
        in_specs=[in_spec] * 6 + [prep_par, adv_par, adv_par],
        out_specs=out_spec,
        out_shape=jax.ShapeDtypeStruct(r.shape, F32),
        scratch_shapes=[pltpu.VMEM((PAIR, PAIR), F32), pltpu.VMEM((nc, 4, PAIR, PAIR), F32),
                        pltpu.VMEM((nc, CHUNK, PAIR), F32)],
        compiler_params=_cparams(("arbitrary",)),
        name="wkv_scan",
    )(r, lw, k, v, kk, a, par(r_k), par(ln_g), par(ln_b))


def _sublane_all(op, x):
    shift = SUBLANES // 2
    while shift:
        x = op(x, pltpu.roll(x, shift, 0))
        shift //= 2
    return x


def _diff_attn_kernel(q_ref, k_ref, vt_ref, lam_ref, sg_ref, o_ref, m_sc, acc_sc, sa_sc, sb_sc, qs_sc, *,
                      lam_init, cg, n_heads, unroll):
    bq = vt_ref.shape[4]
    dv = LANES
    rows = vt_ref.shape[3]
    ncg = 2 * bq // cg
    n_pairs = q_ref.shape[2] // (2 * bq)
    head = (pl.program_id(1) + 1).astype(F32)
    block_bias = jnp.exp2(jnp.full((SUBLANES, cg), -8.0 / n_heads, F32) * head) * (LOG2E * bq)
    first_blk = list(range(ncg))
    second_blk = list(range(ncg, 2 * ncg))
    both = first_blk + second_blk
    q0 = [(g * cg) % bq for g in both]
    bufs = (sa_sc, sb_sc)

    def stack_queries(i):
        first = _half_mask((bq, LANES))
        lane = lax.broadcasted_iota(jnp.int32, (bq, LANES), 1)
        feat = jnp.zeros((bq, LANES), F32)
        for n, part in enumerate(LOG2E_PARTS):
            feat = jnp.where(lane // 2 == n, part, feat)
        feat = feat.astype(BF16)
        for blk in range(2):
            q = q_ref[0, 0, pl.ds(pl.multiple_of((2 * i + blk) * bq, bq), bq), :]
            zero = jnp.zeros_like(q)
            qs_sc[2 * blk * bq:(2 * blk + 1) * bq] = jnp.concatenate([jnp.where(first, q, zero), feat], axis=1)
            qs_sc[(2 * blk + 1) * bq:(2 * blk + 2) * bq] = jnp.concatenate([jnp.where(first, zero, q), feat], axis=1)

    def scores(j, buf, gs):
        kt = k_ref[0, 0, pl.ds(pl.multiple_of(j * bq, bq), bq), :]
        for g in gs:
            buf[g] = lax.dot_general(kt, qs_sc[g * cg:(g + 1) * cg], (((1,), (1,)), ((), ())),
                                     preferred_element_type=F32)

    def pair(i, carry):
        m_sc[...] = jnp.full_like(m_sc, NEG_BIG)
        acc_sc[...] = jnp.zeros_like(acc_sc)

        def tile(j, buf, full, diag):
            p, alpha, nk = {}, {}, {}
            for g in list(full) + list(diag):
                on_diag = g in diag
                nk[g] = min(bq, q0[g] + cg) if on_diag else bq
                sg = buf[g, :nk[g], :]
                if on_diag:
                    key = lax.broadcasted_iota(jnp.int32, sg.shape, 0)
                    qry = lax.broadcasted_iota(jnp.int32, sg.shape, 1) + q0[g]
                    sg = jnp.where(key <= qry, sg, NEG_BIG)
                    offset = 0.0
                else:
                    offset = block_bias * lax.convert_element_type(j - (2 * i + g // ncg), F32)
                sg = sg.reshape(nk[g] // SUBLANES, SUBLANES, cg)
                m_prev = m_sc[g]
                m_new = jnp.maximum(m_prev, _sublane_all(jnp.maximum, jnp.max(sg, axis=0)) + offset)
                p[g] = jnp.exp2(sg - (m_new - offset)[None]).reshape(nk[g], cg).astype(BF16)
                alpha[g] = jnp.exp2(m_prev - m_new)
                m_sc[g] = m_new
            pv = {g: jnp.dot(vt_ref[0, 0, j, :, :nk[g]], p[g], preferred_element_type=F32) for g in p}
            for g in p:
                acc_sc[g] = (alpha[g][None] * acc_sc[g].reshape(rows // SUBLANES, SUBLANES, cg)
                             + pv[g].reshape(rows // SUBLANES, SUBLANES, cg)).reshape(rows, cg)

        def run(first_block, count, with_tail):
            for n in range(count):
                scores(first_block + n + 1, bufs[(n + 1) % 2], both)
                tile(first_block + n, bufs[n % 2], both, [])
            if with_tail:
                scores(2 * i + 1, bufs[(count + 1) % 2], second_blk)
                tile(2 * i, bufs[count % 2], second_blk, first_blk)
                tile(2 * i + 1, bufs[(count + 1) % 2], [], second_blk)

        def body(jj, carry):
            run(unroll * jj, unroll, False)
            return carry

        n_full = 2 * i
        lax.fori_loop(0, n_full // unroll, body, 0)
        for rem in range(0, unroll, 2):
            @pl.when(n_full % unroll == rem)
            def _():
                run(n_full - rem, rem, True)

        stack_queries(jnp.minimum(i + 1, n_pairs - 1))
        scores(0, sa_sc, both)

        lam = lam_ref[...]
        lam_full = (jnp.exp(jnp.sum(lam[0:1] * lam[1:2], axis=-1, keepdims=True))
                    - jnp.exp(jnp.sum(lam[2:3] * lam[3:4], axis=-1, keepdims=True)) + lam_init)
        for blk, gs in enumerate((first_blk, second_blk)):
            o_t = []
            for g in gs:
                l = acc_sc[g, dv:dv + SUBLANES, :]
                o_t.append((acc_sc[g, :dv, :].reshape(dv // SUBLANES, SUBLANES, cg) / l[None]).reshape(dv, cg))
            half = ncg // 2
            o1 = jnp.concatenate(o_t[:half], axis=1)
            o2 = jnp.concatenate(o_t[half:], axis=1)
            o = (o1 - lam_full * o2).T
            o_ref[0, 0, pl.ds(pl.multiple_of((2 * i + blk) * bq, bq), bq), :] = (
                _rms(o, sg_ref[...], SUBLN_EPS) * (1.0 - lam_init)).astype(BF16)
        return carry

    stack_queries(0)
    scores(0, sa_sc, both)
    lax.fori_loop(0, n_pairs, pair, 0)


def _diff_attn(q, k, vt, lam, subln, lam_init, *, unroll=8):
    assert unroll % 2 == 0
    batch, nh, t, _ = q.shape
    rows, bq = vt.shape[-2:]
    assert (t // bq) % 2 == 0
    cg = min(MXU_WIDTH, bq)
    ncg = 4 * bq // cg
    return pl.pallas_call(
        functools.partial(_diff_attn_kernel, lam_init=lam_init, cg=cg, n_heads=nh, unroll=unroll),
        grid=(batch, nh),
        in_specs=[pl.BlockSpec((1, 1, t, LANES), lambda b, h: (b, h, 0, 0)),
                  pl.BlockSpec((1, 1, t, 2 * LANES), lambda b, h: (b, h, 0, 0)),
                  pl.BlockSpec((1, 1, t // bq, rows, bq), lambda b, h: (b, h, 0, 0, 0)),
                  _const_spec(lam.shape), _const_spec((1, LANES))],
        out_specs=pl.BlockSpec((1, 1, t, LANES), lambda b, h: (b, h, 0, 0)),
        out_shape=jax.ShapeDtypeStruct((batch, nh, t, LANES), BF16),
        scratch_shapes=[pltpu.VMEM((ncg, SUBLANES, cg), F32), pltpu.VMEM((ncg, rows, cg), F32),
                        pltpu.VMEM((ncg, bq, cg), F32), pltpu.VMEM((ncg, bq, cg), F32),
                        pltpu.VMEM((4 * bq, 2 * LANES), BF16)],
        compiler_params=_cparams(("parallel", "parallel")),
        name="diff_attn",
    )(q, k, vt, lam, subln.reshape(1, LANES))


def kernel(x, ffn_norm, ffn_w_in, ffn_w_out, mix_norm, rwkv_mu, rwkv_w_rkv, rwkv_w0, rwkv_w1, rwkv_w2, rwkv_a0, rwkv_a1, rwkv_a2, rwkv_g1, rwkv_g2, rwkv_k_k, rwkv_k_a, rwkv_r_k, rwkv_ln_g, rwkv_ln_b, rwkv_w_o, kv_norm, w_kv, k_norm, diff_w_q, diff_q_norm, diff_lambda, diff_subln, diff_w_o):
    batch, t, c = x.shape
    x = x.reshape(batch * t, c)
    w_in, w_out = ffn_w_in.astype(BF16), ffn_w_out.astype(BF16)

    x, = _block(x, batch, ffn_norm[0, 0], w_in, w_out, (0, 0))
    r, lw, k, v, kk, a, g = _rwkv_pre(x, batch, mix_norm[0], rwkv_mu[0], rwkv_w_rkv[0], rwkv_w0[0], rwkv_w1[0],
                                      rwkv_w2[0], rwkv_a0[0], rwkv_a1[0], rwkv_a2[0], rwkv_g1[0], rwkv_g2[0],
                                      rwkv_k_k[0], rwkv_k_a[0])
    y = _wkv_scan(r, lw, k, v, kk, a, rwkv_r_k[0], rwkv_ln_g[0], rwkv_ln_b[0])
    x, k_sh, v_sh = _block(x, batch, ffn_norm[0, 1], w_in, w_out, (0, 1),
                           pre=(y, g, rwkv_w_o[0]), post=("kv", kv_norm, w_kv, k_norm))

    x, q = _block(x, batch, ffn_norm[1, 0], w_in, w_out, (1, 0),
                  post=("q", mix_norm[1], diff_w_q[0], diff_q_norm[0]))
    lam_init = 0.8 - 0.6 * math.exp(-0.3 * 1)
    o = _diff_attn(q, k_sh, v_sh, diff_lambda[0], diff_subln[0], lam_init)
    x, = _block(x, batch, ffn_norm[1, 1], w_in, w_out, (1, 1), pre=(o, None, diff_w_o[0]))
    return x.reshape(batch, t, c)
```

```python
import functools
import math

import jax
import jax.numpy as jnp
from jax import lax
from jax.experimental import pallas as pl
from jax.experimental.pallas import tpu as pltpu

F32 = jnp.float32
BF16 = jnp.bfloat16

LANES = 128
SUBLANES = 8
MXU_WIDTH = 256
HEAD = 64
PAIR = 2 * HEAD
CHUNK = 64
ATT = 512
FFN_TF = MXU_WIDTH
NORM_EPS = 1e-6
SUBLN_EPS = 1e-5
RWKV_LN_EPS = 64e-5
VMEM_LIMIT = 56 * 1024 * 1024
NEG_BIG = -1e30
BF16_ROWS = 16
VT_ROWS = LANES + BF16_ROWS


def _bf16_parts(x, n):
    parts = []
    for _ in range(n):
        m, e = math.frexp(x)
        p = math.ldexp(round(m * 256.0) / 256.0, e)
        parts.append(p)
        x -= p
    return tuple(parts)


LOG2E = math.log2(math.e)
LOG2E_PARTS = _bf16_parts(LOG2E, 3)


def _cparams(sem, fuse_inputs=None):
    return pltpu.CompilerParams(dimension_semantics=sem, vmem_limit_bytes=VMEM_LIMIT,
                                allow_input_fusion=fuse_inputs)


def _const_spec(shape, lead=()):
    nd = len(shape)
    return pl.BlockSpec((None,) * len(lead) + tuple(shape), lambda *_: tuple(lead) + (0,) * nd,
                        pipeline_mode=pl.Buffered(1))


def _mm(a, b):
    return jnp.dot(a.astype(BF16), b.astype(BF16), preferred_element_type=F32)


def _mm_nt(a, b):
    return lax.dot_general(a.astype(BF16), b.astype(BF16), (((1,), (1,)), ((), ())),
                           preferred_element_type=F32)


def _mm_tn(a, b):
    return lax.dot_general(a.astype(BF16), b.astype(BF16), (((0,), (0,)), ((), ())),
                           preferred_element_type=F32)


def _rms(x, g, eps):
    return x * lax.rsqrt(jnp.mean(x * x, axis=-1, keepdims=True) + eps) * g


def _half_mask(shape):
    return lax.broadcasted_iota(jnp.int32, shape, len(shape) - 1) % PAIR < HEAD


def _pair_sum(x, first):
    s1 = jnp.sum(jnp.where(first, x, 0.0), axis=-1, keepdims=True)
    s2 = jnp.sum(jnp.where(first, 0.0, x), axis=-1, keepdims=True)
    return jnp.where(first, s1, s2)


def _ffn_value(x, g_ref, win_ref, wout_ref, h_sc):
    d_ff = wout_ref.shape[0]
    xn = _rms(x, g_ref[...], NORM_EPS).astype(BF16)
    for f0 in range(0, d_ff, FFN_TF):
        gate = jnp.dot(xn, win_ref[:, f0:f0 + FFN_TF], preferred_element_type=F32)
        up = jnp.dot(xn, win_ref[:, d_ff + f0:d_ff + f0 + FFN_TF], preferred_element_type=F32)
        h_sc[:, f0:f0 + FFN_TF] = (gate * jax.nn.sigmoid(gate) * up).astype(BF16)
    return x + 0.5 * jnp.dot(h_sc[...], wout_ref[...], preferred_element_type=F32)


def _keys_values(x, ng_ref, w_ref, kg_ref, k_out, v_out):
    tm, c = x.shape
    n_heads = c // LANES
    kv = _mm(_rms(x, ng_ref[...], NORM_EPS), w_ref[...])
    first = _half_mask((tm, LANES))
    kg = kg_ref[...]
    lane = lax.broadcasted_iota(jnp.int32, (tm, LANES), 1)
    pos = lax.broadcasted_iota(jnp.int32, (tm, LANES), 0)
    base = jnp.where(lane >= 2 * len(LOG2E_PARTS), 0.0,
                     jnp.where(lane % 2 == 0, (pos // HEAD * HEAD).astype(F32), (pos % HEAD).astype(F32)))
    ones = jnp.ones((VT_ROWS - LANES, tm), BF16)
    for h in range(n_heads):
        kh = kv[:, h * LANES:(h + 1) * LANES]
        ms = _pair_sum(kh * kh, first) * (1.0 / HEAD)
        kn = kh * lax.rsqrt(ms + NORM_EPS) * kg
        slope = 2.0 ** (-8.0 * (h + 1) / n_heads)
        k_out[0, h] = jnp.concatenate([kn.astype(BF16), (base * slope).astype(BF16)], axis=-1)
        vt = kv[:, c + h * LANES:c + (h + 1) * LANES].T.astype(BF16)
        v_out[0, h, 0] = jnp.concatenate([vt, ones], axis=0)


def _queries(x, ng_ref, w_ref, qg_ref, q_out):
    tm, c = x.shape
    q = _mm(_rms(x, ng_ref[...], NORM_EPS), w_ref[...])
    first = _half_mask((tm, LANES))
    qg = qg_ref[...] * (HEAD ** -0.5 * LOG2E)
    for h in range(c // LANES):
        qh = q[:, h * LANES:(h + 1) * LANES]
        ms = _pair_sum(qh * qh, first) * (1.0 / HEAD)
        q_out[0, h] = (qh * lax.rsqrt(ms + NORM_EPS) * qg).astype(BF16)


def _block_kernel(*refs, pre, post):
    refs = list(refs)
    x = refs.pop(0)[...]
    if pre is not None:
        y_ref = refs.pop(0)
        y = jnp.concatenate([y_ref[0, p] for p in range(y_ref.shape[1])], axis=-1)
        if pre == "gated":
            y = y * refs.pop(0)[...]
        x = x + jnp.dot(y.astype(BF16), refs.pop(0)[...], preferred_element_type=F32)
    g_ref, win_ref, wout_ref = refs[:3]
    refs = refs[3:]
    post_refs = [refs.pop(0) for _ in range(3)] if post is not None else []
    o_ref = refs.pop(0)
    h_sc = refs.pop()
    x = _ffn_value(x, g_ref, win_ref, wout_ref, h_sc)
    o_ref[...] = x
    if post == "kv":
        _keys_values(x, *post_refs, *refs)
    elif post == "q":
        _queries(x, *post_refs, *refs)


def _block(x, batch, g, w_in, w_out, which, *, pre=None, post=None):
    m, c = x.shape
    t = m // batch
    d_ff = w_out.shape[-2]
    tm = min(ATT if post is not None and post[0] == "kv" else 2 * ATT, t)
    bps = t // tm
    nh = c // LANES
    row_spec = pl.BlockSpec((tm, c), lambda i: (i, 0))
    head_idx = lambda i: (i // bps, 0, i % bps, 0)
    args, in_specs = [x], [row_spec]
    pre_kind = None
    if pre is not None:
        y, gate, w_o = pre
        pre_kind = "plain" if gate is None else "gated"
        args.append(y)
        in_specs.append(pl.BlockSpec((1, y.shape[1], tm, LANES), head_idx))
        if gate is not None:
            args.append(gate)
            in_specs.append(row_spec)
        args.append(w_o.astype(BF16))
        in_specs.append(_const_spec((c, c)))
    args += [g.reshape(1, c), w_in, w_out]
    ffn_weight_args = (len(args) - 2, len(args) - 1)
    in_specs += [_const_spec((1, c)), _const_spec((c, 2 * d_ff), which), _const_spec((d_ff, c), which)]
    out_specs, out_shape = [row_spec], [jax.ShapeDtypeStruct((m, c), F32)]
    post_kind = None
    if post is not None:
        post_kind, ng, w, head_norm = post
        args += [ng.reshape(1, c), w.astype(BF16), jnp.concatenate([head_norm, head_norm]).reshape(1, LANES)]
        in_specs += [_const_spec((1, c)), _const_spec(w.shape), _const_spec((1, LANES))]
        if post_kind == "kv":
            out_specs += [pl.BlockSpec((1, nh, tm, 2 * LANES), head_idx),
                          pl.BlockSpec((1, nh, 1, VT_ROWS, tm), lambda i: (i // bps, 0, i % bps, 0, 0))]
            out_shape += [jax.ShapeDtypeStruct((batch, nh, t, 2 * LANES), BF16),
                          jax.ShapeDtypeStruct((batch, nh, t // tm, VT_ROWS, tm), BF16)]
        else:
            out_specs.append(pl.BlockSpec((1, nh, tm, LANES), head_idx))
            out_shape.append(jax.ShapeDtypeStruct((batch, nh, t, LANES), BF16))
    return pl.pallas_call(
        functools.partial(_block_kernel, pre=pre_kind, post=post_kind),
        grid=(m // tm,),
        in_specs=in_specs,
        out_specs=out_specs,
        out_shape=out_shape,
        scratch_shapes=[pltpu.VMEM((tm, d_ff), BF16)],
        compiler_params=_cparams(("parallel",), [n in ffn_weight_args for n in range(len(args))]),
        name="block_" + (pre_kind or "x") + "_" + (post_kind or "x"),
    )(*args)


def _rwkv_pre_kernel(x_ref, xp_ref, ng_ref, mu_ref, wrkv_ref, w0_ref, w1_ref, w2_ref, a0_ref, a1_ref,
                     a2_ref, g1_ref, g2_ref, kk_ref, ka_ref,
                     r_out, lw_out, k_out, v_out, kk_out, a_out, g_out, *, blocks_per_seq):
    tm, c = x_ref.shape
    ng = ng_ref[...]
    h = _rms(x_ref[...], ng, NORM_EPS)
    prev = _rms(xp_ref[SUBLANES - 1:SUBLANES, :], ng, NORM_EPS)
    prev = jnp.where(pl.program_id(0) % blocks_per_seq == 0, 0.0, prev)
    row = lax.broadcasted_iota(jnp.int32, (tm, c), 0)
    h_prev = jnp.where(row == 0, prev, pltpu.roll(h, 1, 0))
    dx = h_prev - h

    def mix(i):
        return h + dx * mu_ref[i:i + 1, :]

    r = _mm(mix(0), wrkv_ref[0])
    k = _mm(mix(1), wrkv_ref[1])
    v = _mm(mix(2), wrkv_ref[2])
    z = w0_ref[...] + _mm(jnp.tanh(_mm(mix(3), w1_ref[...])), w2_ref[...])
    lw = -math.exp(-0.5) * jax.nn.sigmoid(z)
    a = jax.nn.sigmoid(a0_ref[...] + _mm(_mm(mix(4), a1_ref[...]), a2_ref[...]))
    g_out[...] = _mm(jax.nn.sigmoid(_mm(mix(5), g1_ref[...])), g2_ref[...])
    kk = k * kk_ref[...]
    k = k * (1.0 + (a - 1.0) * ka_ref[...])
    first = _half_mask((tm, PAIR))
    for p in range(c // PAIR):
        sl = slice(p * PAIR, (p + 1) * PAIR)
        kkp = kk[:, sl]
        ss = _pair_sum(kkp * kkp, first)
        r_out[0, p] = r[:, sl]
        lw_out[0, p] = lw[:, sl]
        k_out[0, p] = k[:, sl]
        v_out[0, p] = v[:, sl]
        kk_out[0, p] = kkp * lax.rsqrt(jnp.maximum(ss, 1e-24))
        a_out[0, p] = a[:, sl]


def _rwkv_pre(x, batch, ng, mu, w_rkv, w0, w1, w2, a0, a1, a2, g1, g2, k_k, k_a, *, tm=512):
    m, c = x.shape
    t = m // batch
    tm = min(tm, t)
    bps = t // tm
    npair = c // PAIR
    row = lambda a: a.reshape(1, c)
    pair_spec = pl.BlockSpec((1, npair, tm, PAIR), lambda i: (i // bps, 0, i % bps, 0))
    pair_shape = jax.ShapeDtypeStruct((batch, npair, t, PAIR), F32)
    ws = [w_rkv.astype(BF16), row(w0), w1.astype(BF16), w2.astype(BF16), row(a0), a1.astype(BF16),
          a2.astype(BF16), g1.astype(BF16), g2.astype(BF16), row(k_k), row(k_a)]
    return pl.pallas_call(
        functools.partial(_rwkv_pre_kernel, blocks_per_seq=bps),
        grid=(m // tm,),
        in_specs=[pl.BlockSpec((tm, c), lambda i: (i, 0)),
                  pl.BlockSpec((SUBLANES, c), lambda i: (jnp.maximum(i * (tm // SUBLANES) - 1, 0), 0)),
                  _const_spec((1, c)), _const_spec(mu.shape)] + [_const_spec(w.shape) for w in ws],
        out_specs=[pair_spec] * 6 + [pl.BlockSpec((tm, c), lambda i: (i, 0))],
        out_shape=[pair_shape] * 6 + [jax.ShapeDtypeStruct((m, c), F32)],
        compiler_params=_cparams(("parallel",)),
        name="rwkv_pre",
    )(x, x, row(ng), mu, *ws)


def _stack(x, first):
    return jnp.concatenate([jnp.where(first, x, 0.0), jnp.where(first, 0.0, x)], axis=0)


def _wkv_kernel(r_ref, lw_ref, k_ref, v_ref, kk_ref, a_ref, rk_ref, lng_ref, lnb_ref, y_ref,
                s_sc, carry_sc, bv_sc, *, nblk, blocks_per_seq):
    lb = r_ref.shape[2]
    nc = lb // CHUNK
    l2 = 2 * CHUNK
    i = pl.program_id(0)

    first = _half_mask((CHUNK, PAIR))
    ri = lax.broadcasted_iota(jnp.int32, (l2, l2), 0)
    ci = lax.broadcasted_iota(jnp.int32, (l2, l2), 1)
    same = (ri // CHUNK) == (ci // CHUNK)
    strict = same & (ri % CHUNK > ci % CHUNK)
    incl = same & (ri % CHUNK >= ci % CHUNK)
    eye = ri == ci
    ident = jnp.where(eye, 1.0, 0.0)
    rk = rk_ref[0]
    lng = lng_ref[0]
    lnb = lnb_ref[0]
    chs = range(nc)

    def prepare_stages():
        w = {}

        def prep():
            tri = jnp.where(lax.broadcasted_iota(jnp.int32, (CHUNK, CHUNK), 1)
                            <= lax.broadcasted_iota(jnp.int32, (CHUNK, CHUNK), 0), 1.0, 0.0).astype(BF16)
            lw_wide = jnp.concatenate([lw_ref[0, 0, ch * CHUNK:(ch + 1) * CHUNK, :] for ch in chs], axis=1)
            lw_hi = lw_wide.astype(BF16)
            lw_lo = (lw_wide - lw_hi.astype(F32)).astype(BF16)
            c_wide = (jnp.dot(tri, lw_hi, preferred_element_type=F32)
                      + jnp.dot(tri, lw_lo, preferred_element_type=F32))
            for ch in chs:
                sl = slice(ch * CHUNK, (ch + 1) * CHUNK)
                r = r_ref[0, 0, sl, :]
                k = k_ref[0, 0, sl, :]
                v = v_ref[0, 0, sl, :]
                kk = kk_ref[0, 0, sl, :]
                c = c_wide[:, ch * PAIR:(ch + 1) * PAIR]
                c_last = c[CHUNK - 1:CHUNK, :]
                e_neg = jnp.exp(-c)
                e_last = jnp.exp(c_last - c)
                b = kk * a_ref[0, 0, sl, :]
                rt = _stack(r * jnp.exp(c), first)
                at = _stack(-kk * jnp.exp(c - lw_ref[0, 0, sl, :]), first)
                w[ch] = dict(
                    rt=rt, at=at, vs=_stack(v, first), g_last=jnp.exp(c_last),
                    bonus=_pair_sum(r * k * rk, first) * v,
                    ar=jnp.concatenate([at, rt], axis=0),
                    bkt=jnp.concatenate([_stack(b * e_neg, first), _stack(k * e_neg, first)], axis=0),
                    bk_last=jnp.concatenate([_stack(b * e_last, first), _stack(k * e_last, first)], axis=0))

        def couplings():
            for ch in chs:
                d = w[ch]
                g = _mm_nt(d["ar"], d["bkt"])
                d["n"] = jnp.where(strict, g[:l2, :l2], 0.0)
                d["a_ak"] = jnp.where(strict, g[:l2, l2:], 0.0)
                d["a_rbk"] = jnp.concatenate([jnp.where(incl, g[l2:, :l2], 0.0),
                                              jnp.where(incl, g[l2:, l2:], 0.0)], axis=1)

        def start_inverse():
            for ch in chs:
                d = w[ch]
                d["aakv"] = _mm(d["a_ak"], d["vs"])
                d["tinv"] = ident + d["n"]
                d["pw"] = d["n"]

        def square():
            for ch in chs:
                w[ch]["pw"] = _mm(w[ch]["pw"], w[ch]["pw"])

        def extend():
            for ch in chs:
                w[ch]["tinv"] = w[ch]["tinv"] + _mm(w[ch]["tinv"], w[ch]["pw"])

        def solve():
            for ch in chs:
                d = w[ch]
                x = _mm(d["tinv"], jnp.concatenate([d["at"], d["aakv"]], axis=1))
                d["z"] = jnp.concatenate([x, jnp.concatenate([jnp.zeros_like(d["vs"]), d["vs"]], axis=1)],
                                         axis=0)

        def finish():
            for ch in chs:
                d = w[ch]
                m1 = _mm(d["a_rbk"], d["z"])
                m2 = _mm_tn(d["z"], d["bk_last"])
                carry_sc[ch, 0] = d["rt"] + m1[:, :PAIR]
                carry_sc[ch, 1] = m1[:, PAIR:]
                carry_sc[ch, 2] = jnp.where(eye, d["g_last"], 0.0) + m2[:PAIR]
                carry_sc[ch, 3] = m2[PAIR:]
                bv_sc[ch] = d["bonus"]

        rounds = int(math.log2(CHUNK)) - 1
        return [prep, couplings, start_inverse] + [square, extend] * rounds + [solve, finish]

    def advance(ch, s):
        ys = _mm_nt(carry_sc[ch, 0], s) + carry_sc[ch, 1]
        s = _mm(s, carry_sc[ch, 2]) + carry_sc[ch, 3]
        y = ys[:CHUNK] + ys[CHUNK:]
        mean = _pair_sum(y, first) * (1.0 / HEAD)
        d = y - mean
        var = _pair_sum(d * d, first) * (1.0 / HEAD)
        y_ref[0, 0, ch * CHUNK:(ch + 1) * CHUNK, :] = d * lax.rsqrt(var + RWKV_LN_EPS) * lng + lnb + bv_sc[ch]
        return s

    def carried_state():
        return jnp.where((i - 1) % blocks_per_seq == 0, 0.0, s_sc[...])

    @pl.when(i == 0)
    def _():
        for stage in prepare_stages():
            stage()

    @pl.when((i > 0) & (i < nblk))
    def _():
        s = carried_state()
        stages = prepare_stages()
        slots = len(stages) - 1
        for n_stage, stage in enumerate(stages):
            stage()
            for ch in chs:
                if ch * slots // nc == n_stage:
                    s = advance(ch, s)
        s_sc[...] = s

    @pl.when(i == nblk)
    def _():
        s = carried_state()
        for ch in chs:
            s = advance(ch, s)
        s_sc[...] = s


def _wkv_scan(r, lw, k, v, kk, a, r_k, ln_g, ln_b, *, lb=1024):
    batch, npair, t, _ = r.shape
    lb = min(lb, t)
    bps = t // lb
    nblk = batch * npair * bps
    nc = lb // CHUNK

    def block_of(n):
        seq = n // bps
        return seq // npair, seq % npair, n % bps, 0

    prepared = lambda i: jnp.minimum(i, nblk - 1)
    advanced = lambda i: jnp.maximum(i - 1, 0)
    in_spec = pl.BlockSpec((1, 1, lb, PAIR), lambda i: block_of(prepared(i)))
    out_spec = pl.BlockSpec((1, 1, lb, PAIR), lambda i: block_of(advanced(i)))
    prep_par = pl.BlockSpec((1, 1, PAIR), lambda i: (block_of(prepared(i))[1], 0, 0))
    adv_par = pl.BlockSpec((1, 1, PAIR), lambda i: (block_of(advanced(i))[1], 0, 0))
    par = lambda x: x.reshape(npair, 1, PAIR)
    return pl.pallas_call(
        functools.partial(_wkv_kernel, nblk=nblk, blocks_per_seq=bps),
        grid=(nblk + 1,),
        in_specs=[in_spec] * 6 + [prep_par, adv_par, adv_par],
        out_specs=out_spec,
        out_shape=jax.ShapeDtypeStruct(r.shape, F32),
        scratch_shapes=[pltpu.VMEM((PAIR, PAIR), F32), pltpu.VMEM((nc, 4, PAIR, PAIR), F32),
                        pltpu.VMEM((nc, CHUNK, PAIR), F32)],
        compiler_params=_cparams(("arbitrary",)),
        name="wkv_scan",
    )(r, lw, k, v, kk, a, par(r_k), par(ln_g), par(ln_b))


def _sublane_all(op, x):
    shift = SUBLANES // 2
    while shift:
        x = op(x, pltpu.roll(x, shift, 0))
        shift //= 2
    return x


def _diff_attn_kernel(q_ref, k_ref, vt_ref, lam_ref, sg_ref, o_ref, m_sc, acc_sc, sa_sc, sb_sc, qs_sc, *,
                      lam_init, cg, n_heads, unroll):
    bq = vt_ref.shape[4]
    dv = LANES
    rows = vt_ref.shape[3]
    ncg = 2 * bq // cg
    n_pairs = q_ref.shape[2] // (2 * bq)
    head = (pl.program_id(1) + 1).astype(F32)
    block_bias = jnp.exp2(jnp.full((SUBLANES, cg), -8.0 / n_heads, F32) * head) * (LOG2E * bq)
    first_blk = list(range(ncg))
    second_blk = list(range(ncg, 2 * ncg))
    both = first_blk + second_blk
    q0 = [(g * cg) % bq for g in both]
    bufs = (sa_sc, sb_sc)

    def stack_queries(i):
        first = _half_mask((bq, LANES))
        lane = lax.broadcasted_iota(jnp.int32, (bq, LANES), 1)
        feat = jnp.zeros((bq, LANES), F32)
        for n, part in enumerate(LOG2E_PARTS):
            feat = jnp.where(lane // 2 == n, part, feat)
        feat = feat.astype(BF16)
        for blk in range(2):
            q = q_ref[0, 0, pl.ds(pl.multiple_of((2 * i + blk) * bq, bq), bq), :]
            zero = jnp.zeros_like(q)
            qs_sc[2 * blk * bq:(2 * blk + 1) * bq] = jnp.concatenate([jnp.where(first, q, zero), feat], axis=1)
            qs_sc[(2 * blk + 1) * bq:(2 * blk + 2) * bq] = jnp.concatenate([jnp.where(first, zero, q), feat], axis=1)

    def scores(j, buf, gs):
        kt = k_ref[0, 0, pl.ds(pl.multiple_of(j * bq, bq), bq), :]
        for g in gs:
            buf[g] = lax.dot_general(kt, qs_sc[g * cg:(g + 1) * cg], (((1,), (1,)), ((), ())),
                                     preferred_element_type=F32)

    def pair(i, carry):
        m_sc[...] = jnp.full_like(m_sc, NEG_BIG)
        acc_sc[...] = jnp.zeros_like(acc_sc)

        def tile(j, buf, full, diag):
            p, alpha, nk = {}, {}, {}
            for g in list(full) + list(diag):
                on_diag = g in diag
                nk[g] = min(bq, q0[g] + cg) if on_diag else bq
                sg = buf[g, :nk[g], :]
                if on_diag:
                    key = lax.broadcasted_iota(jnp.int32, sg.shape, 0)
                    qry = lax.broadcasted_iota(jnp.int32, sg.shape, 1) + q0[g]
                    sg = jnp.where(key <= qry, sg, NEG_BIG)
                    offset = 0.0
                else:
                    offset = block_bias * lax.convert_element_type(j - (2 * i + g // ncg), F32)
                sg = sg.reshape(nk[g] // SUBLANES, SUBLANES, cg)
                m_prev = m_sc[g]
                m_new = jnp.maximum(m_prev, _sublane_all(jnp.maximum, jnp.max(sg, axis=0)) + offset)
                p[g] = jnp.exp2(sg - (m_new - offset)[None]).reshape(nk[g], cg).astype(BF16)
                alpha[g] = jnp.exp2(m_prev - m_new)
                m_sc[g] = m_new
            pv = {g: jnp.dot(vt_ref[0, 0, j, :, :nk[g]], p[g], preferred_element_type=F32) for g in p}
            for g in p:
                acc_sc[g] = (alpha[g][None] * acc_sc[g].reshape(rows // SUBLANES, SUBLANES, cg)
                             + pv[g].reshape(rows // SUBLANES, SUBLANES, cg)).reshape(rows, cg)

        def run(first_block, count, with_tail):
            for n in range(count):
                scores(first_block + n + 1, bufs[(n + 1) % 2], both)
                tile(first_block + n, bufs[n % 2], both, [])
            if with_tail:
                scores(2 * i + 1, bufs[(count + 1) % 2], second_blk)
                tile(2 * i, bufs[count % 2], second_blk, first_blk)
                tile(2 * i + 1, bufs[(count + 1) % 2], [], second_blk)

        def body(jj, carry):
            run(unroll * jj, unroll, False)
            return carry

        n_full = 2 * i
        lax.fori_loop(0, n_full // unroll, body, 0)
        for rem in range(0, unroll, 2):
            @pl.when(n_full % unroll == rem)
            def _():
                run(n_full - rem, rem, True)

        stack_queries(jnp.minimum(i + 1, n_pairs - 1))
        scores(0, sa_sc, both)

        lam = lam_ref[...]
        lam_full = (jnp.exp(jnp.sum(lam[0:1] * lam[1:2], axis=-1, keepdims=True))
                    - jnp.exp(jnp.sum(lam[2:3] * lam[3:4], axis=-1, keepdims=True)) + lam_init)
        for blk, gs in enumerate((first_blk, second_blk)):
            o_t = []
            for g in gs:
                l = acc_sc[g, dv:dv + SUBLANES, :]
                o_t.append((acc_sc[g, :dv, :].reshape(dv // SUBLANES, SUBLANES, cg) / l[None]).reshape(dv, cg))
            half = ncg // 2
            o1 = jnp.concatenate(o_t[:half], axis=1)
            o2 = jnp.concatenate(o_t[half:], axis=1)
            o = (o1 - lam_full * o2).T
            o_ref[0, 0, pl.ds(pl.multiple_of((2 * i + blk) * bq, bq), bq), :] = (
                _rms(o, sg_ref[...], SUBLN_EPS) * (1.0 - lam_init)).astype(BF16)
        return carry

    stack_queries(0)
    scores(0, sa_sc, both)
    lax.fori_loop(0, n_pairs, pair, 0)


def _diff_attn(q, k, vt, lam, subln, lam_init, *, unroll=8):
    assert unroll % 2 == 0
    batch, nh, t, _ = q.shape
    rows, bq = vt.shape[-2:]
    assert (t // bq) % 2 == 0
    cg = min(MXU_WIDTH, bq)
    ncg = 4 * bq // cg
    return pl.pallas_call(
        functools.partial(_diff_attn_kernel, lam_init=lam_init, cg=cg, n_heads=nh, unroll=unroll),
        grid=(batch, nh),
        in_specs=[pl.BlockSpec((1, 1, t, LANES), lambda b, h: (b, h, 0, 0)),
                  pl.BlockSpec((1, 1, t, 2 * LANES), lambda b, h: (b, h, 0, 0)),
                  pl.BlockSpec((1, 1, t // bq, rows, bq), lambda b, h: (b, h, 0, 0, 0)),
                  _const_spec(lam.shape), _const_spec((1, LANES))],
        out_specs=pl.BlockSpec((1, 1, t, LANES), lambda b, h: (b, h, 0, 0)),
        out_shape=jax.ShapeDtypeStruct((batch, nh, t, LANES), BF16),
        scratch_shapes=[pltpu.VMEM((ncg, SUBLANES, cg), F32), pltpu.VMEM((ncg, rows, cg), F32),
                        pltpu.VMEM((ncg, bq, cg), F32), pltpu.VMEM((ncg, bq, cg), F32),
                        pltpu.VMEM((4 * bq, 2 * LANES), BF16)],
        compiler_params=_cparams(("parallel", "parallel")),
        name="diff_attn",
    )(q, k, vt, lam, subln.reshape(1, LANES))


def kernel(x, ffn_norm, ffn_w_in, ffn_w_out, mix_norm, rwkv_mu, rwkv_w_rkv, rwkv_w0, rwkv_w1, rwkv_w2, rwkv_a0, rwkv_a1, rwkv_a2, rwkv_g1, rwkv_g2, rwkv_k_k, rwkv_k_a, rwkv_r_k, rwkv_ln_g, rwkv_ln_b, rwkv_w_o, kv_norm, w_kv, k_norm, diff_w_q, diff_q_norm, diff_lambda, diff_subln, diff_w_o):
    batch, t, c = x.shape
    x = x.reshape(batch * t, c)
    w_in, w_out = ffn_w_in.astype(BF16), ffn_w_out.astype(BF16)

    x, = _block(x, batch, ffn_norm[0, 0], w_in, w_out, (0, 0))
    r, lw, k, v, kk, a, g = _rwkv_pre(x, batch, mix_norm[0], rwkv_mu[0], rwkv_w_rkv[0], rwkv_w0[0], rwkv_w1[0],
                                      rwkv_w2[0], rwkv_a0[0], rwkv_a1[0], rwkv_a2[0], rwkv_g1[0], rwkv_g2[0],
                                      rwkv_k_k[0], rwkv_k_a[0])
    y = _wkv_scan(r, lw, k, v, kk, a, rwkv_r_k[0], rwkv_ln_g[0], rwkv_ln_b[0])
    x, k_sh, v_sh = _block(x, batch, ffn_norm[0, 1], w_in, w_out, (0, 1),
                           pre=(y, g, rwkv_w_o[0]), post=("kv", kv_norm, w_kv, k_norm))

    x, q = _block(x, batch, ffn_norm[1, 0], w_in, w_out, (1, 0),
                  post=("q", mix_norm[1], diff_w_q[0], diff_q_norm[0]))
    lam_init = 0.8 - 0.6 * math.exp(-0.3 * 1)
    o = _diff_attn(q, k_sh, v_sh, diff_lambda[0], diff_subln[0], lam_init)
    x, = _block(x, batch, ffn_norm[1, 1], w_in, w_out, (1, 1), pre=(o, None, diff_w_o[0]))
    return x.reshape(batch, t, c)
```

```python
import functools
import math

import jax
import jax.numpy as jnp
from jax import lax
from jax.experimental import pallas as pl
from jax.experimental.pallas import tpu as pltpu

F32 = jnp.float32
BF16 = jnp.bfloat16

LANES = 128
SUBLANES = 8
MXU_WIDTH = 256
HEAD = 64
PAIR = 2 * HEAD
CHUNK = 64
ATT = 512
FFN_TF = MXU_WIDTH
NORM_EPS = 1e-6
SUBLN_EPS = 1e-5
RWKV_LN_EPS = 64e-5
VMEM_LIMIT = 56 * 1024 * 1024
NEG_BIG = -1e30
BF16_ROWS = 16
VT_ROWS = LANES + BF16_ROWS


def _bf16_parts(x, n):
    parts = []
    for _ in range(n):
        m, e = math.frexp(x)
        p = math.ldexp(round(m * 256.0) / 256.0, e)
        parts.append(p)
        x -= p
    return tuple(parts)


LOG2E = math.log2(math.e)
LOG2E_PARTS = _bf16_parts(LOG2E, 3)


def _cparams(sem):
    return pltpu.CompilerParams(dimension_semantics=sem, vmem_limit_bytes=VMEM_LIMIT)


def _const_spec(shape, lead=()):
    nd = len(shape)
    return pl.BlockSpec((None,) * len(lead) + tuple(shape), lambda *_: tuple(lead) + (0,) * nd,
                        pipeline_mode=pl.Buffered(1))


def _mm(a, b):
    return jnp.dot(a.astype(BF16), b.astype(BF16), preferred_element_type=F32)


def _mm_nt(a, b):
    return lax.dot_general(a.astype(BF16), b.astype(BF16), (((1,), (1,)), ((), ())),
                           preferred_element_type=F32)


def _mm_tn(a, b):
    return lax.dot_general(a.astype(BF16), b.astype(BF16), (((0,), (0,)), ((), ())),
                           preferred_element_type=F32)


def _rms(x, g, eps):
    return x * lax.rsqrt(jnp.mean(x * x, axis=-1, keepdims=True) + eps) * g


def _half_mask(shape):
    return lax.broadcasted_iota(jnp.int32, shape, len(shape) - 1) % PAIR < HEAD


def _pair_sum(x, first):
    s1 = jnp.sum(jnp.where(first, x, 0.0), axis=-1, keepdims=True)
    s2 = jnp.sum(jnp.where(first, 0.0, x), axis=-1, keepdims=True)
    return jnp.where(first, s1, s2)


def _ffn_value(x, g_ref, win_ref, wout_ref, h_sc):
    d_ff = wout_ref.shape[0]
    xn = _rms(x, g_ref[...], NORM_EPS).astype(BF16)
    def gate_up(f0):
        return (jnp.dot(xn, win_ref[:, f0:f0 + FFN_TF], preferred_element_type=F32),
                jnp.dot(xn, win_ref[:, d_ff + f0:d_ff + f0 + FFN_TF], preferred_element_type=F32))

    starts = list(range(0, d_ff, FFN_TF))
    pending = gate_up(starts[0])
    for n, f0 in enumerate(starts):
        gate, up = pending
        if n + 1 < len(starts):
            pending = gate_up(starts[n + 1])
        h_sc[:, f0:f0 + FFN_TF] = (gate * jax.nn.sigmoid(gate) * up).astype(BF16)
    return x + 0.5 * jnp.dot(h_sc[...], wout_ref[...], preferred_element_type=F32)


def _keys_values(x, ng_ref, w_ref, kg_ref, k_out, v_out):
    tm, c = x.shape
    n_heads = c // LANES
    kv = _mm(_rms(x, ng_ref[...], NORM_EPS), w_ref[...])
    first = _half_mask((tm, LANES))
    kg = kg_ref[...]
    lane = lax.broadcasted_iota(jnp.int32, (tm, LANES), 1)
    pos = lax.broadcasted_iota(jnp.int32, (tm, LANES), 0)
    base = jnp.where(lane >= 2 * len(LOG2E_PARTS), 0.0,
                     jnp.where(lane % 2 == 0, (pos // HEAD * HEAD).astype(F32), (pos % HEAD).astype(F32)))
    ones = jnp.ones((VT_ROWS - LANES, tm), BF16)
    for h in range(n_heads):
        kh = kv[:, h * LANES:(h + 1) * LANES]
        ms = _pair_sum(kh * kh, first) * (1.0 / HEAD)
        kn = kh * lax.rsqrt(ms + NORM_EPS) * kg
        slope = 2.0 ** (-8.0 * (h + 1) / n_heads)
        k_out[0, h] = jnp.concatenate([kn.astype(BF16), (base * slope).astype(BF16)], axis=-1)
        vt = kv[:, c + h * LANES:c + (h + 1) * LANES].T.astype(BF16)
        v_out[0, h, 0] = jnp.concatenate([vt, ones], axis=0)


def _queries(x, ng_ref, w_ref, qg_ref, q_out):
    tm, c = x.shape
    q = _mm(_rms(x, ng_ref[...], NORM_EPS), w_ref[...])
    first = _half_mask((tm, LANES))
    qg = qg_ref[...] * (HEAD ** -0.5 * LOG2E)
    for h in range(c // LANES):
        qh = q[:, h * LANES:(h + 1) * LANES]
        ms = _pair_sum(qh * qh, first) * (1.0 / HEAD)
        q_out[0, h] = (qh * lax.rsqrt(ms + NORM_EPS) * qg).astype(BF16)


def _block_kernel(*refs, pre, post):
    refs = list(refs)
    x = refs.pop(0)[...]
    if pre is not None:
        y_ref = refs.pop(0)
        y = jnp.concatenate([y_ref[0, p] for p in range(y_ref.shape[1])], axis=-1)
        if pre == "gated":
            y = y * refs.pop(0)[...]
        x = x + jnp.dot(y.astype(BF16), refs.pop(0)[...], preferred_element_type=F32)
    g_ref, win_ref, wout_ref = refs[:3]
    refs = refs[3:]
    post_refs = [refs.pop(0) for _ in range(3)] if post is not None else []
    o_ref = refs.pop(0)
    h_sc = refs.pop()
    x = _ffn_value(x, g_ref, win_ref, wout_ref, h_sc)
    o_ref[...] = x
    if post == "kv":
        _keys_values(x, *post_refs, *refs)
    elif post == "q":
        _queries(x, *post_refs, *refs)


def _block(x, batch, g, w_in, w_out, which, *, pre=None, post=None):
    m, c = x.shape
    t = m // batch
    d_ff = w_out.shape[-2]
    tm = min(ATT if post is not None and post[0] == "kv" else 2 * ATT, t)
    bps = t // tm
    nh = c // LANES
    row_spec = pl.BlockSpec((tm, c), lambda i: (i, 0))
    head_idx = lambda i: (i // bps, 0, i % bps, 0)
    args, in_specs = [x], [row_spec]
    pre_kind = None
    if pre is not None:
        y, gate, w_o = pre
        pre_kind = "plain" if gate is None else "gated"
        args.append(y)
        in_specs.append(pl.BlockSpec((1, y.shape[1], tm, LANES), head_idx))
        if gate is not None:
            args.append(gate)
            in_specs.append(row_spec)
        args.append(w_o.astype(BF16))
        in_specs.append(_const_spec((c, c)))
    args += [g.reshape(1, c), w_in, w_out]
    in_specs += [_const_spec((1, c)), _const_spec((c, 2 * d_ff), which), _const_spec((d_ff, c), which)]
    out_specs, out_shape = [row_spec], [jax.ShapeDtypeStruct((m, c), F32)]
    post_kind = None
    if post is not None:
        post_kind, ng, w, head_norm = post
        args += [ng.reshape(1, c), w.astype(BF16), jnp.concatenate([head_norm, head_norm]).reshape(1, LANES)]
        in_specs += [_const_spec((1, c)), _const_spec(w.shape), _const_spec((1, LANES))]
        if post_kind == "kv":
            out_specs += [pl.BlockSpec((1, nh, tm, 2 * LANES), head_idx),
                          pl.BlockSpec((1, nh, 1, VT_ROWS, tm), lambda i: (i // bps, 0, i % bps, 0, 0))]
            out_shape += [jax.ShapeDtypeStruct((batch, nh, t, 2 * LANES), BF16),
                          jax.ShapeDtypeStruct((batch, nh, t // tm, VT_ROWS, tm), BF16)]
        else:
            out_specs.append(pl.BlockSpec((1, nh, tm, LANES), head_idx))
            out_shape.append(jax.ShapeDtypeStruct((batch, nh, t, LANES), BF16))
    return pl.pallas_call(
        functools.partial(_block_kernel, pre=pre_kind, post=post_kind),
        grid=(m // tm,),
        in_specs=in_specs,
        out_specs=out_specs,
        out_shape=out_shape,
        scratch_shapes=[pltpu.VMEM((tm, d_ff), BF16)],
        compiler_params=_cparams(("parallel",)),
        name="block_" + (pre_kind or "x") + "_" + (post_kind or "x"),
    )(*args)


def _rwkv_pre_kernel(x_ref, xp_ref, ng_ref, mu_ref, wrkv_ref, w0_ref, w1_ref, w2_ref, a0_ref, a1_ref,
                     a2_ref, g1_ref, g2_ref, kk_ref, ka_ref,
                     r_out, lw_out, k_out, v_out, kk_out, a_out, g_out, *, blocks_per_seq):
    tm, c = x_ref.shape
    ng = ng_ref[...]
    h = _rms(x_ref[...], ng, NORM_EPS)
    prev = _rms(xp_ref[SUBLANES - 1:SUBLANES, :], ng, NORM_EPS)
    prev = jnp.where(pl.program_id(0) % blocks_per_seq == 0, 0.0, prev)
    row = lax.broadcasted_iota(jnp.int32, (tm, c), 0)
    h_prev = jnp.where(row == 0, prev, pltpu.roll(h, 1, 0))
    dx = h_prev - h

    def mix(i):
        return h + dx * mu_ref[i:i + 1, :]

    r = _mm(mix(0), wrkv_ref[0])
    k = _mm(mix(1), wrkv_ref[1])
    v = _mm(mix(2), wrkv_ref[2])
    z = w0_ref[...] + _mm(jnp.tanh(_mm(mix(3), w1_ref[...])), w2_ref[...])
    lw = -math.exp(-0.5) * jax.nn.sigmoid(z)
    a = jax.nn.sigmoid(a0_ref[...] + _mm(_mm(mix(4), a1_ref[...]), a2_ref[...]))
    g_out[...] = _mm(jax.nn.sigmoid(_mm(mix(5), g1_ref[...])), g2_ref[...])
    kk = k * kk_ref[...]
    k = k * (1.0 + (a - 1.0) * ka_ref[...])
    first = _half_mask((tm, PAIR))
    for p in range(c // PAIR):
        sl = slice(p * PAIR, (p + 1) * PAIR)
        kkp = kk[:, sl]
        ss = _pair_sum(kkp * kkp, first)
        r_out[0, p] = r[:, sl]
        lw_out[0, p] = lw[:, sl]
        k_out[0, p] = k[:, sl]
        v_out[0, p] = v[:, sl]
        kk_out[0, p] = kkp * lax.rsqrt(jnp.maximum(ss, 1e-24))
        a_out[0, p] = a[:, sl]


def _rwkv_pre(x, batch, ng, mu, w_rkv, w0, w1, w2, a0, a1, a2, g1, g2, k_k, k_a, *, tm=512):
    m, c = x.shape
    t = m // batch
    tm = min(tm, t)
    bps = t // tm
    npair = c // PAIR
    row = lambda a: a.reshape(1, c)
    pair_spec = pl.BlockSpec((1, npair, tm, PAIR), lambda i: (i // bps, 0, i % bps, 0))
    pair_shape = jax.ShapeDtypeStruct((batch, npair, t, PAIR), F32)
    ws = [w_rkv.astype(BF16), row(w0), w1.astype(BF16), w2.astype(BF16), row(a0), a1.astype(BF16),
          a2.astype(BF16), g1.astype(BF16), g2.astype(BF16), row(k_k), row(k_a)]
    return pl.pallas_call(
        functools.partial(_rwkv_pre_kernel, blocks_per_seq=bps),
        grid=(m // tm,),
        in_specs=[pl.BlockSpec((tm, c), lambda i: (i, 0)),
                  pl.BlockSpec((SUBLANES, c), lambda i: (jnp.maximum(i * (tm // SUBLANES) - 1, 0), 0)),
                  _const_spec((1, c)), _const_spec(mu.shape)] + [_const_spec(w.shape) for w in ws],
        out_specs=[pair_spec] * 6 + [pl.BlockSpec((tm, c), lambda i: (i, 0))],
        out_shape=[pair_shape] * 6 + [jax.ShapeDtypeStruct((m, c), F32)],
        compiler_params=_cparams(("parallel",)),
        name="rwkv_pre",
    )(x, x, row(ng), mu, *ws)


def _stack(x, first):
    return jnp.concatenate([jnp.where(first, x, 0.0), jnp.where(first, 0.0, x)], axis=0)


def _wkv_kernel(r_ref, lw_ref, k_ref, v_ref, kk_ref, a_ref, rk_ref, lng_ref, lnb_ref, y_ref,
                s_sc, carry_sc, bv_sc, *, nblk, blocks_per_seq):
    lb = r_ref.shape[2]
    nc = lb // CHUNK
    l2 = 2 * CHUNK
    i = pl.program_id(0)

    first = _half_mask((CHUNK, PAIR))
    ri = lax.broadcasted_iota(jnp.int32, (l2, l2), 0)
    ci = lax.broadcasted_iota(jnp.int32, (l2, l2), 1)
    same = (ri // CHUNK) == (ci // CHUNK)
    strict = same & (ri % CHUNK > ci % CHUNK)
    incl = same & (ri % CHUNK >= ci % CHUNK)
    eye = ri == ci
    ident = jnp.where(eye, 1.0, 0.0)
    rk = rk_ref[0]
    lng = lng_ref[0]
    lnb = lnb_ref[0]
    chs = range(nc)

    def prepare_stages():
        w = {}

        def prep():
            tri = jnp.where(lax.broadcasted_iota(jnp.int32, (CHUNK, CHUNK), 1)
                            <= lax.broadcasted_iota(jnp.int32, (CHUNK, CHUNK), 0), 1.0, 0.0).astype(BF16)
            lw_wide = jnp.concatenate([lw_ref[0, 0, ch * CHUNK:(ch + 1) * CHUNK, :] for ch in chs], axis=1)
            lw_hi = lw_wide.astype(BF16)
            lw_lo = (lw_wide - lw_hi.astype(F32)).astype(BF16)
            c_wide = (jnp.dot(tri, lw_hi, preferred_element_type=F32)
                      + jnp.dot(tri, lw_lo, preferred_element_type=F32))
            for ch in chs:
                sl = slice(ch * CHUNK, (ch + 1) * CHUNK)
                r = r_ref[0, 0, sl, :]
                k = k_ref[0, 0, sl, :]
                v = v_ref[0, 0, sl, :]
                kk = kk_ref[0, 0, sl, :]
                c = c_wide[:, ch * PAIR:(ch + 1) * PAIR]
                c_last = c[CHUNK - 1:CHUNK, :]
                e_neg = jnp.exp(-c)
                e_last = jnp.exp(c_last - c)
                b = kk * a_ref[0, 0, sl, :]
                rt = _stack(r * jnp.exp(c), first)
                at = _stack(-kk * jnp.exp(c - lw_ref[0, 0, sl, :]), first)
                w[ch] = dict(
                    rt=rt, at=at, vs=_stack(v, first), g_last=jnp.exp(c_last),
                    bonus=_pair_sum(r * k * rk, first) * v,
                    ar=jnp.concatenate([at, rt], axis=0),
                    bkt=jnp.concatenate([_stack(b * e_neg, first), _stack(k * e_neg, first)], axis=0),
                    bk_last=jnp.concatenate([_stack(b * e_last, first), _stack(k * e_last, first)], axis=0))

        def couplings():
            for ch in chs:
                d = w[ch]
                g = _mm_nt(d["ar"], d["bkt"])
                d["n"] = jnp.where(strict, g[:l2, :l2], 0.0)
                d["a_ak"] = jnp.where(strict, g[:l2, l2:], 0.0)
                d["a_rbk"] = jnp.concatenate([jnp.where(incl, g[l2:, :l2], 0.0),
                                              jnp.where(incl, g[l2:, l2:], 0.0)], axis=1)

        def start_inverse():
            for ch in chs:
                d = w[ch]
                d["aakv"] = _mm(d["a_ak"], d["vs"])
                d["tinv"] = ident + d["n"]
                d["pw"] = d["n"]

        def square():
            for ch in chs:
                w[ch]["pw"] = _mm(w[ch]["pw"], w[ch]["pw"])

        def extend():
            for ch in chs:
                w[ch]["tinv"] = w[ch]["tinv"] + _mm(w[ch]["tinv"], w[ch]["pw"])

        def solve():
            for ch in chs:
                d = w[ch]
                x = _mm(d["tinv"], jnp.concatenate([d["at"], d["aakv"]], axis=1))
                d["z"] = jnp.concatenate([x, jnp.concatenate([jnp.zeros_like(d["vs"]), d["vs"]], axis=1)],
                                         axis=0)

        def finish():
            for ch in chs:
                d = w[ch]
                m1 = _mm(d["a_rbk"], d["z"])
                m2 = _mm_tn(d["z"], d["bk_last"])
                carry_sc[ch, 0] = d["rt"] + m1[:, :PAIR]
                carry_sc[ch, 1] = m1[:, PAIR:]
                carry_sc[ch, 2] = jnp.where(eye, d["g_last"], 0.0) + m2[:PAIR]
                carry_sc[ch, 3] = m2[PAIR:]
                bv_sc[ch] = d["bonus"]

        rounds = int(math.log2(CHUNK)) - 1
        return [prep, couplings, start_inverse] + [square, extend] * rounds + [solve, finish]

    def advance(ch, s):
        ys = _mm_nt(carry_sc[ch, 0], s) + carry_sc[ch, 1]
        s = _mm(s, carry_sc[ch, 2]) + carry_sc[ch, 3]
        y = ys[:CHUNK] + ys[CHUNK:]
        mean = _pair_sum(y, first) * (1.0 / HEAD)
        d = y - mean
        var = _pair_sum(d * d, first) * (1.0 / HEAD)
        y_ref[0, 0, ch * CHUNK:(ch + 1) * CHUNK, :] = d * lax.rsqrt(var + RWKV_LN_EPS) * lng + lnb + bv_sc[ch]
        return s

    def carried_state():
        return jnp.where((i - 1) % blocks_per_seq == 0, 0.0, s_sc[...])

    @pl.when(i == 0)
    def _():
        for stage in prepare_stages():
            stage()

    @pl.when((i > 0) & (i < nblk))
    def _():
        s = carried_state()
        stages = prepare_stages()
        slots = len(stages) - 1
        for n_stage, stage in enumerate(stages):
            stage()
            for ch in chs:
                if ch * slots // nc == n_stage:
                    s = advance(ch, s)
        s_sc[...] = s

    @pl.when(i == nblk)
    def _():
        s = carried_state()
        for ch in chs:
            s = advance(ch, s)
        s_sc[...] = s


def _wkv_scan(r, lw, k, v, kk, a, r_k, ln_g, ln_b, *, lb=1024):
    batch, npair, t, _ = r.shape
    lb = min(lb, t)
    bps = t // lb
    nblk = batch * npair * bps
    nc = lb // CHUNK

    def block_of(n):
        seq = n // bps
        return seq // npair, seq % npair, n % bps, 0

    prepared = lambda i: jnp.minimum(i, nblk - 1)
    advanced = lambda i: jnp.maximum(i - 1, 0)
    in_spec = pl.BlockSpec((1, 1, lb, PAIR), lambda i: block_of(prepared(i)))
    out_spec = pl.BlockSpec((1, 1, lb, PAIR), lambda i: block_of(advanced(i)))
    prep_par = pl.BlockSpec((1, 1, PAIR), lambda i: (block_of(prepared(i))[1], 0, 0))
    adv_par = pl.BlockSpec((1, 1, PAIR), lambda i: (block_of(advanced(i))[1], 0, 0))
    par = lambda x: x.reshape(npair, 1, PAIR)
    return pl.pallas_call(
        functools.partial(_wkv_kernel, nblk=nblk, blocks_per_seq=bps),
        grid=(nblk + 1,),
        in_specs=[in_spec] * 6 + [prep_par, adv_par, adv_par],
        out_specs=out_spec,
        out_shape=jax.ShapeDtypeStruct(r.shape, F32),
        scratch_shapes=[pltpu.VMEM((PAIR, PAIR), F32), pltpu.VMEM((nc, 4, PAIR, PAIR), F32),
                        pltpu.VMEM((nc, CHUNK, PAIR), F32)],
        compiler_params=_cparams(("arbitrary",)),
        name="wkv_scan",
    )(r, lw, k, v, kk, a, par(r_k), par(ln_g), par(ln_b))


def _sublane_all(op, x):
    shift = SUBLANES // 2
    while shift:
        x = op(x, pltpu.roll(x, shift, 0))
        shift //= 2
    return x


def _diff_attn_kernel(q_ref, k_ref, vt_ref, lam_ref, sg_ref, o_ref, m_sc, acc_sc, sa_sc, sb_sc, qs_sc, *,
                      lam_init, cg, n_heads, unroll):
    bq = vt_ref.shape[4]
    dv = LANES
    rows = vt_ref.shape[3]
    ncg = 2 * bq // cg
    n_pairs = q_ref.shape[2] // (2 * bq)
    head = (pl.program_id(1) + 1).astype(F32)
    block_bias = jnp.exp2(jnp.full((SUBLANES, cg), -8.0 / n_heads, F32) * head) * (LOG2E * bq)
    first_blk = list(range(ncg))
    second_blk = list(range(ncg, 2 * ncg))
    both = first_blk + second_blk
    q0 = [(g * cg) % bq for g in both]
    bufs = (sa_sc, sb_sc)

    def stack_queries(i):
        first = _half_mask((bq, LANES))
        lane = lax.broadcasted_iota(jnp.int32, (bq, LANES), 1)
        feat = jnp.zeros((bq, LANES), F32)
        for n, part in enumerate(LOG2E_PARTS):
            feat = jnp.where(lane // 2 == n, part, feat)
        feat = feat.astype(BF16)
        for blk in range(2):
            q = q_ref[0, 0, pl.ds(pl.multiple_of((2 * i + blk) * bq, bq), bq), :]
            zero = jnp.zeros_like(q)
            qs_sc[2 * blk * bq:(2 * blk + 1) * bq] = jnp.concatenate([jnp.where(first, q, zero), feat], axis=1)
            qs_sc[(2 * blk + 1) * bq:(2 * blk + 2) * bq] = jnp.concatenate([jnp.where(first, zero, q), feat], axis=1)

    def scores(j, buf, gs):
        kt = k_ref[0, 0, pl.ds(pl.multiple_of(j * bq, bq), bq), :]
        for g in gs:
            buf[g] = lax.dot_general(kt, qs_sc[g * cg:(g + 1) * cg], (((1,), (1,)), ((), ())),
                                     preferred_element_type=F32)

    def pair(i, carry):
        m_sc[...] = jnp.full_like(m_sc, NEG_BIG)
        acc_sc[...] = jnp.zeros_like(acc_sc)

        def tile(j, buf, full, diag):
            p, alpha, nk = {}, {}, {}
            for g in list(full) + list(diag):
                on_diag = g in diag
                nk[g] = min(bq, q0[g] + cg) if on_diag else bq
                sg = buf[g, :nk[g], :]
                if on_diag:
                    key = lax.broadcasted_iota(jnp.int32, sg.shape, 0)
                    qry = lax.broadcasted_iota(jnp.int32, sg.shape, 1) + q0[g]
                    sg = jnp.where(key <= qry, sg, NEG_BIG)
                    offset = 0.0
                else:
                    offset = block_bias * lax.convert_element_type(j - (2 * i + g // ncg), F32)
                sg = sg.reshape(nk[g] // SUBLANES, SUBLANES, cg)
                m_prev = m_sc[g]
                m_new = jnp.maximum(m_prev, _sublane_all(jnp.maximum, jnp.max(sg, axis=0)) + offset)
                p[g] = jnp.exp2(sg - (m_new - offset)[None]).reshape(nk[g], cg).astype(BF16)
                alpha[g] = jnp.exp2(m_prev - m_new)
                m_sc[g] = m_new
            pv = {g: jnp.dot(vt_ref[0, 0, j, :, :nk[g]], p[g], preferred_element_type=F32) for g in p}
            for g in p:
                acc_sc[g] = (alpha[g][None] * acc_sc[g].reshape(rows // SUBLANES, SUBLANES, cg)
                             + pv[g].reshape(rows // SUBLANES, SUBLANES, cg)).reshape(rows, cg)

        def run(first_block, count, with_tail):
            for n in range(count):
                scores(first_block + n + 1, bufs[(n + 1) % 2], both)
                tile(first_block + n, bufs[n % 2], both, [])
            if with_tail:
                scores(2 * i + 1, bufs[(count + 1) % 2], second_blk)
                tile(2 * i, bufs[count % 2], second_blk, first_blk)
                tile(2 * i + 1, bufs[(count + 1) % 2], [], second_blk)

        def body(jj, carry):
            run(unroll * jj, unroll, False)
            return carry

        n_full = 2 * i
        lax.fori_loop(0, n_full // unroll, body, 0)
        for rem in range(0, unroll, 2):
            @pl.when(n_full % unroll == rem)
            def _():
                run(n_full - rem, rem, True)

        stack_queries(jnp.minimum(i + 1, n_pairs - 1))
        scores(0, sa_sc, both)

        lam = lam_ref[...]
        lam_full = (jnp.exp(jnp.sum(lam[0:1] * lam[1:2], axis=-1, keepdims=True))
                    - jnp.exp(jnp.sum(lam[2:3] * lam[3:4], axis=-1, keepdims=True)) + lam_init)
        for blk, gs in enumerate((first_blk, second_blk)):
            o_t = []
            for g in gs:
                l = acc_sc[g, dv:dv + SUBLANES, :]
                o_t.append((acc_sc[g, :dv, :].reshape(dv // SUBLANES, SUBLANES, cg) / l[None]).reshape(dv, cg))
            half = ncg // 2
            o1 = jnp.concatenate(o_t[:half], axis=1)
            o2 = jnp.concatenate(o_t[half:], axis=1)
            o = (o1 - lam_full * o2).T
            o_ref[0, 0, pl.ds(pl.multiple_of((2 * i + blk) * bq, bq), bq), :] = (
                _rms(o, sg_ref[...], SUBLN_EPS) * (1.0 - lam_init)).astype(BF16)
        return carry

    stack_queries(0)
    scores(0, sa_sc, both)
    lax.fori_loop(0, n_pairs, pair, 0)


def _diff_attn(q, k, vt, lam, subln, lam_init, *, unroll=8):
    assert unroll % 2 == 0
    batch, nh, t, _ = q.shape
    rows, bq = vt.shape[-2:]
    assert (t // bq) % 2 == 0
    cg = min(MXU_WIDTH, bq)
    ncg = 4 * bq // cg
    return pl.pallas_call(
        functools.partial(_diff_attn_kernel, lam_init=lam_init, cg=cg, n_heads=nh, unroll=unroll),
        grid=(batch, nh),
        in_specs=[pl.BlockSpec((1, 1, t, LANES), lambda b, h: (b, h, 0, 0)),
                  pl.BlockSpec((1, 1, t, 2 * LANES), lambda b, h: (b, h, 0, 0)),
                  pl.BlockSpec((1, 1, t // bq, rows, bq), lambda b, h: (b, h, 0, 0, 0)),
                  _const_spec(lam.shape), _const_spec((1, LANES))],
        out_specs=pl.BlockSpec((1, 1, t, LANES), lambda b, h: (b, h, 0, 0)),
        out_shape=jax.ShapeDtypeStruct((batch, nh, t, LANES), BF16),
        scratch_shapes=[pltpu.VMEM((ncg, SUBLANES, cg), F32), pltpu.VMEM((ncg, rows, cg), F32),
                        pltpu.VMEM((ncg, bq, cg), F32), pltpu.VMEM((ncg, bq, cg), F32),
                        pltpu.VMEM((4 * bq, 2 * LANES), BF16)],
        compiler_params=_cparams(("parallel", "parallel")),
        name="diff_attn",
    )(q, k, vt, lam, subln.reshape(1, LANES))


def kernel(x, ffn_norm, ffn_w_in, ffn_w_out, mix_norm, rwkv_mu, rwkv_w_rkv, rwkv_w0, rwkv_w1, rwkv_w2, rwkv_a0, rwkv_a1, rwkv_a2, rwkv_g1, rwkv_g2, rwkv_k_k, rwkv_k_a, rwkv_r_k, rwkv_ln_g, rwkv_ln_b, rwkv_w_o, kv_norm, w_kv, k_norm, diff_w_q, diff_q_norm, diff_lambda, diff_subln, diff_w_o):
    batch, t, c = x.shape
    x = x.reshape(batch * t, c)
    w_in, w_out = ffn_w_in.astype(BF16), ffn_w_out.astype(BF16)

    x, = _block(x, batch, ffn_norm[0, 0], w_in, w_out, (0, 0))
    r, lw, k, v, kk, a, g = _rwkv_pre(x, batch, mix_norm[0], rwkv_mu[0], rwkv_w_rkv[0], rwkv_w0[0], rwkv_w1[0],
                                      rwkv_w2[0], rwkv_a0[0], rwkv_a1[0], rwkv_a2[0], rwkv_g1[0], rwkv_g2[0],
                                      rwkv_k_k[0], rwkv_k_a[0])
    y = _wkv_scan(r, lw, k, v, kk, a, rwkv_r_k[0], rwkv_ln_g[0], rwkv_ln_b[0])
    x, k_sh, v_sh = _block(x, batch, ffn_norm[0, 1], w_in, w_out, (0, 1),
                           pre=(y, g, rwkv_w_o[0]), post=("kv", kv_norm, w_kv, k_norm))

    x, q = _block(x, batch, ffn_norm[1, 0], w_in, w_out, (1, 0),
                  post=("q", mix_norm[1], diff_w_q[0], diff_q_norm[0]))
    lam_init = 0.8 - 0.6 * math.exp(-0.3 * 1)
    o = _diff_attn(q, k_sh, v_sh, diff_lambda[0], diff_subln[0], lam_init)
    x, = _block(x, batch, ffn_norm[1, 1], w_in, w_out, (1, 1), pre=(o, None, diff_w_o[0]))
    return x.reshape(batch, t, c)
```
